```python
import math
import jax, jax.numpy as jnp
from jax import lax
import numpy as np

D_MODEL = 1024
BATCH = 16
SEQ = 2048
DEPTH = 2

GRID_W = 64
A_HEADS = 4
A_DH = 64
B_HEADS = 8
B_DH = 64
NA_MAX_ROWS = 8
NA_COLS = 16
T5_BUCKETS = 32
T5_MAX_DIST = 128
Q_BLOCK = 128
C_HEADS = 4
C_DK = 128
C_DV = 256
C_GATE_RANK = 16
C_GATE_NORM = 16.0
C_CHUNK = 64
N_GROUPS = 4
EXP_PER_GROUP = 8
N_EXPERTS = N_GROUPS * EXP_PER_GROUP
TOP_K_IN_GROUP = 2
D_EXPERT = 512
EPS = 1e-6

MIX_WIDTH = A_HEADS * 2 * A_DH + B_HEADS * B_DH
EV_IN = 3 * A_HEADS * 2 * A_DH + 3 * B_HEADS * B_DH
OD_SIZES = (C_HEADS * C_DK, C_HEADS * C_DK, C_HEADS * C_DV, C_HEADS * C_DV, C_GATE_RANK, C_GATE_RANK)
OD_IN = sum(OD_SIZES)
N_EVEN = (DEPTH + 1) // 2
N_ODD = DEPTH // 2

kernel_name = "hybrid_diffattn_natten_gla_hmoe"


def rmsnorm(x, gain):
    xf = x.astype(jnp.float32)
    xf = xf * lax.rsqrt(jnp.mean(xf * xf, axis=-1, keepdims=True) + EPS)
    return (xf * gain.astype(jnp.float32)).astype(x.dtype)


def t5_bucket(rel):
    half = T5_BUCKETS // 2
    max_exact = half // 2
    sign_off = jnp.where(rel > 0, half, 0)
    n = jnp.abs(rel)
    nf = jnp.maximum(n, 1).astype(jnp.float32)
    large = max_exact + (jnp.log(nf / max_exact) / math.log(T5_MAX_DIST / max_exact)
                         * (half - max_exact)).astype(jnp.int32)
    large = jnp.minimum(large, half - 1)
    return sign_off + jnp.where(n < max_exact, n, large)


def diff_attention(q, k, v, lam, lam_init, subln, t5_bias):
    b_, h_, _, t_, d = q.shape
    nb = t_ // Q_BLOCK
    qb = jnp.moveaxis((q * d ** -0.5).reshape(b_, h_, 2, nb, Q_BLOCK, d), 3, 0)
    kpos = jnp.arange(t_)

    def block(args):
        q_blk, start = args
        qpos = start + jnp.arange(Q_BLOCK)
        bias = t5_bias[t5_bucket(kpos[None, :] - qpos[:, None])]
        bias = jnp.transpose(bias, (2, 0, 1)).astype(jnp.float32)
        s = jnp.einsum('bhmqd,bhmkd->bhmqk', q_blk, k).astype(jnp.float32) + bias[None, :, None]
        p = jax.nn.softmax(s, axis=-1)
        attn = p[:, :, 0] - lam * p[:, :, 1]
        return jnp.einsum('bhqk,bhkv->bhqv', attn.astype(v.dtype), v)

    o = lax.map(block, (qb, jnp.arange(nb) * Q_BLOCK))
    o = jnp.moveaxis(o, 0, 2).reshape(b_, h_, t_, 2 * d)
    return rmsnorm(o, subln) * (1.0 - lam_init)


def neighbourhood_attention(q, k, v, rpb):
    b_, h_, t_, d = q.shape
    rows = t_ // GRID_W
    kh = min(NA_MAX_ROWS, rows)
    cols = jnp.arange(GRID_W)
    col_start = jnp.clip(cols - NA_COLS // 2, 0, GRID_W - NA_COLS)
    col_win = col_start[:, None] + jnp.arange(NA_COLS)[None, :]
    dc = col_win - cols[:, None] + NA_COLS - 1
    qr = jnp.moveaxis((q * d ** -0.5).reshape(b_, h_, rows, GRID_W, d), 2, 0)

    def row_block(args):
        q_row, r = args
        row_start = jnp.clip(r - kh // 2, 0, rows - kh)
        row_win = row_start + jnp.arange(kh)
        tok = row_win[None, :, None] * GRID_W + col_win[:, None, :]
        k_g = jnp.take(k, tok, axis=2)
        v_g = jnp.take(v, tok, axis=2)
        dr = row_win - r + NA_MAX_ROWS - 1
        bias = rpb[:, dr[None, :, None], dc[:, None, :]].astype(jnp.float32)
        s = jnp.einsum('bhwd,bhwijd->bhwij', q_row, k_g).astype(jnp.float32) + bias[None]
        p = jax.nn.softmax(s.reshape(b_, h_, GRID_W, kh * NA_COLS), axis=-1).reshape(s.shape)
        return jnp.einsum('bhwij,bhwijd->bhwd', p.astype(v.dtype), v_g)

    o = lax.map(row_block, (qr, jnp.arange(rows)))
    return jnp.moveaxis(o, 0, 2).reshape(b_, h_, t_, d)


def even_mixer(h, w_in, lam_params, subln, rpb, w_out, t5_bias, layer_idx):
    b_, t_, _ = h.shape
    proj = h @ w_in
    a_w = A_HEADS * 2 * A_DH
    b_w = B_HEADS * B_DH
    qa, ka, va, qb, kb, vb = jnp.split(proj, [a_w, 2 * a_w, 3 * a_w, 3 * a_w + b_w, 3 * a_w + 2 * b_w], axis=-1)
    qa = qa.reshape(b_, t_, A_HEADS, 2, A_DH).transpose(0, 2, 3, 1, 4)
    ka = ka.reshape(b_, t_, A_HEADS, 2, A_DH).transpose(0, 2, 3, 1, 4)
    va = va.reshape(b_, t_, A_HEADS, 2 * A_DH).transpose(0, 2, 1, 3)
    heads_b = lambda z: z.reshape(b_, t_, B_HEADS, B_DH).transpose(0, 2, 1, 3)
    lam_init = 0.8 - 0.6 * math.exp(-0.3 * layer_idx)
    lp = lam_params.astype(jnp.float32)
    lam = jnp.exp(jnp.sum(lp[0] * lp[1])) - jnp.exp(jnp.sum(lp[2] * lp[3])) + lam_init
    o_a = diff_attention(qa, ka, va, lam, lam_init, subln, t5_bias)
    o_b = neighbourhood_attention(heads_b(qb), heads_b(kb), heads_b(vb), rpb)
    o_a = o_a.transpose(0, 2, 1, 3).reshape(b_, t_, a_w)
    o_b = o_b.transpose(0, 2, 1, 3).reshape(b_, t_, b_w)
    return jnp.concatenate([o_a, o_b], axis=-1) @ w_out


def gla_chunked(q, k, v, log_g, strict):
    b_, h_, t_, dk = q.shape
    dv = v.shape[-1]
    nc = t_ // C_CHUNK
    chunks = lambda z: jnp.moveaxis(z.reshape(b_, h_, nc, C_CHUNK, z.shape[-1]), 2, 0)
    bcum = jnp.cumsum(chunks(log_g.astype(jnp.float32)), axis=-2)
    mask = jnp.tril(jnp.ones((C_CHUNK, C_CHUNK), dtype=bool), k=-1 if strict else 0)

    def step(state, xs):
        qc, kc, vc, bc = xs
        qc = qc.astype(jnp.float32)
        kc = kc.astype(jnp.float32)
        vc = vc.astype(jnp.float32)
        b_last = bc[..., -1:, :]
        qe = qc * jnp.exp(bc)
        ke = kc * jnp.exp(-bc)
        att = jnp.where(mask, jnp.einsum('bhid,bhjd->bhij', qe, ke), 0.0)
        o = att @ vc + qe @ state
        state = jnp.exp(b_last[..., 0, :])[..., None] * state + jnp.einsum(
            'bhjd,bhjv->bhdv', kc * jnp.exp(b_last - bc), vc)
        return state, o

    s0 = jnp.zeros((b_, h_, dk, dv), jnp.float32)
    _, o = lax.scan(step, s0, (chunks(q), chunks(k), chunks(v), bcum))
    return jnp.moveaxis(o, 0, 2).reshape(b_, h_, t_, dv)


def odd_mixer(h, w_in, w_gk_f, b_gk_f, w_gk_b, b_gk_b, out_norm, w_out):
    b_, t_, _ = h.shape
    proj = h @ w_in
    q, k, v, g, gf, gb = jnp.split(proj, list(np.cumsum(OD_SIZES)[:-1]), axis=-1)
    heads = lambda z, d: z.reshape(b_, t_, C_HEADS, d).transpose(0, 2, 1, 3)
    q = heads(q, C_DK) * C_DK ** -0.5
    k = heads(k, C_DK)
    v = heads(v, C_DV)
    log_f = heads(jax.nn.log_sigmoid((gf @ w_gk_f + b_gk_f).astype(jnp.float32)) / C_GATE_NORM, C_DK)
    log_b = heads(jax.nn.log_sigmoid((gb @ w_gk_b + b_gk_b).astype(jnp.float32)) / C_GATE_NORM, C_DK)
    flip = lambda z: jnp.flip(z, axis=2)
    o_fwd = gla_chunked(q, k, v, log_f, strict=False)
    o_bwd = flip(gla_chunked(flip(q), flip(k), flip(v), flip(log_b), strict=True))
    o = (o_fwd + o_bwd).astype(h.dtype)
    o = rmsnorm(o, out_norm) * jax.nn.silu(heads(g, C_DV))
    return o.transpose(0, 2, 1, 3).reshape(b_, t_, C_HEADS * C_DV) @ w_out


def hierarchical_moe(h, w_grp, b_grp, w_exp, b_exp, w_gate, w_up, w_down):
    b_, t_, _ = h.shape
    grp_p = jax.nn.softmax((h @ w_grp + b_grp).astype(jnp.float32), axis=-1)
    g_w, g_idx = lax.top_k(grp_p, 1)
    exp_logits = (h @ w_exp + b_exp).astype(jnp.float32).reshape(b_, t_, N_GROUPS, EXP_PER_GROUP)
    sel = jnp.take_along_axis(exp_logits, g_idx[..., None], axis=2)[:, :, 0]
    e_w, e_idx = lax.top_k(jax.nn.softmax(sel, axis=-1), TOP_K_IN_GROUP)
    e_w = e_w / jnp.sum(e_w, axis=-1, keepdims=True)
    weights = g_w * e_w
    expert_id = g_idx * EXP_PER_GROUP + e_idx
    gates = jnp.sum(jax.nn.one_hot(expert_id, N_EXPERTS, dtype=jnp.float32) * weights[..., None], axis=-2)
    y = jnp.zeros(h.shape, jnp.float32)
    for e in range(N_EXPERTS):
        hid = jax.nn.silu(h @ w_gate[e]) * (h @ w_up[e])
        y = y + gates[..., e:e + 1] * (hid @ w_down[e]).astype(jnp.float32)
    return y.astype(h.dtype)


def setup_inputs(seed: int = 0) -> dict:
    key = jax.random.key(seed)
    ks = jax.random.split(key, 26)
    nrm = lambda k, shape, scale: jax.random.normal(k, shape, jnp.float32) * scale
    gain = lambda k, shape: 1.0 + nrm(k, shape, 0.02)
    return {
        'x': nrm(ks[0], (BATCH, SEQ, D_MODEL), 1.0),
        't5_bias': nrm(ks[1], (T5_BUCKETS, A_HEADS), 0.5),
        'norm_mix': gain(ks[2], (DEPTH, D_MODEL)),
        'norm_ffn': gain(ks[3], (DEPTH, D_MODEL)),
        'norm_final': gain(ks[4], (D_MODEL,)),
        'ev_w_in': nrm(ks[5], (N_EVEN, D_MODEL, EV_IN), D_MODEL ** -0.5),
        'ev_lambda': nrm(ks[6], (N_EVEN, 4, A_DH), 0.1),
        'ev_subln': gain(ks[7], (N_EVEN, 2 * A_DH)),
        'ev_rpb': nrm(ks[8], (N_EVEN, B_HEADS, 2 * NA_MAX_ROWS - 1, 2 * NA_COLS - 1), 0.5),
        'ev_w_out': nrm(ks[9], (N_EVEN, MIX_WIDTH, D_MODEL), MIX_WIDTH ** -0.5),
        'od_w_in': nrm(ks[10], (N_ODD, D_MODEL, OD_IN), D_MODEL ** -0.5),
        'od_w_gk_fwd': nrm(ks[11], (N_ODD, C_GATE_RANK, C_HEADS * C_DK), C_GATE_RANK ** -0.5),
        'od_b_gk_fwd': nrm(ks[12], (N_ODD, C_HEADS * C_DK), 0.1),
        'od_w_gk_bwd': nrm(ks[13], (N_ODD, C_GATE_RANK, C_HEADS * C_DK), C_GATE_RANK ** -0.5),
        'od_b_gk_bwd': nrm(ks[14], (N_ODD, C_HEADS * C_DK), 0.1),
        'od_out_norm': gain(ks[15], (N_ODD, C_DV)),
        'od_w_out': nrm(ks[16], (N_ODD, C_HEADS * C_DV, D_MODEL), (C_HEADS * C_DV) ** -0.5),
        'moe_w_grp': nrm(ks[17], (DEPTH, D_MODEL, N_GROUPS), D_MODEL ** -0.5),
        'moe_b_grp': nrm(ks[18], (DEPTH, N_GROUPS), 0.01),
        'moe_w_exp': nrm(ks[19], (DEPTH, D_MODEL, N_EXPERTS), D_MODEL ** -0.5),
        'moe_b_exp': nrm(ks[20], (DEPTH, N_EXPERTS), 0.01),
        'moe_w_gate': nrm(ks[21], (DEPTH, N_EXPERTS, D_MODEL, D_EXPERT), D_MODEL ** -0.5),
        'moe_w_up': nrm(ks[22], (DEPTH, N_EXPERTS, D_MODEL, D_EXPERT), D_MODEL ** -0.5),
        'moe_w_down': nrm(ks[23], (DEPTH, N_EXPERTS, D_EXPERT, D_MODEL), D_EXPERT ** -0.5),
    }


def reference(x, t5_bias, norm_mix, norm_ffn, norm_final, ev_w_in, ev_lambda, ev_subln, ev_rpb,
              ev_w_out, od_w_in, od_w_gk_fwd, od_b_gk_fwd, od_w_gk_bwd, od_b_gk_bwd, od_out_norm,
              od_w_out, moe_w_grp, moe_b_grp, moe_w_exp, moe_b_exp, moe_w_gate, moe_w_up, moe_w_down):
    for layer in range(DEPTH):
        i = layer // 2
        h = rmsnorm(x, norm_mix[layer])
        if layer % 2 == 0:
            mix = even_mixer(h, ev_w_in[i], ev_lambda[i], ev_subln[i], ev_rpb[i], ev_w_out[i], t5_bias, layer)
        else:
            mix = odd_mixer(h, od_w_in[i], od_w_gk_fwd[i], od_b_gk_fwd[i], od_w_gk_bwd[i], od_b_gk_bwd[i],
                            od_out_norm[i], od_w_out[i])
        x = x + mix
        h = rmsnorm(x, norm_ffn[layer])
        x = x + hierarchical_moe(h, moe_w_grp[layer], moe_b_grp[layer], moe_w_exp[layer], moe_b_exp[layer],
                                 moe_w_gate[layer], moe_w_up[layer], moe_w_down[layer])
    return rmsnorm(x, norm_final)
```

```python
import functools
import math

import jax
import jax.numpy as jnp
from jax import lax
from jax.experimental import pallas as pl
from jax.experimental.pallas import tpu as pltpu

D_MODEL = 1024
GRID_W = 64
A_HEADS = 4
A_DH = 64
B_HEADS = 8
B_DH = 64
NA_MAX_ROWS = 8
NA_COLS = 16
T5_BUCKETS = 32
T5_MAX_DIST = 128
C_HEADS = 4
C_DK = 128
C_DV = 256
C_GATE_RANK = 16
C_GATE_NORM = 16.0
C_CHUNK = 64
N_GROUPS = 4
EXP_PER_GROUP = 8
N_EXPERTS = N_GROUPS * EXP_PER_GROUP
D_EXPERT = 512
EPS = 1e-6

A_W = A_HEADS * 2 * A_DH
B_W = B_HEADS * B_DH
LANES = 128
VMEM_LIMIT = 56 * 1024 * 1024

ROW_TILE = 512
ATT_TQ = 256
NA_QROWS = 4
NA_KROWS = 12
MOE_TILE = 256

F32 = jnp.float32
BF16 = jnp.bfloat16
NT_DIMS = (((1,), (1,)), ((), ()))
TN_DIMS = (((0,), (0,)), ((), ()))


def _cparams(sem):
    return pltpu.CompilerParams(dimension_semantics=sem, vmem_limit_bytes=VMEM_LIMIT)


def _rms(x, gain):
    return x * lax.rsqrt(jnp.mean(x * x, axis=-1, keepdims=True) + EPS) * gain


def _sigmoid(x):
    return 1.0 / (1.0 + jnp.exp(-x))


def _log_sigmoid(z):
    return jnp.minimum(z, 0.0) - jnp.log1p(jnp.exp(-jnp.abs(z)))


def _in_proj_kernel(*refs, n_parts, gates):
    parts = refs[:n_parts]
    gain_ref, w_ref = refs[n_parts], refs[n_parts + 1]
    pos = n_parts + 2
    if gates:
        wg_ref, wf_ref, bf_ref, wb_ref, bb_ref = refs[pos:pos + 5]
        pos += 5
    outs = refs[pos:]
    x = parts[0][...]
    for p in parts[1:]:
        x = x + p[...]
    oi = 0
    if n_parts > 1:
        outs[0][...] = x
        oi = 1
    h = _rms(x, gain_ref[...]).astype(BF16)
    proj_ref = outs[oi]
    n_out = proj_ref.shape[1]
    step = 512
    for j in range(n_out // step):
        proj_ref[:, j * step:(j + 1) * step] = jnp.dot(
            h, w_ref[:, j * step:(j + 1) * step], preferred_element_type=F32).astype(BF16)
    if gates:
        lf_ref, lb_ref = outs[oi + 1], outs[oi + 2]
        g = jnp.dot(h, wg_ref[...], preferred_element_type=F32).astype(BF16)
        zf = jnp.dot(g, wf_ref[...], preferred_element_type=F32) + bf_ref[...]
        zb = jnp.dot(g, wb_ref[...], preferred_element_type=F32) + bb_ref[...]
        lf_ref[...] = _log_sigmoid(zf) / C_GATE_NORM
        lb_ref[...] = _log_sigmoid(zb) / C_GATE_NORM


def _row_spec(tm, width, block_off=0):
    return pl.BlockSpec((tm, width), lambda i, o=block_off: (i + o, 0))


def _full_spec(shape):
    nd = len(shape)
    return pl.BlockSpec(shape, lambda i, _nd=nd: (0,) * _nd)


def _in_proj_call(n, parts, gain, w, gates=None):
    tm = ROW_TILE
    n_parts = len(parts)
    n_out = w.shape[1]
    in_specs = [_row_spec(tm, D_MODEL, off // tm) for _, off in parts]
    args = [a for a, _ in parts]
    in_specs += [_full_spec((1, D_MODEL)), _full_spec(w.shape)]
    args += [gain.reshape(1, D_MODEL), w]
    out_shape, out_specs = [], []
    if n_parts > 1:
        out_shape.append(jax.ShapeDtypeStruct((n, D_MODEL), F32))
        out_specs.append(_row_spec(tm, D_MODEL))
    out_shape.append(jax.ShapeDtypeStruct((n, n_out), BF16))
    out_specs.append(_row_spec(tm, n_out))
    if gates is not None:
        for a in gates:
            in_specs.append(_full_spec(a.shape))
            args.append(a)
        kw = C_HEADS * C_DK
        out_shape += [jax.ShapeDtypeStruct((n, kw), F32)] * 2
        out_specs += [_row_spec(tm, kw)] * 2
    return pl.pallas_call(
        functools.partial(_in_proj_kernel, n_parts=n_parts, gates=gates is not None),
        grid=(n // tm,),
        in_specs=in_specs,
        out_specs=out_specs,
        out_shape=out_shape,
        compiler_params=_cparams(("parallel",)),
        name="in_proj_gla" if gates is not None else "in_proj_attn",
    )(*args)


def _diff_attn_kernel(lam_ref, q_ref, k_ref, v_ref, bias_ref, gain_ref, o_ref, *, seq, out_scale):
    tq = q_ref.shape[0]
    qb = pl.program_id(2)
    q = q_ref[...]
    k = k_ref[...]
    lane = lax.broadcasted_iota(jnp.int32, q.shape, 1)
    scale = A_DH ** -0.5
    zero = jnp.zeros_like(q)
    off = pl.multiple_of((seq - tq) - qb * tq, LANES)
    bias = bias_ref[:, pl.ds(off, seq)]

    def softmax_map(qm):
        s = lax.dot_general(qm * scale, k, NT_DIMS, preferred_element_type=F32) + bias
        m = jnp.max(s, axis=-1, keepdims=True)
        p = jnp.exp(s - m)
        return p / jnp.sum(p, axis=-1, keepdims=True)

    p0 = softmax_map(jnp.where(lane < A_DH, q, zero))
    p1 = softmax_map(jnp.where(lane >= A_DH, q, zero))
    attn = (p0 - lam_ref[0] * p1).astype(BF16)
    o = jnp.dot(attn, v_ref[...], preferred_element_type=F32)
    o_ref[...] = (_rms(o, gain_ref[...]) * out_scale).astype(BF16)


def _t5_bucket(rel):
    half = T5_BUCKETS // 2
    max_exact = half // 2
    sign_off = jnp.where(rel > 0, half, 0)
    n = jnp.abs(rel)
    nf = jnp.maximum(n, 1).astype(F32)
    large = max_exact + (jnp.log(nf / max_exact) / math.log(T5_MAX_DIST / max_exact)
                         * (half - max_exact)).astype(jnp.int32)
    large = jnp.minimum(large, half - 1)
    return sign_off + jnp.where(n < max_exact, n, large)


def _t5_strip(t5_bias, seq, tq):
    rel = jnp.arange(2 * seq - tq)[None, :] - jnp.arange(tq)[:, None] - (seq - tq)
    return jnp.transpose(t5_bias[_t5_bucket(rel)], (2, 0, 1)).astype(F32)


def _diff_attn(proj3, lam, strip, subln, out_scale):
    b, seq, _ = proj3.shape
    tq = ATT_TQ
    kblk, vblk = A_W // LANES, 2 * A_W // LANES
    return pl.pallas_call(
        functools.partial(_diff_attn_kernel, seq=seq, out_scale=out_scale),
        grid=(A_HEADS, b, seq // tq),
        in_specs=[
            pl.BlockSpec(memory_space=pltpu.SMEM),
            pl.BlockSpec((None, tq, LANES), lambda h, i, j: (i, j, h)),
            pl.BlockSpec((None, seq, LANES), lambda h, i, j: (i, 0, kblk + h)),
            pl.BlockSpec((None, seq, LANES), lambda h, i, j: (i, 0, vblk + h)),
            pl.BlockSpec((None, tq, 2 * seq - tq), lambda h, i, j: (h, 0, 0)),
            pl.BlockSpec((1, LANES), lambda h, i, j: (0, 0)),
        ],
        out_specs=pl.BlockSpec((None, tq, LANES), lambda h, i, j: (i, j, h)),
        out_shape=jax.ShapeDtypeStruct((b, seq, A_W), BF16),
        compiler_params=_cparams(("parallel", "parallel", "parallel")),
        name="diff_attn",
    )(lam, proj3, proj3, proj3, strip, subln.reshape(1, LANES))


def _na_block_geometry(rows):
    nblk = rows // NA_QROWS
    kh = min(NA_MAX_ROWS, rows)
    starts, classes, reps = [], [], []
    for j in range(nblk):
        ks = min(max(j * NA_QROWS - kh // 2, 0), rows - NA_KROWS)
        rel = (j * NA_QROWS - ks,) + tuple(
            min(max(r - kh // 2, 0), rows - kh) - ks for r in range(j * NA_QROWS, (j + 1) * NA_QROWS))
        starts.append(ks)
        if rel not in reps:
            reps.append(rel)
        classes.append(reps.index(rel))
    return starts, classes, reps


def _na_kernel(q_ref, k_ref, v_ref, tab_ref, o_ref, *, starts, classes):
    qn, kn = NA_QROWS * GRID_W, NA_KROWS * GRID_W
    scale = B_DH ** -0.5
    lane = lax.broadcasted_iota(jnp.int32, (qn, LANES), 1)
    for j, (ks, cls) in enumerate(zip(starts, classes)):
        q = q_ref[j * qn:(j + 1) * qn, :]
        kw = k_ref[ks * GRID_W:ks * GRID_W + kn, :]
        vw = v_ref[ks * GRID_W:ks * GRID_W + kn, :]
        zero = jnp.zeros_like(q)
        outs = []
        for hl in range(2):
            in_head = (lane >= hl * B_DH) & (lane < (hl + 1) * B_DH)
            qm = jnp.where(in_head, q, zero) * scale
            s = lax.dot_general(qm, kw, NT_DIMS, preferred_element_type=F32) + tab_ref[cls, hl]
            m = jnp.max(s, axis=-1, keepdims=True)
            p = jnp.exp(s - m)
            p = (p / jnp.sum(p, axis=-1, keepdims=True)).astype(BF16)
            outs.append(jnp.dot(p, vw, preferred_element_type=F32))
        o_ref[j * qn:(j + 1) * qn, :] = jnp.where(lane < B_DH, outs[0], outs[1]).astype(BF16)


def _na_attn(proj3, table, starts, classes):
    b, seq, _ = proj3.shape
    qblk = 3 * A_W // LANES
    kblk = qblk + B_W // LANES
    vblk = kblk + B_W // LANES
    ncls = table.shape[1]
    qn, kn = NA_QROWS * GRID_W, NA_KROWS * GRID_W
    return pl.pallas_call(
        functools.partial(_na_kernel, starts=tuple(starts), classes=tuple(classes)),
        grid=(B_HEADS // 2, b),
        in_specs=[
            pl.BlockSpec((None, seq, LANES), lambda h, i: (i, 0, qblk + h)),
            pl.BlockSpec((None, seq, LANES), lambda h, i: (i, 0, kblk + h)),
            pl.BlockSpec((None, seq, LANES), lambda h, i: (i, 0, vblk + h)),
            pl.BlockSpec((None, ncls, 2, qn, kn), lambda h, i: (h, 0, 0, 0, 0)),
        ],
        out_specs=pl.BlockSpec((None, seq, LANES), lambda h, i: (i, 0, h)),
        out_shape=jax.ShapeDtypeStruct((b, seq, B_W), BF16),
        compiler_params=_cparams(("parallel", "parallel")),
        name="na_attn",
    )(proj3, proj3, proj3, table)


def _out_proj_router_kernel(*refs, n_a):
    x_ref = refs[0]
    a_refs = refs[1:1 + n_a]
    w_refs = refs[1 + n_a:1 + 2 * n_a]
    gain_ref, wr_ref, br_ref, xo_ref, hn_ref, ids_ref, wts_ref = refs[1 + 2 * n_a:]
    acc = x_ref[...]
    for a, w in zip(a_refs, w_refs):
        acc = acc + jnp.dot(a[...], w[...], preferred_element_type=F32)
    xo_ref[...] = acc
    h = _rms(acc, gain_ref[...])
    hn_ref[...] = h
    logits = jnp.dot(h, wr_ref[...], preferred_element_type=F32,
                     precision=lax.Precision.HIGHEST) + br_ref[...]
    lane = lax.broadcasted_iota(jnp.int32, logits.shape, 1)
    neg = jnp.float32(-jnp.inf)
    big = jnp.int32(LANES)

    def masked_softmax(mask):
        z = jnp.where(mask, logits, neg)
        e = jnp.exp(z - jnp.max(z, axis=-1, keepdims=True))
        return e / jnp.sum(e, axis=-1, keepdims=True)

    def top1(p, mask):
        w = jnp.max(jnp.where(mask, p, -1.0), axis=-1, keepdims=True)
        idx = jnp.min(jnp.where(mask & (p == w), lane, big), axis=-1, keepdims=True)
        return w, idx

    is_grp = lane < N_GROUPS
    g_w, g_idx = top1(masked_softmax(is_grp), is_grp)
    e_lane = lane - N_GROUPS
    in_grp = (e_lane >= g_idx * EXP_PER_GROUP) & (e_lane < (g_idx + 1) * EXP_PER_GROUP)
    p_e = masked_softmax(in_grp)
    w1, i1 = top1(p_e, in_grp)
    rest = in_grp & (lane != i1)
    w2, i2 = top1(p_e, rest)
    denom = w1 + w2
    ids_ref[...] = jnp.where(lane == 0, i1 - N_GROUPS, i2 - N_GROUPS)
    wts_ref[...] = jnp.where(lane == 0, g_w * (w1 / denom), g_w * (w2 / denom))


def _out_proj_router(n, x, acts, ws, gain, w_router, b_router):
    tm = ROW_TILE
    n_a = len(acts)
    in_specs = [_row_spec(tm, D_MODEL)]
    in_specs += [_row_spec(tm, a.shape[1]) for a in acts]
    in_specs += [_full_spec(w.shape) for w in ws]
    in_specs += [_full_spec((1, D_MODEL)), _full_spec(w_router.shape), _full_spec((1, LANES))]
    return pl.pallas_call(
        functools.partial(_out_proj_router_kernel, n_a=n_a),
        grid=(n // tm,),
        in_specs=in_specs,
        out_specs=[_row_spec(tm, D_MODEL), _row_spec(tm, D_MODEL), _row_spec(tm, LANES), _row_spec(tm, LANES)],
        out_shape=[jax.ShapeDtypeStruct((n, D_MODEL), F32), jax.ShapeDtypeStruct((n, D_MODEL), F32),
                   jax.ShapeDtypeStruct((n, LANES), jnp.int32), jax.ShapeDtypeStruct((n, LANES), F32)],
        compiler_params=_cparams(("parallel",)),
        name="out_proj_router",
    )(x, *acts, *ws, gain.reshape(1, D_MODEL), w_router, b_router)


def _router_params(w_grp, b_grp, w_exp, b_exp):
    pad = LANES - N_GROUPS - N_EXPERTS
    w = jnp.concatenate([w_grp, w_exp, jnp.zeros((D_MODEL, pad), F32)], axis=1)
    b = jnp.concatenate([b_grp, b_exp, jnp.zeros((pad,), F32)]).reshape(1, LANES)
    return w, b


def _moe_plan(ids, wts, n):
    tm = MOE_TILE
    n_assign = 2 * n
    p_rows = n_assign + N_EXPERTS * tm
    e_flat = ids.T.reshape(-1)
    w_flat = wts.T.reshape(-1)
    order = jnp.argsort(e_flat, stable=True).astype(jnp.int32)
    e_sorted = e_flat[order]
    counts = jnp.zeros((N_EXPERTS,), jnp.int32).at[e_flat].add(1)
    padded = (counts + tm - 1) // tm * tm
    p_end = jnp.cumsum(padded)
    p_off = p_end - padded
    off = jnp.cumsum(counts) - counts
    pos = p_off[e_sorted] + jnp.arange(n_assign, dtype=jnp.int32) - off[e_sorted]
    src = jnp.zeros((p_rows,), jnp.int32).at[pos].set(order % n)
    dest = jnp.zeros((p_rows,), jnp.int32).at[pos].set(order)
    w_row = jnp.zeros((p_rows,), F32).at[pos].set(w_flat[order])
    nt = p_rows // tm
    tile_start = jnp.arange(nt, dtype=jnp.int32) * tm
    tile_expert = jnp.minimum(jnp.searchsorted(p_end, tile_start, side="right"), N_EXPERTS - 1).astype(jnp.int32)
    tile_count = jnp.clip(counts[tile_expert] - (tile_start - p_off[tile_expert]), 0, tm).astype(jnp.int32)
    n_used = (p_end[-1] // tm).astype(jnp.int32).reshape(1)
    return (src.reshape(nt, 1, tm), dest.reshape(nt, 1, tm), w_row.reshape(p_rows, 1), tile_expert,
            tile_count, n_used)


def _moe_kernel(te_ref, tc_ref, nu_ref, src_ref, dst_ref, wrow_ref, x_hbm, wg_ref, wu_ref, wd_ref, y_hbm,
                xbuf, ybuf, wgb, wub, wdb, gsem, ssem):
    tm = MOE_TILE
    i = pl.program_id(0)
    n_used = nu_ref[0]
    last = pl.num_programs(0) - 1

    def gather_copy(slot, r, tok):
        return pltpu.make_async_copy(x_hbm.at[pl.ds(tok, 1)], xbuf.at[slot, pl.ds(r, 1)], gsem.at[slot])

    def scatter_copy(slot, r, row):
        return pltpu.make_async_copy(ybuf.at[slot, pl.ds(r, 1)], y_hbm.at[pl.ds(row, 1)], ssem.at[slot])

    def for_rows(count, fn):
        def body(r, c):
            fn(r)
            return c

        @pl.when(count == tm)
        def _():
            lax.fori_loop(0, tm, body, 0, unroll=8)

        @pl.when(count < tm)
        def _():
            lax.fori_loop(0, count, body, 0)

    def wait_scatter(t):
        slot = t % 2
        count = tc_ref[t]

        @pl.when(count == tm)
        def _():
            pltpu.make_async_copy(ybuf.at[slot], y_hbm.at[pl.ds(0, tm)], ssem.at[slot]).wait()

        @pl.when(count < tm)
        def _():
            lax.fori_loop(0, count, lambda r, c: (scatter_copy(slot, r, 0).wait(), c)[1], 0)

    @pl.when(i < n_used)
    def _():
        slot = i % 2
        lax.fori_loop(0, tm, lambda r, c: (gather_copy(slot, r, src_ref[0, r]).start(), c)[1], 0, unroll=8)

    @pl.when((i >= 1) & (i <= n_used))
    def _():
        t = i - 1
        slot = t % 2
        pltpu.make_async_copy(x_hbm.at[pl.ds(0, tm)], xbuf.at[slot], gsem.at[slot]).wait()

        @pl.when(t >= 2)
        def _():
            wait_scatter(t - 2)

        new_expert = jnp.logical_or(t == 0, te_ref[t] != te_ref[jnp.maximum(t - 1, 0)])

        @pl.when(new_expert)
        def _():
            wgb[...] = wg_ref[...].astype(BF16)
            wub[...] = wu_ref[...].astype(BF16)
            wdb[...] = wd_ref[...].astype(BF16)

        x = xbuf[slot].astype(BF16)
        g = jnp.dot(x, wgb[...], preferred_element_type=F32)
        u = jnp.dot(x, wub[...], preferred_element_type=F32)
        hid = (g * _sigmoid(g) * u).astype(BF16)
        y = jnp.dot(hid, wdb[...], preferred_element_type=F32)
        ybuf[slot] = y * wrow_ref[...]

        for_rows(tc_ref[t], lambda r: scatter_copy(slot, r, dst_ref[0, r]).start())

    @pl.when(i == last)
    def _():
        wait_scatter(n_used - 1)

        @pl.when(n_used >= 2)
        def _():
            wait_scatter(n_used - 2)


def _moe_experts(n, hn, plan, w_gate, w_up, w_down):
    tm = MOE_TILE
    src, dest, w_row, tile_expert, tile_count, n_used = plan
    nt = src.shape[0]
    prev = lambda i: jnp.maximum(i - 1, 0)
    grid_spec = pltpu.PrefetchScalarGridSpec(
        num_scalar_prefetch=3,
        grid=(nt + 1,),
        in_specs=[
            pl.BlockSpec((None, 1, tm), lambda i, te, tc, nu: (jnp.minimum(i, nt - 1), 0, 0), memory_space=pltpu.SMEM),
            pl.BlockSpec((None, 1, tm), lambda i, te, tc, nu: (prev(i), 0, 0), memory_space=pltpu.SMEM),
            pl.BlockSpec((tm, 1), lambda i, te, tc, nu: (prev(i), 0)),
            pl.BlockSpec(memory_space=pl.ANY),
            pl.BlockSpec((None, D_MODEL, D_EXPERT), lambda i, te, tc, nu: (te[prev(i)], 0, 0)),
            pl.BlockSpec((None, D_MODEL, D_EXPERT), lambda i, te, tc, nu: (te[prev(i)], 0, 0)),
            pl.BlockSpec((None, D_EXPERT, D_MODEL), lambda i, te, tc, nu: (te[prev(i)], 0, 0)),
        ],
        out_specs=pl.BlockSpec(memory_space=pl.ANY),
        scratch_shapes=[
            pltpu.VMEM((2, tm, D_MODEL), F32),
            pltpu.VMEM((2, tm, D_MODEL), F32),
            pltpu.VMEM((D_MODEL, D_EXPERT), BF16),
            pltpu.VMEM((D_MODEL, D_EXPERT), BF16),
            pltpu.VMEM((D_EXPERT, D_MODEL), BF16),
            pltpu.SemaphoreType.DMA((2,)),
            pltpu.SemaphoreType.DMA((2,)),
        ],
    )
    return pl.pallas_call(
        _moe_kernel,
        grid_spec=grid_spec,
        out_shape=jax.ShapeDtypeStruct((2 * n, D_MODEL), F32),
        compiler_params=_cparams(("arbitrary",)),
        name="moe_experts",
    )(tile_expert, tile_count, n_used, src, dest, w_row, hn, w_gate, w_up, w_down)


def _gla_kernel(q_ref, k_ref, v_ref, g_ref, lf_ref, lb_ref, gain_ref, o_ref,
                bf_s, bb_s, sf_s, sb_s, stf, stb, *, seq):
    c_len = C_CHUNK
    nc = seq // c_len
    scale = C_DK ** -0.5
    row = lax.broadcasted_iota(jnp.int32, (c_len, C_DK), 0)
    shifts = [1 << s for s in range(int(math.log2(c_len)))]

    def rows(c):
        return pl.ds(pl.multiple_of(c * c_len, c_len), c_len)

    def cum_body(c, carry):
        y = lf_ref[rows(c), :]
        for s in shifts:
            y = y + jnp.where(row >= s, pltpu.roll(y, s, 0), 0.0)
        bf_s[rows(c), :] = y
        y = lb_ref[rows(c), :]
        for s in shifts:
            y = y + jnp.where(row < c_len - s, pltpu.roll(y, c_len - s, 0), 0.0)
        bb_s[rows(c), :] = y
        return carry

    lax.fori_loop(0, nc, cum_body, 0)

    stf[...] = jnp.zeros_like(stf)
    stb[...] = jnp.zeros_like(stb)

    def state_step(c, b_s, last_row, st, s_all):
        k = k_ref[rows(c), :].astype(F32)
        bc = b_s[rows(c), :]
        bl = bc[last_row:last_row + 1, :]
        kd = (k * jnp.exp(bl - bc)).astype(BF16)
        kv_t = lax.dot_general(v_ref[rows(c), :], kd, TN_DIMS, preferred_element_type=F32)
        s_all[c] = st[...].astype(BF16)
        st[...] = st[...] * jnp.exp(bl) + kv_t

    def state_body(i, carry):
        state_step(i, bf_s, c_len - 1, stf, sf_s)
        state_step(nc - 1 - i, bb_s, 0, stb, sb_s)
        return carry

    lax.fori_loop(0, nc, state_body, 0)

    ri = lax.broadcasted_iota(jnp.int32, (c_len, c_len), 0)
    ci = lax.broadcasted_iota(jnp.int32, (c_len, c_len), 1)

    def out_body(c, carry):
        q = q_ref[rows(c), :].astype(F32) * scale
        k = k_ref[rows(c), :].astype(F32)
        v = v_ref[rows(c), :]

        def direction(b_s, s_all, mask):
            bc = b_s[rows(c), :]
            qe = (q * jnp.exp(bc)).astype(BF16)
            ke = (k * jnp.exp(-bc)).astype(BF16)
            att = lax.dot_general(qe, ke, NT_DIMS, preferred_element_type=F32)
            att = jnp.where(mask, att, 0.0).astype(BF16)
            return (jnp.dot(att, v, preferred_element_type=F32)
                    + lax.dot_general(qe, s_all[c], NT_DIMS, preferred_element_type=F32))

        o = direction(bf_s, sf_s, ri >= ci) + direction(bb_s, sb_s, ri < ci)
        g = g_ref[rows(c), :].astype(F32)
        o_ref[rows(c), :] = (_rms(o, gain_ref[...]) * (g * _sigmoid(g))).astype(BF16)
        return carry

    lax.fori_loop(0, nc, out_body, 0)


def _gla(proj3, lf3, lb3, out_norm):
    b, seq, _ = proj3.shape
    kw = C_HEADS * C_DK
    k_blk = kw // C_DK
    v_blk = 2 * kw // C_DV
    g_blk = v_blk + C_HEADS
    nc = seq // C_CHUNK
    return pl.pallas_call(
        functools.partial(_gla_kernel, seq=seq),
        grid=(b, C_HEADS),
        in_specs=[
            pl.BlockSpec((None, seq, C_DK), lambda i, h: (i, 0, h)),
            pl.BlockSpec((None, seq, C_DK), lambda i, h: (i, 0, k_blk + h)),
            pl.BlockSpec((None, seq, C_DV), lambda i, h: (i, 0, v_blk + h)),
            pl.BlockSpec((None, seq, C_DV), lambda i, h: (i, 0, g_blk + h)),
            pl.BlockSpec((None, seq, C_DK), lambda i, h: (i, 0, h)),
            pl.BlockSpec((None, seq, C_DK), lambda i, h: (i, 0, h)),
            pl.BlockSpec((1, C_DV), lambda i, h: (0, 0)),
        ],
        out_specs=pl.BlockSpec((None, seq, C_DV), lambda i, h: (i, 0, h)),
        out_shape=jax.ShapeDtypeStruct((b, seq, C_HEADS * C_DV), BF16),
        scratch_shapes=[
            pltpu.VMEM((seq, C_DK), F32),
            pltpu.VMEM((seq, C_DK), F32),
            pltpu.VMEM((nc, C_DV, C_DK), BF16),
            pltpu.VMEM((nc, C_DV, C_DK), BF16),
            pltpu.VMEM((C_DV, C_DK), F32),
            pltpu.VMEM((C_DV, C_DK), F32),
        ],
        compiler_params=_cparams(("parallel", "parallel")),
        name="gla",
    )(proj3, proj3, proj3, proj3, lf3, lb3, out_norm.reshape(1, C_DV))


def _final_kernel(x_ref, y0_ref, y1_ref, gain_ref, o_ref):
    o_ref[...] = _rms(x_ref[...] + y0_ref[...] + y1_ref[...], gain_ref[...])


def _final_norm(n, x, y, gain):
    tm = ROW_TILE
    return pl.pallas_call(
        _final_kernel,
        grid=(n // tm,),
        in_specs=[_row_spec(tm, D_MODEL), _row_spec(tm, D_MODEL), _row_spec(tm, D_MODEL, n // tm),
                  _full_spec((1, D_MODEL))],
        out_specs=_row_spec(tm, D_MODEL),
        out_shape=jax.ShapeDtypeStruct((n, D_MODEL), F32),
        compiler_params=_cparams(("parallel",)),
        name="final_norm",
    )(x, y, y, gain.reshape(1, D_MODEL))


def _moe_layer(n, hn, ids_pad, wts_pad, w_gate, w_up, w_down):
    plan = _moe_plan(ids_pad[:, :2], wts_pad[:, :2], n)
    return _moe_experts(n, hn, plan, w_gate, w_up, w_down)


def kernel(x, t5_bias, norm_mix, norm_ffn, norm_final, ev_w_in, ev_lambda, ev_subln, ev_rpb, ev_w_out,
           od_w_in, od_w_gk_fwd, od_b_gk_fwd, od_w_gk_bwd, od_b_gk_bwd, od_out_norm, od_w_out,
           moe_w_grp, moe_b_grp, moe_w_exp, moe_b_exp, moe_w_gate, moe_w_up, moe_w_down):
    b, seq, d = x.shape
    n = b * seq
    rows = seq // GRID_W
    assert d == D_MODEL and n % ROW_TILE == 0 and seq % ATT_TQ == 0 and seq % C_CHUNK == 0
    assert rows % NA_QROWS == 0 and rows >= NA_KROWS and MOE_TILE <= ROW_TILE
    x2 = x.reshape(n, d)

    (proj,) = _in_proj_call(n, [(x2, 0)], norm_mix[0], ev_w_in[0].astype(BF16))
    proj3 = proj.reshape(b, seq, proj.shape[1])
    lam_init = 0.8 - 0.6 * math.exp(-0.3 * 0)
    lp = ev_lambda[0].astype(F32)
    lam = (jnp.exp(jnp.sum(lp[0] * lp[1])) - jnp.exp(jnp.sum(lp[2] * lp[3])) + lam_init).reshape(1)
    strip = _t5_strip(t5_bias, seq, ATT_TQ)
    a_diff = _diff_attn(proj3, lam, strip, ev_subln[0], 1.0 - lam_init)

    starts, classes, _ = _na_block_geometry(rows)
    table = _na_table(ev_rpb[0], rows)
    a_na = _na_attn(proj3, table, starts, classes)

    w_out = ev_w_out[0].astype(BF16)
    w_r, b_r = _router_params(moe_w_grp[0], moe_b_grp[0], moe_w_exp[0], moe_b_exp[0])
    x1, hn, ids, wts = _out_proj_router(
        n, x2, [a_diff.reshape(n, A_W), a_na.reshape(n, B_W)], [w_out[:A_W], w_out[A_W:]],
        norm_ffn[0], w_r, b_r)
    y = _moe_layer(n, hn, ids, wts, moe_w_gate[0], moe_w_up[0], moe_w_down[0])

    kw = C_HEADS * C_DK
    main_w = 2 * kw + 2 * C_HEADS * C_DV
    w_in = od_w_in[0]
    w_gates = jnp.pad(w_in[:, main_w:], ((0, 0), (0, LANES - 2 * C_GATE_RANK))).astype(BF16)
    wf = jnp.pad(od_w_gk_fwd[0], ((0, LANES - C_GATE_RANK), (0, 0))).astype(BF16)
    wb = jnp.pad(od_w_gk_bwd[0], ((C_GATE_RANK, LANES - 2 * C_GATE_RANK), (0, 0))).astype(BF16)
    x2b, proj, lf, lb = _in_proj_call(
        n, [(x1, 0), (y, 0), (y, n)], norm_mix[1], w_in[:, :main_w].astype(BF16),
        gates=[w_gates, wf, od_b_gk_fwd[0].reshape(1, kw), wb, od_b_gk_bwd[0].reshape(1, kw)])
    a_gla = _gla(proj.reshape(b, seq, main_w), lf.reshape(b, seq, kw), lb.reshape(b, seq, kw), od_out_norm[0])

    w_r, b_r = _router_params(moe_w_grp[1], moe_b_grp[1], moe_w_exp[1], moe_b_exp[1])
    x3, hn, ids, wts = _out_proj_router(
        n, x2b, [a_gla.reshape(n, C_HEADS * C_DV)], [od_w_out[0].astype(BF16)], norm_ffn[1], w_r, b_r)
    y = _moe_layer(n, hn, ids, wts, moe_w_gate[1], moe_w_up[1], moe_w_down[1])

    return _final_norm(n, x3, y, norm_final).reshape(b, seq, d)


def _na_table(rpb, rows):
    kh = min(NA_MAX_ROWS, rows)
    starts, classes, reps = _na_block_geometry(rows)
    qn, kn = NA_QROWS * GRID_W, NA_KROWS * GRID_W
    q_row, q_col = jnp.arange(qn) // GRID_W, jnp.arange(qn) % GRID_W
    k_row, k_col = jnp.arange(kn) // GRID_W, jnp.arange(kn) % GRID_W
    col_start = jnp.clip(q_col - NA_COLS // 2, 0, GRID_W - NA_COLS)
    col_ok = (k_col[None, :] >= col_start[:, None]) & (k_col[None, :] < col_start[:, None] + NA_COLS)
    dc = jnp.clip(k_col[None, :] - q_col[:, None] + NA_COLS - 1, 0, 2 * NA_COLS - 2)
    per_class = []
    for cls in range(len(reps)):
        j = classes.index(cls)
        q_abs = j * NA_QROWS + q_row
        k_abs = starts[j] + k_row
        row_start = jnp.clip(q_abs - kh // 2, 0, rows - kh)
        row_ok = (k_abs[None, :] >= row_start[:, None]) & (k_abs[None, :] < row_start[:, None] + kh)
        dr = jnp.clip(k_abs[None, :] - q_abs[:, None] + NA_MAX_ROWS - 1, 0, 2 * NA_MAX_ROWS - 2)
        bias = rpb[:, dr, dc].astype(F32)
        per_class.append(jnp.where((row_ok & col_ok)[None], bias, -jnp.inf))
    t = jnp.stack(per_class, axis=1)
    return jnp.transpose(t.reshape(B_HEADS // 2, 2, len(reps), qn, kn), (0, 2, 1, 3, 4))
```

```python
import functools
import math

import jax
import jax.numpy as jnp
from jax import lax
from jax.experimental import pallas as pl
from jax.experimental.pallas import tpu as pltpu

D_MODEL = 1024
GRID_W = 64
A_HEADS = 4
A_DH = 64
B_HEADS = 8
B_DH = 64
NA_MAX_ROWS = 8
NA_COLS = 16
T5_BUCKETS = 32
T5_MAX_DIST = 128
C_HEADS = 4
C_DK = 128
C_DV = 256
C_GATE_RANK = 16
C_GATE_NORM = 16.0
C_CHUNK = 64
N_GROUPS = 4
EXP_PER_GROUP = 8
N_EXPERTS = N_GROUPS * EXP_PER_GROUP
D_EXPERT = 512
EPS = 1e-6

A_W = A_HEADS * 2 * A_DH
B_W = B_HEADS * B_DH
LANES = 128
VMEM_LIMIT = 56 * 1024 * 1024

ROW_TILE = 512
ATT_TQ = 256
NA_QROWS = 4
NA_KROWS = 12
MOE_TILE = 256

F32 = jnp.float32
BF16 = jnp.bfloat16
NT_DIMS = (((1,), (1,)), ((), ()))
TN_DIMS = (((0,), (0,)), ((), ()))


def _cparams(sem):
    return pltpu.CompilerParams(dimension_semantics=sem, vmem_limit_bytes=VMEM_LIMIT)


def _rms(x, gain):
    return x * lax.rsqrt(jnp.mean(x * x, axis=-1, keepdims=True) + EPS) * gain


def _sigmoid(x):
    return 1.0 / (1.0 + jnp.exp(-x))


def _log_sigmoid(z):
    return jnp.minimum(z, 0.0) - jnp.log1p(jnp.exp(-jnp.abs(z)))


def _in_proj_kernel(*refs, n_parts, gates):
    parts = refs[:n_parts]
    gain_ref, w_ref = refs[n_parts], refs[n_parts + 1]
    pos = n_parts + 2
    if gates:
        wg_ref, wf_ref, bf_ref, wb_ref, bb_ref = refs[pos:pos + 5]
        pos += 5
    outs = refs[pos:]
    x = parts[0][...]
    for p in parts[1:]:
        x = x + p[...]
    oi = 0
    if n_parts > 1:
        outs[0][...] = x
        oi = 1
    h = _rms(x, gain_ref[...]).astype(BF16)
    proj_ref = outs[oi]
    n_out = proj_ref.shape[1]
    step = 512
    for j in range(n_out // step):
        proj_ref[:, j * step:(j + 1) * step] = jnp.dot(
            h, w_ref[:, j * step:(j + 1) * step], preferred_element_type=F32).astype(BF16)
    if gates:
        lf_ref, lb_ref = outs[oi + 1], outs[oi + 2]
        g = jnp.dot(h, wg_ref[...], preferred_element_type=F32).astype(BF16)
        zf = jnp.dot(g, wf_ref[...], preferred_element_type=F32) + bf_ref[...]
        zb = jnp.dot(g, wb_ref[...], preferred_element_type=F32) + bb_ref[...]
        lf_ref[...] = _log_sigmoid(zf) / C_GATE_NORM
        lb_ref[...] = _log_sigmoid(zb) / C_GATE_NORM


def _row_spec(tm, width, block_off=0):
    return pl.BlockSpec((tm, width), lambda i, o=block_off: (i + o, 0))


def _full_spec(shape):
    nd = len(shape)
    return pl.BlockSpec(shape, lambda i, _nd=nd: (0,) * _nd)


def _in_proj_call(n, parts, gain, w, gates=None):
    tm = ROW_TILE
    n_parts = len(parts)
    n_out = w.shape[1]
    in_specs = [_row_spec(tm, D_MODEL, off // tm) for _, off in parts]
    args = [a for a, _ in parts]
    in_specs += [_full_spec((1, D_MODEL)), _full_spec(w.shape)]
    args += [gain.reshape(1, D_MODEL), w]
    out_shape, out_specs = [], []
    if n_parts > 1:
        out_shape.append(jax.ShapeDtypeStruct((n, D_MODEL), F32))
        out_specs.append(_row_spec(tm, D_MODEL))
    out_shape.append(jax.ShapeDtypeStruct((n, n_out), BF16))
    out_specs.append(_row_spec(tm, n_out))
    if gates is not None:
        for a in gates:
            in_specs.append(_full_spec(a.shape))
            args.append(a)
        kw = C_HEADS * C_DK
        out_shape += [jax.ShapeDtypeStruct((n, kw), F32)] * 2
        out_specs += [_row_spec(tm, kw)] * 2
    return pl.pallas_call(
        functools.partial(_in_proj_kernel, n_parts=n_parts, gates=gates is not None),
        grid=(n // tm,),
        in_specs=in_specs,
        out_specs=out_specs,
        out_shape=out_shape,
        compiler_params=_cparams(("parallel",)),
        name="in_proj_gla" if gates is not None else "in_proj_attn",
    )(*args)


def _diff_attn_kernel(lam_ref, q_ref, k_ref, v_ref, bias_ref, gain_ref, o_ref, *, seq, out_scale):
    tq = q_ref.shape[0]
    qb = pl.program_id(2)
    q = q_ref[...]
    k = k_ref[...]
    lane = lax.broadcasted_iota(jnp.int32, q.shape, 1)
    scale = A_DH ** -0.5
    zero = jnp.zeros_like(q)
    off = pl.multiple_of((seq - tq) - qb * tq, LANES)
    bias = bias_ref[:, pl.ds(off, seq)]

    def softmax_map(qm):
        s = lax.dot_general(qm * scale, k, NT_DIMS, preferred_element_type=F32) + bias
        m = jnp.max(s, axis=-1, keepdims=True)
        p = jnp.exp(s - m)
        return p / jnp.sum(p, axis=-1, keepdims=True)

    p0 = softmax_map(jnp.where(lane < A_DH, q, zero))
    p1 = softmax_map(jnp.where(lane >= A_DH, q, zero))
    attn = (p0 - lam_ref[0] * p1).astype(BF16)
    o = jnp.dot(attn, v_ref[...], preferred_element_type=F32)
    o_ref[...] = (_rms(o, gain_ref[...]) * out_scale).astype(BF16)


def _t5_bucket(rel):
    half = T5_BUCKETS // 2
    max_exact = half // 2
    sign_off = jnp.where(rel > 0, half, 0)
    n = jnp.abs(rel)
    nf = jnp.maximum(n, 1).astype(F32)
    large = max_exact + (jnp.log(nf / max_exact) / math.log(T5_MAX_DIST / max_exact)
                         * (half - max_exact)).astype(jnp.int32)
    large = jnp.minimum(large, half - 1)
    return sign_off + jnp.where(n < max_exact, n, large)


def _t5_strip(t5_bias, seq, tq):
    n_rel = 2 * seq - 1
    vec = t5_bias[_t5_bucket(jnp.arange(n_rel) - (seq - 1))].astype(F32).T
    return _toeplitz(vec, tq, 2 * seq - tq)


def _toeplitz(vec, n_rows, n_cols):
    n = vec.shape[-1]
    assert n_cols + n_rows - 1 <= n
    tiled = jnp.tile(vec, (1,) * (vec.ndim - 1) + (n_rows + 1,))[..., :n_rows * (n + 1)]
    shifted = tiled.reshape(vec.shape[:-1] + (n_rows, n + 1))[..., :n_cols]
    return jnp.flip(shifted, axis=-2)


def _diff_attn(proj3, lam, strip, subln, out_scale):
    b, seq, _ = proj3.shape
    tq = ATT_TQ
    kblk, vblk = A_W // LANES, 2 * A_W // LANES
    return pl.pallas_call(
        functools.partial(_diff_attn_kernel, seq=seq, out_scale=out_scale),
        grid=(A_HEADS, b, seq // tq),
        in_specs=[
            pl.BlockSpec(memory_space=pltpu.SMEM),
            pl.BlockSpec((None, tq, LANES), lambda h, i, j: (i, j, h)),
            pl.BlockSpec((None, seq, LANES), lambda h, i, j: (i, 0, kblk + h)),
            pl.BlockSpec((None, seq, LANES), lambda h, i, j: (i, 0, vblk + h)),
            pl.BlockSpec((None, tq, 2 * seq - tq), lambda h, i, j: (h, 0, 0)),
            pl.BlockSpec((1, LANES), lambda h, i, j: (0, 0)),
        ],
        out_specs=pl.BlockSpec((None, tq, LANES), lambda h, i, j: (i, j, h)),
        out_shape=jax.ShapeDtypeStruct((b, seq, A_W), BF16),
        compiler_params=_cparams(("parallel", "parallel", "parallel")),
        name="diff_attn",
    )(lam, proj3, proj3, proj3, strip, subln.reshape(1, LANES))


def _na_block_geometry(rows):
    nblk = rows // NA_QROWS
    kh = min(NA_MAX_ROWS, rows)
    starts, classes, reps = [], [], []
    for j in range(nblk):
        ks = min(max(j * NA_QROWS - kh // 2, 0), rows - NA_KROWS)
        rel = (j * NA_QROWS - ks,) + tuple(
            min(max(r - kh // 2, 0), rows - kh) - ks for r in range(j * NA_QROWS, (j + 1) * NA_QROWS))
        starts.append(ks)
        if rel not in reps:
            reps.append(rel)
        classes.append(reps.index(rel))
    return starts, classes, reps


def _na_kernel(q_ref, k_ref, v_ref, tab_ref, o_ref, *, starts, classes):
    qn, kn = NA_QROWS * GRID_W, NA_KROWS * GRID_W
    scale = B_DH ** -0.5
    lane = lax.broadcasted_iota(jnp.int32, (qn, LANES), 1)
    for j, (ks, cls) in enumerate(zip(starts, classes)):
        q = q_ref[j * qn:(j + 1) * qn, :]
        kw = k_ref[ks * GRID_W:ks * GRID_W + kn, :]
        vw = v_ref[ks * GRID_W:ks * GRID_W + kn, :]
        zero = jnp.zeros_like(q)
        outs = []
        for hl in range(2):
            in_head = (lane >= hl * B_DH) & (lane < (hl + 1) * B_DH)
            qm = jnp.where(in_head, q, zero) * scale
            s = lax.dot_general(qm, kw, NT_DIMS, preferred_element_type=F32) + tab_ref[cls, hl]
            m = jnp.max(s, axis=-1, keepdims=True)
            p = jnp.exp(s - m)
            p = (p / jnp.sum(p, axis=-1, keepdims=True)).astype(BF16)
            outs.append(jnp.dot(p, vw, preferred_element_type=F32))
        o_ref[j * qn:(j + 1) * qn, :] = jnp.where(lane < B_DH, outs[0], outs[1]).astype(BF16)


def _na_attn(proj3, table, starts, classes):
    b, seq, _ = proj3.shape
    qblk = 3 * A_W // LANES
    kblk = qblk + B_W // LANES
    vblk = kblk + B_W // LANES
    ncls = table.shape[1]
    qn, kn = NA_QROWS * GRID_W, NA_KROWS * GRID_W
    return pl.pallas_call(
        functools.partial(_na_kernel, starts=tuple(starts), classes=tuple(classes)),
        grid=(B_HEADS // 2, b),
        in_specs=[
            pl.BlockSpec((None, seq, LANES), lambda h, i: (i, 0, qblk + h)),
            pl.BlockSpec((None, seq, LANES), lambda h, i: (i, 0, kblk + h)),
            pl.BlockSpec((None, seq, LANES), lambda h, i: (i, 0, vblk + h)),
            pl.BlockSpec((None, ncls, 2, qn, kn), lambda h, i: (h, 0, 0, 0, 0)),
        ],
        out_specs=pl.BlockSpec((None, seq, LANES), lambda h, i: (i, 0, h)),
        out_shape=jax.ShapeDtypeStruct((b, seq, B_W), BF16),
        compiler_params=_cparams(("parallel", "parallel")),
        name="na_attn",
    )(proj3, proj3, proj3, table)


def _out_proj_router_kernel(*refs, n_a):
    x_ref = refs[0]
    a_refs = refs[1:1 + n_a]
    w_refs = refs[1 + n_a:1 + 2 * n_a]
    gain_ref, wr_ref, br_ref, xo_ref, hn_ref, ids_ref, wts_ref = refs[1 + 2 * n_a:]
    acc = x_ref[...]
    for a, w in zip(a_refs, w_refs):
        acc = acc + jnp.dot(a[...], w[...], preferred_element_type=F32)
    xo_ref[...] = acc
    h = _rms(acc, gain_ref[...])
    hn_ref[...] = h
    logits = jnp.dot(h, wr_ref[...], preferred_element_type=F32,
                     precision=lax.Precision.HIGHEST) + br_ref[...]
    lane = lax.broadcasted_iota(jnp.int32, logits.shape, 1)
    neg = jnp.float32(-jnp.inf)
    big = jnp.int32(LANES)

    def masked_softmax(mask):
        z = jnp.where(mask, logits, neg)
        e = jnp.exp(z - jnp.max(z, axis=-1, keepdims=True))
        return e / jnp.sum(e, axis=-1, keepdims=True)

    def top1(p, mask):
        w = jnp.max(jnp.where(mask, p, -1.0), axis=-1, keepdims=True)
        idx = jnp.min(jnp.where(mask & (p == w), lane, big), axis=-1, keepdims=True)
        return w, idx

    is_grp = lane < N_GROUPS
    g_w, g_idx = top1(masked_softmax(is_grp), is_grp)
    e_lane = lane - N_GROUPS
    in_grp = (e_lane >= g_idx * EXP_PER_GROUP) & (e_lane < (g_idx + 1) * EXP_PER_GROUP)
    p_e = masked_softmax(in_grp)
    w1, i1 = top1(p_e, in_grp)
    rest = in_grp & (lane != i1)
    w2, i2 = top1(p_e, rest)
    denom = w1 + w2
    ids_ref[...] = jnp.where(lane == 0, i1 - N_GROUPS, i2 - N_GROUPS)
    wts_ref[...] = jnp.where(lane == 0, g_w * (w1 / denom), g_w * (w2 / denom))


def _out_proj_router(n, x, acts, ws, gain, w_router, b_router):
    tm = ROW_TILE
    n_a = len(acts)
    in_specs = [_row_spec(tm, D_MODEL)]
    in_specs += [_row_spec(tm, a.shape[1]) for a in acts]
    in_specs += [_full_spec(w.shape) for w in ws]
    in_specs += [_full_spec((1, D_MODEL)), _full_spec(w_router.shape), _full_spec((1, LANES))]
    return pl.pallas_call(
        functools.partial(_out_proj_router_kernel, n_a=n_a),
        grid=(n // tm,),
        in_specs=in_specs,
        out_specs=[_row_spec(tm, D_MODEL), _row_spec(tm, D_MODEL), _row_spec(tm, LANES), _row_spec(tm, LANES)],
        out_shape=[jax.ShapeDtypeStruct((n, D_MODEL), F32), jax.ShapeDtypeStruct((n, D_MODEL), F32),
                   jax.ShapeDtypeStruct((n, LANES), jnp.int32), jax.ShapeDtypeStruct((n, LANES), F32)],
        compiler_params=_cparams(("parallel",)),
        name="out_proj_router",
    )(x, *acts, *ws, gain.reshape(1, D_MODEL), w_router, b_router)


def _router_params(w_grp, b_grp, w_exp, b_exp):
    pad = LANES - N_GROUPS - N_EXPERTS
    w = jnp.concatenate([w_grp, w_exp, jnp.zeros((D_MODEL, pad), F32)], axis=1)
    b = jnp.concatenate([b_grp, b_exp, jnp.zeros((pad,), F32)]).reshape(1, LANES)
    return w, b


def _moe_plan(ids, wts, n):
    tm = MOE_TILE
    n_assign = 2 * n
    p_rows = n_assign + N_EXPERTS * tm
    e_flat = ids.T.reshape(-1)
    w_flat = wts.T.reshape(-1)
    e_sorted, order, w_sorted = lax.sort((e_flat, jnp.arange(n_assign, dtype=jnp.int32), w_flat),
                                         num_keys=1, is_stable=True)
    bounds = jnp.searchsorted(e_sorted, jnp.arange(N_EXPERTS + 1, dtype=jnp.int32)).astype(jnp.int32)
    off, counts = bounds[:-1], bounds[1:] - bounds[:-1]
    padded = (counts + tm - 1) // tm * tm
    p_end = jnp.cumsum(padded)
    p_off = p_end - padded
    nt = p_rows // tm
    tile_start = jnp.arange(nt, dtype=jnp.int32) * tm
    tile_expert = jnp.minimum(jnp.searchsorted(p_end, tile_start, side="right"), N_EXPERTS - 1).astype(jnp.int32)
    tile_count = jnp.clip(counts[tile_expert] - (tile_start - p_off[tile_expert]), 0, tm).astype(jnp.int32)
    n_used = (p_end[-1] // tm).astype(jnp.int32).reshape(1)
    in_tile = jnp.arange(tm, dtype=jnp.int32)[None, :]
    real = in_tile < tile_count[:, None]
    s_idx = jnp.clip((off[tile_expert] + tile_start - p_off[tile_expert])[:, None] + in_tile, 0, n_assign - 1)
    src = jnp.where(real, order[s_idx] % n, 0).reshape(p_rows)
    dest = jnp.where(real, order[s_idx], 0).reshape(p_rows)
    w_row = jnp.where(real, w_sorted[s_idx], 0.0).reshape(p_rows)
    return (src.reshape(nt, 1, tm), dest.reshape(nt, 1, tm), w_row.reshape(p_rows, 1), tile_expert,
            tile_count, n_used)


def _moe_kernel(te_ref, tc_ref, nu_ref, src_ref, dst_ref, wrow_ref, x_hbm, wg_ref, wu_ref, wd_ref, y_hbm,
                xbuf, ybuf, wgb, wub, wdb, gsem, ssem):
    tm = MOE_TILE
    i = pl.program_id(0)
    n_used = nu_ref[0]
    last = pl.num_programs(0) - 1

    def gather_copy(slot, r, tok):
        return pltpu.make_async_copy(x_hbm.at[pl.ds(tok, 1)], xbuf.at[slot, pl.ds(r, 1)], gsem.at[slot])

    def scatter_copy(slot, r, row):
        return pltpu.make_async_copy(ybuf.at[slot, pl.ds(r, 1)], y_hbm.at[pl.ds(row, 1)], ssem.at[slot])

    def for_rows(count, fn):
        def body(r, c):
            fn(r)
            return c

        @pl.when(count == tm)
        def _():
            lax.fori_loop(0, tm, body, 0, unroll=8)

        @pl.when(count < tm)
        def _():
            lax.fori_loop(0, count, body, 0)

    def wait_scatter(t):
        slot = t % 2
        count = tc_ref[t]

        @pl.when(count == tm)
        def _():
            pltpu.make_async_copy(ybuf.at[slot], y_hbm.at[pl.ds(0, tm)], ssem.at[slot]).wait()

        @pl.when(count < tm)
        def _():
            lax.fori_loop(0, count, lambda r, c: (scatter_copy(slot, r, 0).wait(), c)[1], 0)

    @pl.when(i < n_used)
    def _():
        slot = i % 2
        lax.fori_loop(0, tm, lambda r, c: (gather_copy(slot, r, src_ref[0, r]).start(), c)[1], 0, unroll=8)

    @pl.when((i >= 1) & (i <= n_used))
    def _():
        t = i - 1
        slot = t % 2
        pltpu.make_async_copy(x_hbm.at[pl.ds(0, tm)], xbuf.at[slot], gsem.at[slot]).wait()

        @pl.when(t >= 2)
        def _():
            wait_scatter(t - 2)

        new_expert = jnp.logical_or(t == 0, te_ref[t] != te_ref[jnp.maximum(t - 1, 0)])

        @pl.when(new_expert)
        def _():
            wgb[...] = wg_ref[...].astype(BF16)
            wub[...] = wu_ref[...].astype(BF16)
            wdb[...] = wd_ref[...].astype(BF16)

        x = xbuf[slot].astype(BF16)
        g = jnp.dot(x, wgb[...], preferred_element_type=F32)
        u = jnp.dot(x, wub[...], preferred_element_type=F32)
        hid = (g * _sigmoid(g) * u).astype(BF16)
        y = jnp.dot(hid, wdb[...], preferred_element_type=F32)
        ybuf[slot] = y * wrow_ref[...]

        for_rows(tc_ref[t], lambda r: scatter_copy(slot, r, dst_ref[0, r]).start())

    @pl.when(i == last)
    def _():
        wait_scatter(n_used - 1)

        @pl.when(n_used >= 2)
        def _():
            wait_scatter(n_used - 2)


def _moe_experts(n, hn, plan, w_gate, w_up, w_down):
    tm = MOE_TILE
    src, dest, w_row, tile_expert, tile_count, n_used = plan
    nt = src.shape[0]
    prev = lambda i: jnp.maximum(i - 1, 0)
    grid_spec = pltpu.PrefetchScalarGridSpec(
        num_scalar_prefetch=3,
        grid=(nt + 1,),
        in_specs=[
            pl.BlockSpec((None, 1, tm), lambda i, te, tc, nu: (jnp.minimum(i, nt - 1), 0, 0), memory_space=pltpu.SMEM),
            pl.BlockSpec((None, 1, tm), lambda i, te, tc, nu: (prev(i), 0, 0), memory_space=pltpu.SMEM),
            pl.BlockSpec((tm, 1), lambda i, te, tc, nu: (prev(i), 0)),
            pl.BlockSpec(memory_space=pl.ANY),
            pl.BlockSpec((None, D_MODEL, D_EXPERT), lambda i, te, tc, nu: (te[prev(i)], 0, 0)),
            pl.BlockSpec((None, D_MODEL, D_EXPERT), lambda i, te, tc, nu: (te[prev(i)], 0, 0)),
            pl.BlockSpec((None, D_EXPERT, D_MODEL), lambda i, te, tc, nu: (te[prev(i)], 0, 0)),
        ],
        out_specs=pl.BlockSpec(memory_space=pl.ANY),
        scratch_shapes=[
            pltpu.VMEM((2, tm, D_MODEL), F32),
            pltpu.VMEM((2, tm, D_MODEL), F32),
            pltpu.VMEM((D_MODEL, D_EXPERT), BF16),
            pltpu.VMEM((D_MODEL, D_EXPERT), BF16),
            pltpu.VMEM((D_EXPERT, D_MODEL), BF16),
            pltpu.SemaphoreType.DMA((2,)),
            pltpu.SemaphoreType.DMA((2,)),
        ],
    )
    return pl.pallas_call(
        _moe_kernel,
        grid_spec=grid_spec,
        out_shape=jax.ShapeDtypeStruct((2 * n, D_MODEL), F32),
        compiler_params=_cparams(("arbitrary",)),
        name="moe_experts",
    )(tile_expert, tile_count, n_used, src, dest, w_row, hn, w_gate, w_up, w_down)


def _gla_kernel(q_ref, k_ref, v_ref, g_ref, lf_ref, lb_ref, gain_ref, o_ref,
                bf_s, bb_s, sf_s, sb_s, stf, stb, *, seq):
    c_len = C_CHUNK
    nc = seq // c_len
    scale = C_DK ** -0.5
    row = lax.broadcasted_iota(jnp.int32, (c_len, C_DK), 0)
    shifts = [1 << s for s in range(int(math.log2(c_len)))]

    def rows(c):
        return pl.ds(pl.multiple_of(c * c_len, c_len), c_len)

    def cum_body(c, carry):
        y = lf_ref[rows(c), :]
        for s in shifts:
            y = y + jnp.where(row >= s, pltpu.roll(y, s, 0), 0.0)
        bf_s[rows(c), :] = y
        y = lb_ref[rows(c), :]
        for s in shifts:
            y = y + jnp.where(row < c_len - s, pltpu.roll(y, c_len - s, 0), 0.0)
        bb_s[rows(c), :] = y
        return carry

    lax.fori_loop(0, nc, cum_body, 0)

    stf[...] = jnp.zeros_like(stf)
    stb[...] = jnp.zeros_like(stb)

    def state_step(c, b_s, last_row, st, s_all):
        k = k_ref[rows(c), :].astype(F32)
        bc = b_s[rows(c), :]
        bl = bc[last_row:last_row + 1, :]
        kd = (k * jnp.exp(bl - bc)).astype(BF16)
        kv_t = lax.dot_general(v_ref[rows(c), :], kd, TN_DIMS, preferred_element_type=F32)
        s_all[c] = st[...].astype(BF16)
        st[...] = st[...] * jnp.exp(bl) + kv_t

    def state_body(i, carry):
        state_step(i, bf_s, c_len - 1, stf, sf_s)
        state_step(nc - 1 - i, bb_s, 0, stb, sb_s)
        return carry

    lax.fori_loop(0, nc, state_body, 0)

    ri = lax.broadcasted_iota(jnp.int32, (c_len, c_len), 0)
    ci = lax.broadcasted_iota(jnp.int32, (c_len, c_len), 1)

    def out_body(c, carry):
        q = q_ref[rows(c), :].astype(F32) * scale
        k = k_ref[rows(c), :].astype(F32)
        v = v_ref[rows(c), :]

        def direction(b_s, s_all, mask):
            bc = b_s[rows(c), :]
            qe = (q * jnp.exp(bc)).astype(BF16)
            ke = (k * jnp.exp(-bc)).astype(BF16)
            att = lax.dot_general(qe, ke, NT_DIMS, preferred_element_type=F32)
            att = jnp.where(mask, att, 0.0).astype(BF16)
            return (jnp.dot(att, v, preferred_element_type=F32)
                    + lax.dot_general(qe, s_all[c], NT_DIMS, preferred_element_type=F32))

        o = direction(bf_s, sf_s, ri >= ci) + direction(bb_s, sb_s, ri < ci)
        g = g_ref[rows(c), :].astype(F32)
        o_ref[rows(c), :] = (_rms(o, gain_ref[...]) * (g * _sigmoid(g))).astype(BF16)
        return carry

    lax.fori_loop(0, nc, out_body, 0)


def _gla(proj3, lf3, lb3, out_norm):
    b, seq, _ = proj3.shape
    kw = C_HEADS * C_DK
    k_blk = kw // C_DK
    v_blk = 2 * kw // C_DV
    g_blk = v_blk + C_HEADS
    nc = seq // C_CHUNK
    return pl.pallas_call(
        functools.partial(_gla_kernel, seq=seq),
        grid=(b, C_HEADS),
        in_specs=[
            pl.BlockSpec((None, seq, C_DK), lambda i, h: (i, 0, h)),
            pl.BlockSpec((None, seq, C_DK), lambda i, h: (i, 0, k_blk + h)),
            pl.BlockSpec((None, seq, C_DV), lambda i, h: (i, 0, v_blk + h)),
            pl.BlockSpec((None, seq, C_DV), lambda i, h: (i, 0, g_blk + h)),
            pl.BlockSpec((None, seq, C_DK), lambda i, h: (i, 0, h)),
            pl.BlockSpec((None, seq, C_DK), lambda i, h: (i, 0, h)),
            pl.BlockSpec((1, C_DV), lambda i, h: (0, 0)),
        ],
        out_specs=pl.BlockSpec((None, seq, C_DV), lambda i, h: (i, 0, h)),
        out_shape=jax.ShapeDtypeStruct((b, seq, C_HEADS * C_DV), BF16),
        scratch_shapes=[
            pltpu.VMEM((seq, C_DK), F32),
            pltpu.VMEM((seq, C_DK), F32),
            pltpu.VMEM((nc, C_DV, C_DK), BF16),
            pltpu.VMEM((nc, C_DV, C_DK), BF16),
            pltpu.VMEM((C_DV, C_DK), F32),
            pltpu.VMEM((C_DV, C_DK), F32),
        ],
        compiler_params=_cparams(("parallel", "parallel")),
        name="gla",
    )(proj3, proj3, proj3, proj3, lf3, lb3, out_norm.reshape(1, C_DV))


def _final_kernel(x_ref, y0_ref, y1_ref, gain_ref, o_ref):
    o_ref[...] = _rms(x_ref[...] + y0_ref[...] + y1_ref[...], gain_ref[...])


def _final_norm(n, x, y, gain):
    tm = ROW_TILE
    return pl.pallas_call(
        _final_kernel,
        grid=(n // tm,),
        in_specs=[_row_spec(tm, D_MODEL), _row_spec(tm, D_MODEL), _row_spec(tm, D_MODEL, n // tm),
                  _full_spec((1, D_MODEL))],
        out_specs=_row_spec(tm, D_MODEL),
        out_shape=jax.ShapeDtypeStruct((n, D_MODEL), F32),
        compiler_params=_cparams(("parallel",)),
        name="final_norm",
    )(x, y, y, gain.reshape(1, D_MODEL))


def _moe_layer(n, hn, ids_pad, wts_pad, w_gate, w_up, w_down):
    plan = _moe_plan(ids_pad[:, :2], wts_pad[:, :2], n)
    return _moe_experts(n, hn, plan, w_gate, w_up, w_down)


def kernel(x, t5_bias, norm_mix, norm_ffn, norm_final, ev_w_in, ev_lambda, ev_subln, ev_rpb, ev_w_out,
           od_w_in, od_w_gk_fwd, od_b_gk_fwd, od_w_gk_bwd, od_b_gk_bwd, od_out_norm, od_w_out,
           moe_w_grp, moe_b_grp, moe_w_exp, moe_b_exp, moe_w_gate, moe_w_up, moe_w_down):
    b, seq, d = x.shape
    n = b * seq
    rows = seq // GRID_W
    assert d == D_MODEL and n % ROW_TILE == 0 and seq % ATT_TQ == 0 and seq % C_CHUNK == 0
    assert rows % NA_QROWS == 0 and rows >= NA_KROWS and MOE_TILE <= ROW_TILE
    x2 = x.reshape(n, d)

    (proj,) = _in_proj_call(n, [(x2, 0)], norm_mix[0], ev_w_in[0].astype(BF16))
    proj3 = proj.reshape(b, seq, proj.shape[1])
    lam_init = 0.8 - 0.6 * math.exp(-0.3 * 0)
    lp = ev_lambda[0].astype(F32)
    lam = (jnp.exp(jnp.sum(lp[0] * lp[1])) - jnp.exp(jnp.sum(lp[2] * lp[3])) + lam_init).reshape(1)
    strip = _t5_strip(t5_bias, seq, ATT_TQ)
    a_diff = _diff_attn(proj3, lam, strip, ev_subln[0], 1.0 - lam_init)

    starts, classes, _ = _na_block_geometry(rows)
    table = _na_table(ev_rpb[0], rows)
    a_na = _na_attn(proj3, table, starts, classes)

    w_out = ev_w_out[0].astype(BF16)
    w_r, b_r = _router_params(moe_w_grp[0], moe_b_grp[0], moe_w_exp[0], moe_b_exp[0])
    x1, hn, ids, wts = _out_proj_router(
        n, x2, [a_diff.reshape(n, A_W), a_na.reshape(n, B_W)], [w_out[:A_W], w_out[A_W:]],
        norm_ffn[0], w_r, b_r)
    y = _moe_layer(n, hn, ids, wts, moe_w_gate[0], moe_w_up[0], moe_w_down[0])

    kw = C_HEADS * C_DK
    main_w = 2 * kw + 2 * C_HEADS * C_DV
    w_in = od_w_in[0]
    w_gates = jnp.pad(w_in[:, main_w:], ((0, 0), (0, LANES - 2 * C_GATE_RANK))).astype(BF16)
    wf = jnp.pad(od_w_gk_fwd[0], ((0, LANES - C_GATE_RANK), (0, 0))).astype(BF16)
    wb = jnp.pad(od_w_gk_bwd[0], ((C_GATE_RANK, LANES - 2 * C_GATE_RANK), (0, 0))).astype(BF16)
    x2b, proj, lf, lb = _in_proj_call(
        n, [(x1, 0), (y, 0), (y, n)], norm_mix[1], w_in[:, :main_w].astype(BF16),
        gates=[w_gates, wf, od_b_gk_fwd[0].reshape(1, kw), wb, od_b_gk_bwd[0].reshape(1, kw)])
    a_gla = _gla(proj.reshape(b, seq, main_w), lf.reshape(b, seq, kw), lb.reshape(b, seq, kw), od_out_norm[0])

    w_r, b_r = _router_params(moe_w_grp[1], moe_b_grp[1], moe_w_exp[1], moe_b_exp[1])
    x3, hn, ids, wts = _out_proj_router(
        n, x2b, [a_gla.reshape(n, C_HEADS * C_DV)], [od_w_out[0].astype(BF16)], norm_ffn[1], w_r, b_r)
    y = _moe_layer(n, hn, ids, wts, moe_w_gate[1], moe_w_up[1], moe_w_down[1])

    return _final_norm(n, x3, y, norm_final).reshape(b, seq, d)


def _na_table(rpb, rows):
    kh = min(NA_MAX_ROWS, rows)
    starts, classes, reps = _na_block_geometry(rows)
    qn, kn = NA_QROWS * GRID_W, NA_KROWS * GRID_W
    q_row, q_col = jnp.arange(qn) // GRID_W, jnp.arange(qn) % GRID_W
    k_row, k_col = jnp.arange(kn) // GRID_W, jnp.arange(kn) % GRID_W
    col_start = jnp.clip(q_col - NA_COLS // 2, 0, GRID_W - NA_COLS)
    col_ok = (k_col[None, :] >= col_start[:, None]) & (k_col[None, :] < col_start[:, None] + NA_COLS)
    side = GRID_W - NA_COLS
    col_bias = _toeplitz(jnp.pad(rpb.astype(F32), ((0, 0), (0, 0), (side, side))), GRID_W, GRID_W)
    per_class = []
    for cls in range(len(reps)):
        j = classes.index(cls)
        q_abs = j * NA_QROWS + q_row
        k_abs = starts[j] + k_row
        row_start = jnp.clip(q_abs - kh // 2, 0, rows - kh)
        row_ok = (k_abs[None, :] >= row_start[:, None]) & (k_abs[None, :] < row_start[:, None] + kh)
        dr = jnp.clip(starts[j] + jnp.arange(NA_KROWS)[None, :] - (j * NA_QROWS + jnp.arange(NA_QROWS))[:, None]
                      + NA_MAX_ROWS - 1, 0, 2 * NA_MAX_ROWS - 2)
        blocks = col_bias[:, dr]
        bias = jnp.transpose(blocks, (0, 1, 3, 2, 4)).reshape(B_HEADS, qn, kn)
        per_class.append(jnp.where((row_ok & col_ok)[None], bias, -jnp.inf))
    t = jnp.stack(per_class, axis=1)
    return jnp.transpose(t.reshape(B_HEADS // 2, 2, len(reps), qn, kn), (0, 2, 1, 3, 4))
```

```python
import functools
import math

import jax
import jax.numpy as jnp
from jax import lax
from jax.experimental import pallas as pl
from jax.experimental.pallas import tpu as pltpu

D_MODEL = 1024
GRID_W = 64
A_HEADS = 4
A_DH = 64
B_HEADS = 8
B_DH = 64
NA_MAX_ROWS = 8
NA_COLS = 16
T5_BUCKETS = 32
T5_MAX_DIST = 128
C_HEADS = 4
C_DK = 128
C_DV = 256
C_GATE_RANK = 16
C_GATE_NORM = 16.0
C_CHUNK = 64
N_GROUPS = 4
EXP_PER_GROUP = 8
N_EXPERTS = N_GROUPS * EXP_PER_GROUP
D_EXPERT = 512
EPS = 1e-6

A_W = A_HEADS * 2 * A_DH
B_W = B_HEADS * B_DH
LANES = 128
VMEM_LIMIT = 56 * 1024 * 1024

ROW_TILE = 512
ATT_TQ = 256
NA_QROWS = 4
NA_KROWS = 12
MOE_TILE = 256

F32 = jnp.float32
BF16 = jnp.bfloat16
NT_DIMS = (((1,), (1,)), ((), ()))
TN_DIMS = (((0,), (0,)), ((), ()))


def _cparams(sem):
    return pltpu.CompilerParams(dimension_semantics=sem, vmem_limit_bytes=VMEM_LIMIT)


def _rms(x, gain):
    return x * lax.rsqrt(jnp.mean(x * x, axis=-1, keepdims=True) + EPS) * gain


def _sigmoid(x):
    return 1.0 / (1.0 + jnp.exp(-x))


def _log_sigmoid(z):
    return jnp.minimum(z, 0.0) - jnp.log1p(jnp.exp(-jnp.abs(z)))


def _in_proj_kernel(*refs, n_parts, gates):
    parts = refs[:n_parts]
    gain_ref, w_ref = refs[n_parts], refs[n_parts + 1]
    pos = n_parts + 2
    if gates:
        wg_ref, wf_ref, bf_ref, wb_ref, bb_ref = refs[pos:pos + 5]
        pos += 5
    outs = refs[pos:]
    x = parts[0][...]
    for p in parts[1:]:
        x = x + p[...]
    oi = 0
    if n_parts > 1:
        outs[0][...] = x
        oi = 1
    h = _rms(x, gain_ref[...]).astype(BF16)
    proj_ref = outs[oi]
    n_out = proj_ref.shape[1]
    step = 512
    for j in range(n_out // step):
        proj_ref[:, j * step:(j + 1) * step] = jnp.dot(
            h, w_ref[:, j * step:(j + 1) * step], preferred_element_type=F32).astype(BF16)
    if gates:
        lf_ref, lb_ref = outs[oi + 1], outs[oi + 2]
        g = jnp.dot(h, wg_ref[...], preferred_element_type=F32).astype(BF16)
        zf = jnp.dot(g, wf_ref[...], preferred_element_type=F32) + bf_ref[...]
        zb = jnp.dot(g, wb_ref[...], preferred_element_type=F32) + bb_ref[...]
        lf_ref[...] = _log_sigmoid(zf) / C_GATE_NORM
        lb_ref[...] = _log_sigmoid(zb) / C_GATE_NORM


def _row_spec(tm, width, block_off=0):
    return pl.BlockSpec((tm, width), lambda i, o=block_off: (i + o, 0))


def _full_spec(shape):
    nd = len(shape)
    return pl.BlockSpec(shape, lambda i, _nd=nd: (0,) * _nd)


def _in_proj_call(n, parts, gain, w, gates=None):
    tm = ROW_TILE
    n_parts = len(parts)
    n_out = w.shape[1]
    in_specs = [_row_spec(tm, D_MODEL, off // tm) for _, off in parts]
    args = [a for a, _ in parts]
    in_specs += [_full_spec((1, D_MODEL)), _full_spec(w.shape)]
    args += [gain.reshape(1, D_MODEL), w]
    out_shape, out_specs = [], []
    if n_parts > 1:
        out_shape.append(jax.ShapeDtypeStruct((n, D_MODEL), F32))
        out_specs.append(_row_spec(tm, D_MODEL))
    out_shape.append(jax.ShapeDtypeStruct((n, n_out), BF16))
    out_specs.append(_row_spec(tm, n_out))
    if gates is not None:
        for a in gates:
            in_specs.append(_full_spec(a.shape))
            args.append(a)
        kw = C_HEADS * C_DK
        out_shape += [jax.ShapeDtypeStruct((n, kw), F32)] * 2
        out_specs += [_row_spec(tm, kw)] * 2
    return pl.pallas_call(
        functools.partial(_in_proj_kernel, n_parts=n_parts, gates=gates is not None),
        grid=(n // tm,),
        in_specs=in_specs,
        out_specs=out_specs,
        out_shape=out_shape,
        compiler_params=_cparams(("parallel",)),
        name="in_proj_gla" if gates is not None else "in_proj_attn",
    )(*args)


def _diff_attn_kernel(lam_ref, q_ref, k_ref, v_ref, bias_ref, gain_ref, o_ref, *, seq, out_scale):
    tq = q_ref.shape[0]
    qb = pl.program_id(2)
    q = q_ref[...]
    k = k_ref[...]
    lane = lax.broadcasted_iota(jnp.int32, q.shape, 1)
    scale = A_DH ** -0.5
    zero = jnp.zeros_like(q)
    off = pl.multiple_of((seq - tq) - qb * tq, LANES)
    bias = bias_ref[:, pl.ds(off, seq)]

    def softmax_map(qm):
        s = lax.dot_general(qm * scale, k, NT_DIMS, preferred_element_type=F32) + bias
        m = jnp.max(s, axis=-1, keepdims=True)
        p = jnp.exp(s - m)
        return p / jnp.sum(p, axis=-1, keepdims=True)

    p0 = softmax_map(jnp.where(lane < A_DH, q, zero))
    p1 = softmax_map(jnp.where(lane >= A_DH, q, zero))
    attn = (p0 - lam_ref[0] * p1).astype(BF16)
    o = jnp.dot(attn, v_ref[...], preferred_element_type=F32)
    o_ref[...] = (_rms(o, gain_ref[...]) * out_scale).astype(BF16)


def _t5_bucket(rel):
    half = T5_BUCKETS // 2
    max_exact = half // 2
    sign_off = jnp.where(rel > 0, half, 0)
    n = jnp.abs(rel)
    nf = jnp.maximum(n, 1).astype(F32)
    large = max_exact + (jnp.log(nf / max_exact) / math.log(T5_MAX_DIST / max_exact)
                         * (half - max_exact)).astype(jnp.int32)
    large = jnp.minimum(large, half - 1)
    return sign_off + jnp.where(n < max_exact, n, large)


def _t5_strip(t5_bias, seq, tq):
    n_rel = 2 * seq - 1
    vec = t5_bias[_t5_bucket(jnp.arange(n_rel) - (seq - 1))].astype(F32).T
    return _toeplitz(vec, tq, 2 * seq - tq)


def _toeplitz(vec, n_rows, n_cols):
    n = vec.shape[-1]
    lead = vec.shape[:-1]
    assert n_cols + n_rows - 1 <= n and n_cols <= n - 1
    rolled = jnp.roll(vec, -(n_rows - 1), axis=-1)
    reps = -(-(n_rows * (n - 1)) // n)
    flat = jnp.broadcast_to(rolled[..., None, :], lead + (reps, n)).reshape(lead + (reps * n,))
    return flat[..., :n_rows * (n - 1)].reshape(lead + (n_rows, n - 1))[..., :n_cols]


def _diff_attn(proj3, lam, strip, subln, out_scale):
    b, seq, _ = proj3.shape
    tq = ATT_TQ
    kblk, vblk = A_W // LANES, 2 * A_W // LANES
    return pl.pallas_call(
        functools.partial(_diff_attn_kernel, seq=seq, out_scale=out_scale),
        grid=(A_HEADS, b, seq // tq),
        in_specs=[
            pl.BlockSpec(memory_space=pltpu.SMEM),
            pl.BlockSpec((None, tq, LANES), lambda h, i, j: (i, j, h)),
            pl.BlockSpec((None, seq, LANES), lambda h, i, j: (i, 0, kblk + h)),
            pl.BlockSpec((None, seq, LANES), lambda h, i, j: (i, 0, vblk + h)),
            pl.BlockSpec((None, tq, 2 * seq - tq), lambda h, i, j: (h, 0, 0)),
            pl.BlockSpec((1, LANES), lambda h, i, j: (0, 0)),
        ],
        out_specs=pl.BlockSpec((None, tq, LANES), lambda h, i, j: (i, j, h)),
        out_shape=jax.ShapeDtypeStruct((b, seq, A_W), BF16),
        compiler_params=_cparams(("parallel", "parallel", "parallel")),
        name="diff_attn",
    )(lam, proj3, proj3, proj3, strip, subln.reshape(1, LANES))


def _na_block_geometry(rows):
    nblk = rows // NA_QROWS
    kh = min(NA_MAX_ROWS, rows)
    starts, classes, reps = [], [], []
    for j in range(nblk):
        ks = min(max(j * NA_QROWS - kh // 2, 0), rows - NA_KROWS)
        rel = (j * NA_QROWS - ks,) + tuple(
            min(max(r - kh // 2, 0), rows - kh) - ks for r in range(j * NA_QROWS, (j + 1) * NA_QROWS))
        starts.append(ks)
        if rel not in reps:
            reps.append(rel)
        classes.append(reps.index(rel))
    return starts, classes, reps


def _na_kernel(q_ref, k_ref, v_ref, tab_ref, o_ref, *, starts, classes):
    qn, kn = NA_QROWS * GRID_W, NA_KROWS * GRID_W
    scale = B_DH ** -0.5
    lane = lax.broadcasted_iota(jnp.int32, (qn, LANES), 1)
    for j, (ks, cls) in enumerate(zip(starts, classes)):
        q = q_ref[j * qn:(j + 1) * qn, :]
        kw = k_ref[ks * GRID_W:ks * GRID_W + kn, :]
        vw = v_ref[ks * GRID_W:ks * GRID_W + kn, :]
        zero = jnp.zeros_like(q)
        outs = []
        for hl in range(2):
            in_head = (lane >= hl * B_DH) & (lane < (hl + 1) * B_DH)
            qm = jnp.where(in_head, q, zero) * scale
            s = lax.dot_general(qm, kw, NT_DIMS, preferred_element_type=F32) + tab_ref[cls, hl]
            m = jnp.max(s, axis=-1, keepdims=True)
            p = jnp.exp(s - m)
            denom = jnp.sum(p, axis=-1, keepdims=True)
            outs.append(jnp.dot(p.astype(BF16), vw, preferred_element_type=F32) / denom)
        o_ref[j * qn:(j + 1) * qn, :] = jnp.where(lane < B_DH, outs[0], outs[1]).astype(BF16)


def _na_attn(proj3, table, starts, classes):
    b, seq, _ = proj3.shape
    qblk = 3 * A_W // LANES
    kblk = qblk + B_W // LANES
    vblk = kblk + B_W // LANES
    ncls = table.shape[1]
    qn, kn = NA_QROWS * GRID_W, NA_KROWS * GRID_W
    return pl.pallas_call(
        functools.partial(_na_kernel, starts=tuple(starts), classes=tuple(classes)),
        grid=(B_HEADS // 2, b),
        in_specs=[
            pl.BlockSpec((None, seq, LANES), lambda h, i: (i, 0, qblk + h)),
            pl.BlockSpec((None, seq, LANES), lambda h, i: (i, 0, kblk + h)),
            pl.BlockSpec((None, seq, LANES), lambda h, i: (i, 0, vblk + h)),
            pl.BlockSpec((None, ncls, 2, qn, kn), lambda h, i: (h, 0, 0, 0, 0)),
        ],
        out_specs=pl.BlockSpec((None, seq, LANES), lambda h, i: (i, 0, h)),
        out_shape=jax.ShapeDtypeStruct((b, seq, B_W), BF16),
        compiler_params=_cparams(("parallel", "parallel")),
        name="na_attn",
    )(proj3, proj3, proj3, table)


def _out_proj_router_kernel(*refs, n_a):
    x_ref = refs[0]
    a_refs = refs[1:1 + n_a]
    w_refs = refs[1 + n_a:1 + 2 * n_a]
    gain_ref, wr_ref, br_ref, xo_ref, hn_ref, ids_ref, wts_ref = refs[1 + 2 * n_a:]
    acc = x_ref[...]
    for a, w in zip(a_refs, w_refs):
        acc = acc + jnp.dot(a[...], w[...], preferred_element_type=F32)
    xo_ref[...] = acc
    h = _rms(acc, gain_ref[...])
    hn_ref[...] = h
    h_hi = h.astype(BF16)
    h_lo = (h - h_hi.astype(F32)).astype(BF16)
    hi = jnp.dot(h_hi, wr_ref[...], preferred_element_type=F32)
    logits = (hi[:, :LANES] + hi[:, LANES:]
              + jnp.dot(h_lo, wr_ref[:, :LANES], preferred_element_type=F32) + br_ref[...])
    lane = lax.broadcasted_iota(jnp.int32, logits.shape, 1)
    neg = jnp.float32(-jnp.inf)
    big = jnp.int32(LANES)

    def masked_softmax(mask):
        z = jnp.where(mask, logits, neg)
        e = jnp.exp(z - jnp.max(z, axis=-1, keepdims=True))
        return e / jnp.sum(e, axis=-1, keepdims=True)

    def top1(p, mask):
        w = jnp.max(jnp.where(mask, p, -1.0), axis=-1, keepdims=True)
        idx = jnp.min(jnp.where(mask & (p == w), lane, big), axis=-1, keepdims=True)
        return w, idx

    is_grp = lane < N_GROUPS
    g_w, g_idx = top1(masked_softmax(is_grp), is_grp)
    e_lane = lane - N_GROUPS
    in_grp = (e_lane >= g_idx * EXP_PER_GROUP) & (e_lane < (g_idx + 1) * EXP_PER_GROUP)
    p_e = masked_softmax(in_grp)
    w1, i1 = top1(p_e, in_grp)
    rest = in_grp & (lane != i1)
    w2, i2 = top1(p_e, rest)
    denom = w1 + w2
    ids_ref[...] = jnp.where(lane == 0, i1 - N_GROUPS, i2 - N_GROUPS)
    wts_ref[...] = jnp.where(lane == 0, g_w * (w1 / denom), g_w * (w2 / denom))


def _out_proj_router(n, x, acts, ws, gain, w_router, b_router):
    tm = ROW_TILE
    n_a = len(acts)
    in_specs = [_row_spec(tm, D_MODEL)]
    in_specs += [_row_spec(tm, a.shape[1]) for a in acts]
    in_specs += [_full_spec(w.shape) for w in ws]
    in_specs += [_full_spec((1, D_MODEL)), _full_spec(w_router.shape), _full_spec((1, LANES))]
    return pl.pallas_call(
        functools.partial(_out_proj_router_kernel, n_a=n_a),
        grid=(n // tm,),
        in_specs=in_specs,
        out_specs=[_row_spec(tm, D_MODEL), _row_spec(tm, D_MODEL), _row_spec(tm, LANES), _row_spec(tm, LANES)],
        out_shape=[jax.ShapeDtypeStruct((n, D_MODEL), F32), jax.ShapeDtypeStruct((n, D_MODEL), F32),
                   jax.ShapeDtypeStruct((n, LANES), jnp.int32), jax.ShapeDtypeStruct((n, LANES), F32)],
        compiler_params=_cparams(("parallel",)),
        name="out_proj_router",
    )(x, *acts, *ws, gain.reshape(1, D_MODEL), w_router, b_router)


def _router_params(w_grp, b_grp, w_exp, b_exp):
    pad = LANES - N_GROUPS - N_EXPERTS
    w = jnp.concatenate([w_grp, w_exp, jnp.zeros((D_MODEL, pad), F32)], axis=1)
    b = jnp.concatenate([b_grp, b_exp, jnp.zeros((pad,), F32)]).reshape(1, LANES)
    w_hi = w.astype(BF16)
    w_lo = (w - w_hi.astype(F32)).astype(BF16)
    return jnp.concatenate([w_hi, w_lo], axis=1), b


def _moe_plan(ids, wts, n):
    tm = MOE_TILE
    n_assign = 2 * n
    p_rows = n_assign + N_EXPERTS * tm
    e_flat = ids.T.reshape(-1)
    w_flat = wts.T.reshape(-1)
    idx_bits = max(n_assign - 1, 1).bit_length()
    assert idx_bits + N_EXPERTS.bit_length() <= 31
    key = jnp.sort((e_flat << idx_bits) | jnp.arange(n_assign, dtype=jnp.int32))
    order = key & ((1 << idx_bits) - 1)
    w_sorted = w_flat[order]
    experts = jnp.arange(N_EXPERTS, dtype=jnp.int32)
    counts = jnp.sum((key >> idx_bits)[None, :] == experts[:, None], axis=1, dtype=jnp.int32)
    off = jnp.cumsum(counts) - counts
    padded = (counts + tm - 1) // tm * tm
    p_end = jnp.cumsum(padded)
    p_off = p_end - padded
    nt = p_rows // tm
    tile_start = jnp.arange(nt, dtype=jnp.int32) * tm
    tile_expert = jnp.minimum(jnp.sum(p_end[None, :] <= tile_start[:, None], axis=1, dtype=jnp.int32),
                              N_EXPERTS - 1)
    tile_count = jnp.clip(counts[tile_expert] - (tile_start - p_off[tile_expert]), 0, tm).astype(jnp.int32)
    n_used = (p_end[-1] // tm).astype(jnp.int32).reshape(1)
    in_tile = jnp.arange(tm, dtype=jnp.int32)[None, :]
    real = in_tile < tile_count[:, None]
    s_idx = jnp.clip((off[tile_expert] + tile_start - p_off[tile_expert])[:, None] + in_tile, 0, n_assign - 1)
    src = jnp.where(real, order[s_idx] % n, 0).reshape(p_rows)
    dest = jnp.where(real, order[s_idx], 0).reshape(p_rows)
    w_row = jnp.where(real, w_sorted[s_idx], 0.0).reshape(p_rows)
    return (src.reshape(nt, 1, tm), dest.reshape(nt, 1, tm), w_row.reshape(p_rows, 1), tile_expert,
            tile_count, n_used)


def _moe_kernel(te_ref, tc_ref, nu_ref, src_ref, dst_ref, wrow_ref, x_hbm, wg_ref, wu_ref, wd_ref, y_hbm,
                xbuf, ybuf, wgb, wub, wdb, gsem, ssem):
    tm = MOE_TILE
    i = pl.program_id(0)
    n_used = nu_ref[0]
    last = pl.num_programs(0) - 1

    def gather_copy(slot, r, tok):
        return pltpu.make_async_copy(x_hbm.at[pl.ds(tok, 1)], xbuf.at[slot, pl.ds(r, 1)], gsem.at[slot])

    def scatter_copy(slot, r, row):
        return pltpu.make_async_copy(ybuf.at[slot, pl.ds(r, 1)], y_hbm.at[pl.ds(row, 1)], ssem.at[slot])

    def start_all(make):
        def body(r2, c):
            make(2 * r2).start(priority=0)
            make(2 * r2 + 1).start(priority=1)
            return c
        lax.fori_loop(0, tm // 2, body, 0, unroll=4)

    def start_rows(count, make):
        @pl.when(count == tm)
        def _():
            start_all(make)

        @pl.when(count < tm)
        def _():
            lax.fori_loop(0, count, lambda r, c: (make(r).start(), c)[1], 0)

    def wait_scatter(t):
        slot = t % 2
        count = tc_ref[t]

        @pl.when(count == tm)
        def _():
            pltpu.make_async_copy(ybuf.at[slot], y_hbm.at[pl.ds(0, tm)], ssem.at[slot]).wait()

        @pl.when(count < tm)
        def _():
            lax.fori_loop(0, count, lambda r, c: (scatter_copy(slot, r, 0).wait(), c)[1], 0)

    @pl.when(i < n_used)
    def _():
        slot = i % 2
        start_all(lambda r: gather_copy(slot, r, src_ref[0, r]))

    @pl.when((i >= 1) & (i <= n_used))
    def _():
        t = i - 1
        slot = t % 2
        pltpu.make_async_copy(x_hbm.at[pl.ds(0, tm)], xbuf.at[slot], gsem.at[slot]).wait()

        @pl.when(t >= 2)
        def _():
            wait_scatter(t - 2)

        new_expert = jnp.logical_or(t == 0, te_ref[t] != te_ref[jnp.maximum(t - 1, 0)])

        @pl.when(new_expert)
        def _():
            wgb[...] = wg_ref[...].astype(BF16)
            wub[...] = wu_ref[...].astype(BF16)
            wdb[...] = wd_ref[...].astype(BF16)

        x = xbuf[slot].astype(BF16)
        g = jnp.dot(x, wgb[...], preferred_element_type=F32)
        u = jnp.dot(x, wub[...], preferred_element_type=F32)
        hid = (g * _sigmoid(g) * u).astype(BF16)
        y = jnp.dot(hid, wdb[...], preferred_element_type=F32)
        ybuf[slot] = y * wrow_ref[...]

        start_rows(tc_ref[t], lambda r: scatter_copy(slot, r, dst_ref[0, r]))

    @pl.when(i == last)
    def _():
        wait_scatter(n_used - 1)

        @pl.when(n_used >= 2)
        def _():
            wait_scatter(n_used - 2)


def _moe_experts(n, hn, plan, w_gate, w_up, w_down, layer):
    tm = MOE_TILE
    src, dest, w_row, tile_expert, tile_count, n_used = plan
    nt = src.shape[0]
    prev = lambda i: jnp.maximum(i - 1, 0)
    grid_spec = pltpu.PrefetchScalarGridSpec(
        num_scalar_prefetch=3,
        grid=(nt + 1,),
        in_specs=[
            pl.BlockSpec((None, 1, tm), lambda i, te, tc, nu: (jnp.minimum(i, nt - 1), 0, 0), memory_space=pltpu.SMEM),
            pl.BlockSpec((None, 1, tm), lambda i, te, tc, nu: (prev(i), 0, 0), memory_space=pltpu.SMEM),
            pl.BlockSpec((tm, 1), lambda i, te, tc, nu: (prev(i), 0)),
            pl.BlockSpec(memory_space=pl.ANY),
            pl.BlockSpec((None, None, D_MODEL, D_EXPERT), lambda i, te, tc, nu: (layer, te[prev(i)], 0, 0)),
            pl.BlockSpec((None, None, D_MODEL, D_EXPERT), lambda i, te, tc, nu: (layer, te[prev(i)], 0, 0)),
            pl.BlockSpec((None, None, D_EXPERT, D_MODEL), lambda i, te, tc, nu: (layer, te[prev(i)], 0, 0)),
        ],
        out_specs=pl.BlockSpec(memory_space=pl.ANY),
        scratch_shapes=[
            pltpu.VMEM((2, tm, D_MODEL), F32),
            pltpu.VMEM((2, tm, D_MODEL), F32),
            pltpu.VMEM((D_MODEL, D_EXPERT), BF16),
            pltpu.VMEM((D_MODEL, D_EXPERT), BF16),
            pltpu.VMEM((D_EXPERT, D_MODEL), BF16),
            pltpu.SemaphoreType.DMA((2,)),
            pltpu.SemaphoreType.DMA((2,)),
        ],
    )
    return pl.pallas_call(
        _moe_kernel,
        grid_spec=grid_spec,
        out_shape=jax.ShapeDtypeStruct((2 * n, D_MODEL), F32),
        compiler_params=_cparams(("arbitrary",)),
        name="moe_experts",
    )(tile_expert, tile_count, n_used, src, dest, w_row, hn, w_gate, w_up, w_down)


def _gla_kernel(q_ref, k_ref, v_ref, g_ref, lf_ref, lb_ref, gain_ref, o_ref,
                bf_s, bb_s, s_all, stf, stb, *, seq):
    c_len = C_CHUNK
    nc = seq // c_len
    scale = C_DK ** -0.5
    row = lax.broadcasted_iota(jnp.int32, (c_len, C_DK), 0)
    shifts = [1 << s for s in range(int(math.log2(c_len)))]

    def rows(c):
        return pl.ds(pl.multiple_of(c * c_len, c_len), c_len)

    def cum_body(c, carry):
        y = lf_ref[rows(c), :]
        for s in shifts:
            y = y + jnp.where(row >= s, pltpu.roll(y, s, 0), 0.0)
        bf_s[rows(c), :] = y
        y = lb_ref[rows(c), :]
        for s in shifts:
            y = y + jnp.where(row < c_len - s, pltpu.roll(y, c_len - s, 0), 0.0)
        bb_s[rows(c), :] = y
        return carry

    lax.fori_loop(0, nc, cum_body, 0)

    stf[...] = jnp.zeros_like(stf)
    stb[...] = jnp.zeros_like(stb)

    def state_step(c, b_s, last_row, st, lane0):
        k = k_ref[rows(c), :].astype(F32)
        bc = b_s[rows(c), :]
        bl = bc[last_row:last_row + 1, :]
        kd = (k * jnp.exp(bl - bc)).astype(BF16)
        kv_t = lax.dot_general(v_ref[rows(c), :], kd, TN_DIMS, preferred_element_type=F32)
        s_all[c, :, lane0:lane0 + C_DK] = st[...].astype(BF16)
        st[...] = st[...] * jnp.exp(bl) + kv_t

    def state_body(i, carry):
        state_step(i, bf_s, c_len - 1, stf, 0)
        state_step(nc - 1 - i, bb_s, 0, stb, C_DK)
        return carry

    lax.fori_loop(0, nc, state_body, 0, unroll=2)

    ri = lax.broadcasted_iota(jnp.int32, (c_len, c_len), 0)
    ci = lax.broadcasted_iota(jnp.int32, (c_len, c_len), 1)

    def out_body(c, carry):
        q = q_ref[rows(c), :].astype(F32) * scale
        k = k_ref[rows(c), :].astype(F32)
        bcf = bf_s[rows(c), :]
        bcb = bb_s[rows(c), :]
        qf = (q * jnp.exp(bcf)).astype(BF16)
        kf = (k * jnp.exp(-bcf)).astype(BF16)
        qb = (q * jnp.exp(bcb)).astype(BF16)
        kb = (k * jnp.exp(-bcb)).astype(BF16)
        att_f = lax.dot_general(qf, kf, NT_DIMS, preferred_element_type=F32)
        att_b = lax.dot_general(qb, kb, NT_DIMS, preferred_element_type=F32)
        att = jnp.where(ri >= ci, att_f, att_b).astype(BF16)
        q_both = jnp.concatenate([qf, qb], axis=1)
        o = (jnp.dot(att, v_ref[rows(c), :], preferred_element_type=F32)
             + lax.dot_general(q_both, s_all[c], NT_DIMS, preferred_element_type=F32))
        g = g_ref[rows(c), :].astype(F32)
        o_ref[rows(c), :] = (_rms(o, gain_ref[...]) * (g * _sigmoid(g))).astype(BF16)
        return carry

    lax.fori_loop(0, nc, out_body, 0, unroll=4)


def _gla(proj3, lf3, lb3, out_norm):
    b, seq, _ = proj3.shape
    kw = C_HEADS * C_DK
    k_blk = kw // C_DK
    v_blk = 2 * kw // C_DV
    g_blk = v_blk + C_HEADS
    nc = seq // C_CHUNK
    return pl.pallas_call(
        functools.partial(_gla_kernel, seq=seq),
        grid=(b, C_HEADS),
        in_specs=[
            pl.BlockSpec((None, seq, C_DK), lambda i, h: (i, 0, h)),
            pl.BlockSpec((None, seq, C_DK), lambda i, h: (i, 0, k_blk + h)),
            pl.BlockSpec((None, seq, C_DV), lambda i, h: (i, 0, v_blk + h)),
            pl.BlockSpec((None, seq, C_DV), lambda i, h: (i, 0, g_blk + h)),
            pl.BlockSpec((None, seq, C_DK), lambda i, h: (i, 0, h)),
            pl.BlockSpec((None, seq, C_DK), lambda i, h: (i, 0, h)),
            pl.BlockSpec((1, C_DV), lambda i, h: (0, 0)),
        ],
        out_specs=pl.BlockSpec((None, seq, C_DV), lambda i, h: (i, 0, h)),
        out_shape=jax.ShapeDtypeStruct((b, seq, C_HEADS * C_DV), BF16),
        scratch_shapes=[
            pltpu.VMEM((seq, C_DK), F32),
            pltpu.VMEM((seq, C_DK), F32),
            pltpu.VMEM((nc, C_DV, 2 * C_DK), BF16),
            pltpu.VMEM((C_DV, C_DK), F32),
            pltpu.VMEM((C_DV, C_DK), F32),
        ],
        compiler_params=_cparams(("parallel", "parallel")),
        name="gla",
    )(proj3, proj3, proj3, proj3, lf3, lb3, out_norm.reshape(1, C_DV))


def _final_kernel(x_ref, y0_ref, y1_ref, gain_ref, o_ref):
    o_ref[...] = _rms(x_ref[...] + y0_ref[...] + y1_ref[...], gain_ref[...])


def _final_norm(n, x, y, gain):
    tm = ROW_TILE
    return pl.pallas_call(
        _final_kernel,
        grid=(n // tm,),
        in_specs=[_row_spec(tm, D_MODEL), _row_spec(tm, D_MODEL), _row_spec(tm, D_MODEL, n // tm),
                  _full_spec((1, D_MODEL))],
        out_specs=_row_spec(tm, D_MODEL),
        out_shape=jax.ShapeDtypeStruct((n, D_MODEL), F32),
        compiler_params=_cparams(("parallel",)),
        name="final_norm",
    )(x, y, y, gain.reshape(1, D_MODEL))


def _moe_layer(n, hn, ids_pad, wts_pad, w_gate, w_up, w_down, layer):
    plan = _moe_plan(ids_pad[:, :2], wts_pad[:, :2], n)
    return _moe_experts(n, hn, plan, w_gate, w_up, w_down, layer)


def kernel(x, t5_bias, norm_mix, norm_ffn, norm_final, ev_w_in, ev_lambda, ev_subln, ev_rpb, ev_w_out,
           od_w_in, od_w_gk_fwd, od_b_gk_fwd, od_w_gk_bwd, od_b_gk_bwd, od_out_norm, od_w_out,
           moe_w_grp, moe_b_grp, moe_w_exp, moe_b_exp, moe_w_gate, moe_w_up, moe_w_down):
    b, seq, d = x.shape
    n = b * seq
    rows = seq // GRID_W
    assert d == D_MODEL and n % ROW_TILE == 0 and seq % ATT_TQ == 0 and seq % C_CHUNK == 0
    assert rows % NA_QROWS == 0 and rows >= NA_KROWS and MOE_TILE <= ROW_TILE
    x2 = x.reshape(n, d)

    (proj,) = _in_proj_call(n, [(x2, 0)], norm_mix[0], ev_w_in[0].astype(BF16))
    proj3 = proj.reshape(b, seq, proj.shape[1])
    lam_init = 0.8 - 0.6 * math.exp(-0.3 * 0)
    lp = ev_lambda[0].astype(F32)
    lam = (jnp.exp(jnp.sum(lp[0] * lp[1])) - jnp.exp(jnp.sum(lp[2] * lp[3])) + lam_init).reshape(1)
    strip = _t5_strip(t5_bias, seq, ATT_TQ)
    a_diff = _diff_attn(proj3, lam, strip, ev_subln[0], 1.0 - lam_init)

    starts, classes, _ = _na_block_geometry(rows)
    table = _na_table(ev_rpb[0], rows)
    a_na = _na_attn(proj3, table, starts, classes)

    w_out = ev_w_out[0].astype(BF16)
    w_r, b_r = _router_params(moe_w_grp[0], moe_b_grp[0], moe_w_exp[0], moe_b_exp[0])
    x1, hn, ids, wts = _out_proj_router(
        n, x2, [a_diff.reshape(n, A_W), a_na.reshape(n, B_W)], [w_out[:A_W], w_out[A_W:]],
        norm_ffn[0], w_r, b_r)
    y = _moe_layer(n, hn, ids, wts, moe_w_gate, moe_w_up, moe_w_down, 0)

    kw = C_HEADS * C_DK
    main_w = 2 * kw + 2 * C_HEADS * C_DV
    w_in = od_w_in[0]
    w_gates = jnp.pad(w_in[:, main_w:], ((0, 0), (0, LANES - 2 * C_GATE_RANK))).astype(BF16)
    wf = jnp.pad(od_w_gk_fwd[0], ((0, LANES - C_GATE_RANK), (0, 0))).astype(BF16)
    wb = jnp.pad(od_w_gk_bwd[0], ((C_GATE_RANK, LANES - 2 * C_GATE_RANK), (0, 0))).astype(BF16)
    x2b, proj, lf, lb = _in_proj_call(
        n, [(x1, 0), (y, 0), (y, n)], norm_mix[1], w_in[:, :main_w].astype(BF16),
        gates=[w_gates, wf, od_b_gk_fwd[0].reshape(1, kw), wb, od_b_gk_bwd[0].reshape(1, kw)])
    a_gla = _gla(proj.reshape(b, seq, main_w), lf.reshape(b, seq, kw), lb.reshape(b, seq, kw), od_out_norm[0])

    w_r, b_r = _router_params(moe_w_grp[1], moe_b_grp[1], moe_w_exp[1], moe_b_exp[1])
    x3, hn, ids, wts = _out_proj_router(
        n, x2b, [a_gla.reshape(n, C_HEADS * C_DV)], [od_w_out[0].astype(BF16)], norm_ffn[1], w_r, b_r)
    y = _moe_layer(n, hn, ids, wts, moe_w_gate, moe_w_up, moe_w_down, 1)

    return _final_norm(n, x3, y, norm_final).reshape(b, seq, d)


def _na_table(rpb, rows):
    kh = min(NA_MAX_ROWS, rows)
    starts, classes, reps = _na_block_geometry(rows)
    qn, kn = NA_QROWS * GRID_W, NA_KROWS * GRID_W
    q_row, q_col = jnp.arange(qn) // GRID_W, jnp.arange(qn) % GRID_W
    k_row, k_col = jnp.arange(kn) // GRID_W, jnp.arange(kn) % GRID_W
    col_start = jnp.clip(q_col - NA_COLS // 2, 0, GRID_W - NA_COLS)
    col_ok = (k_col[None, :] >= col_start[:, None]) & (k_col[None, :] < col_start[:, None] + NA_COLS)
    side = GRID_W - NA_COLS
    col_bias = _toeplitz(jnp.pad(rpb.astype(F32), ((0, 0), (0, 0), (side, side))), GRID_W, GRID_W)
    per_class = []
    for cls in range(len(reps)):
        j = classes.index(cls)
        q_abs = j * NA_QROWS + q_row
        k_abs = starts[j] + k_row
        row_start = jnp.clip(q_abs - kh // 2, 0, rows - kh)
        row_ok = (k_abs[None, :] >= row_start[:, None]) & (k_abs[None, :] < row_start[:, None] + kh)
        dr = jnp.clip(starts[j] + jnp.arange(NA_KROWS)[None, :] - (j * NA_QROWS + jnp.arange(NA_QROWS))[:, None]
                      + NA_MAX_ROWS - 1, 0, 2 * NA_MAX_ROWS - 2)
        blocks = col_bias[:, dr]
        bias = jnp.transpose(blocks, (0, 1, 3, 2, 4)).reshape(B_HEADS, qn, kn)
        per_class.append(jnp.where((row_ok & col_ok)[None], bias, -jnp.inf))
    t = jnp.stack(per_class, axis=1)
    return jnp.transpose(t.reshape(B_HEADS // 2, 2, len(reps), qn, kn), (0, 2, 1, 3, 4))
```

```python
import functools
import math

import jax
import jax.numpy as jnp
from jax import lax
from jax.experimental import pallas as pl
from jax.experimental.pallas import tpu as pltpu

D_MODEL = 1024
GRID_W = 64
A_HEADS = 4
A_DH = 64
B_HEADS = 8
B_DH = 64
NA_MAX_ROWS = 8
NA_COLS = 16
T5_BUCKETS = 32
T5_MAX_DIST = 128
C_HEADS = 4
C_DK = 128
C_DV = 256
C_GATE_RANK = 16
C_GATE_NORM = 16.0
C_CHUNK = 64
N_GROUPS = 4
EXP_PER_GROUP = 8
N_EXPERTS = N_GROUPS * EXP_PER_GROUP
D_EXPERT = 512
EPS = 1e-6

A_W = A_HEADS * 2 * A_DH
B_W = B_HEADS * B_DH
LANES = 128
FEAT_ROWS = D_MODEL // LANES
VMEM_LIMIT = 56 * 1024 * 1024

ROW_TILE = 512
ATT_TQ = 256
NA_QROWS = 4
NA_KROWS = 12
MOE_TILE = 256

F32 = jnp.float32
BF16 = jnp.bfloat16
NT_DIMS = (((1,), (1,)), ((), ()))
TN_DIMS = (((0,), (0,)), ((), ()))


def _cparams(sem):
    return pltpu.CompilerParams(dimension_semantics=sem, vmem_limit_bytes=VMEM_LIMIT)


def _load_token_tiles(ref, tm, lead=()):
    return jnp.concatenate(
        [ref[lead + (pl.ds(j, tm, stride=FEAT_ROWS), slice(None))] for j in range(FEAT_ROWS)], axis=1)


def _store_token_tiles(ref, val, lead=()):
    tm = val.shape[0]
    for j in range(FEAT_ROWS):
        ref[lead + (pl.ds(j, tm, stride=FEAT_ROWS), slice(None))] = val[:, j * LANES:(j + 1) * LANES]


def _rms(x, gain):
    return x * lax.rsqrt(jnp.mean(x * x, axis=-1, keepdims=True) + EPS) * gain


def _sigmoid(x):
    return 1.0 / (1.0 + jnp.exp(-x))


def _log_sigmoid(z):
    return jnp.minimum(z, 0.0) - jnp.log1p(jnp.exp(-jnp.abs(z)))


def _in_proj_kernel(*refs, n_parts, gates):
    parts = refs[:n_parts]
    gain_ref, w_ref = refs[n_parts], refs[n_parts + 1]
    pos = n_parts + 2
    if gates:
        wg_ref, wf_ref, bf_ref, wb_ref, bb_ref = refs[pos:pos + 5]
        pos += 5
    outs = refs[pos:]
    x = parts[0][...]
    for p in parts[1:]:
        x = x + _load_token_tiles(p, x.shape[0])
    oi = 0
    if n_parts > 1:
        outs[0][...] = x
        oi = 1
    h = _rms(x, gain_ref[...]).astype(BF16)
    proj_ref = outs[oi]
    n_out = proj_ref.shape[1]
    step = 512
    for j in range(n_out // step):
        proj_ref[:, j * step:(j + 1) * step] = jnp.dot(
            h, w_ref[:, j * step:(j + 1) * step], preferred_element_type=F32).astype(BF16)
    if gates:
        lf_ref, lb_ref = outs[oi + 1], outs[oi + 2]
        g = jnp.dot(h, wg_ref[...], preferred_element_type=F32).astype(BF16)
        zf = jnp.dot(g, wf_ref[...], preferred_element_type=F32) + bf_ref[...]
        zb = jnp.dot(g, wb_ref[...], preferred_element_type=F32) + bb_ref[...]
        lf_ref[...] = _log_sigmoid(zf) / C_GATE_NORM
        lb_ref[...] = _log_sigmoid(zb) / C_GATE_NORM


def _row_spec(tm, width, block_off=0):
    return pl.BlockSpec((tm, width), lambda i, o=block_off: (i + o, 0))


def _full_spec(shape):
    nd = len(shape)
    return pl.BlockSpec(shape, lambda i, _nd=nd: (0,) * _nd)


def _in_proj_call(n, parts, gain, w, gates=None):
    tm = ROW_TILE
    n_parts = len(parts)
    n_out = w.shape[1]
    in_specs = [_row_spec(tm, D_MODEL)] + [_row_spec(tm * FEAT_ROWS, LANES, off // tm) for _, off in parts[1:]]
    args = [a for a, _ in parts]
    in_specs += [_full_spec((1, D_MODEL)), _full_spec(w.shape)]
    args += [gain.reshape(1, D_MODEL), w]
    out_shape, out_specs = [], []
    if n_parts > 1:
        out_shape.append(jax.ShapeDtypeStruct((n, D_MODEL), F32))
        out_specs.append(_row_spec(tm, D_MODEL))
    out_shape.append(jax.ShapeDtypeStruct((n, n_out), BF16))
    out_specs.append(_row_spec(tm, n_out))
    if gates is not None:
        for a in gates:
            in_specs.append(_full_spec(a.shape))
            args.append(a)
        kw = C_HEADS * C_DK
        out_shape += [jax.ShapeDtypeStruct((n, kw), F32)] * 2
        out_specs += [_row_spec(tm, kw)] * 2
    return pl.pallas_call(
        functools.partial(_in_proj_kernel, n_parts=n_parts, gates=gates is not None),
        grid=(n // tm,),
        in_specs=in_specs,
        out_specs=out_specs,
        out_shape=out_shape,
        compiler_params=_cparams(("parallel",)),
        name="in_proj_gla" if gates is not None else "in_proj_attn",
    )(*args)


def _diff_attn_kernel(lam_ref, q_ref, k_ref, v_ref, bias_ref, gain_ref, o_ref, *, seq, out_scale):
    tq = q_ref.shape[0]
    qb = pl.program_id(2)
    q = q_ref[...]
    k = k_ref[...]
    lane = lax.broadcasted_iota(jnp.int32, q.shape, 1)
    scale = A_DH ** -0.5
    zero = jnp.zeros_like(q)
    off = pl.multiple_of((seq - tq) - qb * tq, LANES)
    bias = bias_ref[:, pl.ds(off, seq)]

    v = v_ref[...]

    def map_output(qm):
        s = lax.dot_general(qm * scale, k, NT_DIMS, preferred_element_type=F32) + bias
        e = jnp.exp(s - jnp.max(s, axis=-1, keepdims=True))
        denom = jnp.sum(e, axis=-1, keepdims=True)
        return jnp.dot(e.astype(BF16), v, preferred_element_type=F32) / denom

    o = map_output(jnp.where(lane < A_DH, q, zero)) - lam_ref[0] * map_output(jnp.where(lane >= A_DH, q, zero))
    o_ref[...] = (_rms(o, gain_ref[...]) * out_scale).astype(BF16)


def _t5_bucket(rel):
    half = T5_BUCKETS // 2
    max_exact = half // 2
    sign_off = jnp.where(rel > 0, half, 0)
    n = jnp.abs(rel)
    nf = jnp.maximum(n, 1).astype(F32)
    large = max_exact + (jnp.log(nf / max_exact) / math.log(T5_MAX_DIST / max_exact)
                         * (half - max_exact)).astype(jnp.int32)
    large = jnp.minimum(large, half - 1)
    return sign_off + jnp.where(n < max_exact, n, large)


def _t5_strip(t5_bias, seq, tq):
    n_rel = 2 * seq - 1
    vec = t5_bias[_t5_bucket(jnp.arange(n_rel) - (seq - 1))].astype(F32).T
    return _toeplitz(vec, tq, 2 * seq - tq)


def _toeplitz(vec, n_rows, n_cols):
    n = vec.shape[-1]
    lead = vec.shape[:-1]
    assert n_cols + n_rows - 1 <= n and n_cols <= n - 1
    rolled = jnp.roll(vec, -(n_rows - 1), axis=-1)
    reps = -(-(n_rows * (n - 1)) // n)
    flat = jnp.broadcast_to(rolled[..., None, :], lead + (reps, n)).reshape(lead + (reps * n,))
    return flat[..., :n_rows * (n - 1)].reshape(lead + (n_rows, n - 1))[..., :n_cols]


def _diff_attn(proj3, lam, strip, subln, out_scale):
    b, seq, _ = proj3.shape
    tq = ATT_TQ
    kblk, vblk = A_W // LANES, 2 * A_W // LANES
    return pl.pallas_call(
        functools.partial(_diff_attn_kernel, seq=seq, out_scale=out_scale),
        grid=(A_HEADS, b, seq // tq),
        in_specs=[
            pl.BlockSpec(memory_space=pltpu.SMEM),
            pl.BlockSpec((None, tq, LANES), lambda h, i, j: (i, j, h)),
            pl.BlockSpec((None, seq, LANES), lambda h, i, j: (i, 0, kblk + h)),
            pl.BlockSpec((None, seq, LANES), lambda h, i, j: (i, 0, vblk + h)),
            pl.BlockSpec((None, tq, 2 * seq - tq), lambda h, i, j: (h, 0, 0)),
            pl.BlockSpec((1, LANES), lambda h, i, j: (0, 0)),
        ],
        out_specs=pl.BlockSpec((None, tq, LANES), lambda h, i, j: (i, j, h)),
        out_shape=jax.ShapeDtypeStruct((b, seq, A_W), BF16),
        compiler_params=_cparams(("parallel", "parallel", "parallel")),
        name="diff_attn",
    )(lam, proj3, proj3, proj3, strip, subln.reshape(1, LANES))


def _na_block_geometry(rows):
    nblk = rows // NA_QROWS
    kh = min(NA_MAX_ROWS, rows)
    starts, classes, reps = [], [], []
    for j in range(nblk):
        ks = min(max(j * NA_QROWS - kh // 2, 0), rows - NA_KROWS)
        rel = (j * NA_QROWS - ks,) + tuple(
            min(max(r - kh // 2, 0), rows - kh) - ks for r in range(j * NA_QROWS, (j + 1) * NA_QROWS))
        starts.append(ks)
        if rel not in reps:
            reps.append(rel)
        classes.append(reps.index(rel))
    return starts, classes, reps


def _na_kernel(q_ref, k_ref, v_ref, tab_ref, o_ref, *, starts, classes):
    qn, kn = NA_QROWS * GRID_W, NA_KROWS * GRID_W
    scale = B_DH ** -0.5
    lane = lax.broadcasted_iota(jnp.int32, (qn, LANES), 1)
    for j, (ks, cls) in enumerate(zip(starts, classes)):
        q = q_ref[j * qn:(j + 1) * qn, :]
        kw = k_ref[ks * GRID_W:ks * GRID_W + kn, :]
        vw = v_ref[ks * GRID_W:ks * GRID_W + kn, :]
        zero = jnp.zeros_like(q)
        outs = []
        for hl in range(2):
            in_head = (lane >= hl * B_DH) & (lane < (hl + 1) * B_DH)
            qm = jnp.where(in_head, q, zero) * scale
            s = lax.dot_general(qm, kw, NT_DIMS, preferred_element_type=F32) + tab_ref[cls, hl]
            m = jnp.max(s, axis=-1, keepdims=True)
            p = jnp.exp(s - m)
            denom = jnp.sum(p, axis=-1, keepdims=True)
            outs.append(jnp.dot(p.astype(BF16), vw, preferred_element_type=F32) / denom)
        o_ref[j * qn:(j + 1) * qn, :] = jnp.where(lane < B_DH, outs[0], outs[1]).astype(BF16)


def _na_attn(proj3, table, starts, classes):
    b, seq, _ = proj3.shape
    qblk = 3 * A_W // LANES
    kblk = qblk + B_W // LANES
    vblk = kblk + B_W // LANES
    ncls = table.shape[1]
    qn, kn = NA_QROWS * GRID_W, NA_KROWS * GRID_W
    return pl.pallas_call(
        functools.partial(_na_kernel, starts=tuple(starts), classes=tuple(classes)),
        grid=(B_HEADS // 2, b),
        in_specs=[
            pl.BlockSpec((None, seq, LANES), lambda h, i: (i, 0, qblk + h)),
            pl.BlockSpec((None, seq, LANES), lambda h, i: (i, 0, kblk + h)),
            pl.BlockSpec((None, seq, LANES), lambda h, i: (i, 0, vblk + h)),
            pl.BlockSpec((None, ncls, 2, qn, kn), lambda h, i: (h, 0, 0, 0, 0)),
        ],
        out_specs=pl.BlockSpec((None, seq, LANES), lambda h, i: (i, 0, h)),
        out_shape=jax.ShapeDtypeStruct((b, seq, B_W), BF16),
        compiler_params=_cparams(("parallel", "parallel")),
        name="na_attn",
    )(proj3, proj3, proj3, table)


def _out_proj_router_kernel(*refs, n_a):
    x_ref = refs[0]
    a_refs = refs[1:1 + n_a]
    w_refs = refs[1 + n_a:1 + 2 * n_a]
    gain_ref, wr_ref, br_ref, xo_ref, hn_ref, ids_ref, wts_ref = refs[1 + 2 * n_a:]
    acc = x_ref[...]
    for a, w in zip(a_refs, w_refs):
        acc = acc + jnp.dot(a[...], w[...], preferred_element_type=F32)
    xo_ref[...] = acc
    h = _rms(acc, gain_ref[...])
    _store_token_tiles(hn_ref, h)
    h_hi = h.astype(BF16)
    h_lo = (h - h_hi.astype(F32)).astype(BF16)
    hi = jnp.dot(h_hi, wr_ref[...], preferred_element_type=F32)
    logits = (hi[:, :LANES] + hi[:, LANES:]
              + jnp.dot(h_lo, wr_ref[:, :LANES], preferred_element_type=F32) + br_ref[...])
    lane = lax.broadcasted_iota(jnp.int32, logits.shape, 1)
    neg = jnp.float32(-jnp.inf)
    big = jnp.int32(LANES)

    def masked_softmax(mask):
        z = jnp.where(mask, logits, neg)
        e = jnp.exp(z - jnp.max(z, axis=-1, keepdims=True))
        return e / jnp.sum(e, axis=-1, keepdims=True)

    def top1(p, mask):
        w = jnp.max(jnp.where(mask, p, -1.0), axis=-1, keepdims=True)
        idx = jnp.min(jnp.where(mask & (p == w), lane, big), axis=-1, keepdims=True)
        return w, idx

    is_grp = lane < N_GROUPS
    g_w, g_idx = top1(masked_softmax(is_grp), is_grp)
    e_lane = lane - N_GROUPS
    in_grp = (e_lane >= g_idx * EXP_PER_GROUP) & (e_lane < (g_idx + 1) * EXP_PER_GROUP)
    p_e = masked_softmax(in_grp)
    w1, i1 = top1(p_e, in_grp)
    rest = in_grp & (lane != i1)
    w2, i2 = top1(p_e, rest)
    denom = w1 + w2
    ids_ref[...] = jnp.where(lane == 0, i1 - N_GROUPS, i2 - N_GROUPS)
    wts_ref[...] = jnp.where(lane == 0, g_w * (w1 / denom), g_w * (w2 / denom))


def _out_proj_router(n, x, acts, ws, gain, w_router, b_router):
    tm = ROW_TILE
    n_a = len(acts)
    in_specs = [_row_spec(tm, D_MODEL)]
    in_specs += [_row_spec(tm, a.shape[1]) for a in acts]
    in_specs += [_full_spec(w.shape) for w in ws]
    in_specs += [_full_spec((1, D_MODEL)), _full_spec(w_router.shape), _full_spec((1, LANES))]
    return pl.pallas_call(
        functools.partial(_out_proj_router_kernel, n_a=n_a),
        grid=(n // tm,),
        in_specs=in_specs,
        out_specs=[_row_spec(tm, D_MODEL), _row_spec(tm * FEAT_ROWS, LANES), _row_spec(tm, LANES),
                   _row_spec(tm, LANES)],
        out_shape=[jax.ShapeDtypeStruct((n, D_MODEL), F32), jax.ShapeDtypeStruct((n * FEAT_ROWS, LANES), F32),
                   jax.ShapeDtypeStruct((n, LANES), jnp.int32), jax.ShapeDtypeStruct((n, LANES), F32)],
        compiler_params=_cparams(("parallel",)),
        name="out_proj_router",
    )(x, *acts, *ws, gain.reshape(1, D_MODEL), w_router, b_router)


def _router_params(w_grp, b_grp, w_exp, b_exp):
    pad = LANES - N_GROUPS - N_EXPERTS
    w = jnp.concatenate([w_grp, w_exp, jnp.zeros((D_MODEL, pad), F32)], axis=1)
    b = jnp.concatenate([b_grp, b_exp, jnp.zeros((pad,), F32)]).reshape(1, LANES)
    w_hi = w.astype(BF16)
    w_lo = (w - w_hi.astype(F32)).astype(BF16)
    return jnp.concatenate([w_hi, w_lo], axis=1), b


def _moe_plan(ids, wts, n):
    tm = MOE_TILE
    n_assign = 2 * n
    p_rows = n_assign + N_EXPERTS * tm
    e_flat = ids.T.reshape(-1)
    w_flat = wts.T.reshape(-1)
    idx_bits = max(n_assign - 1, 1).bit_length()
    assert idx_bits + N_EXPERTS.bit_length() <= 31
    key = jnp.sort((e_flat << idx_bits) | jnp.arange(n_assign, dtype=jnp.int32))
    order = key & ((1 << idx_bits) - 1)
    w_sorted = w_flat[order]
    experts = jnp.arange(N_EXPERTS, dtype=jnp.int32)
    counts = jnp.sum((key >> idx_bits)[None, :] == experts[:, None], axis=1, dtype=jnp.int32)
    off = jnp.cumsum(counts) - counts
    padded = (counts + tm - 1) // tm * tm
    p_end = jnp.cumsum(padded)
    p_off = p_end - padded
    nt = p_rows // tm
    tile_start = jnp.arange(nt, dtype=jnp.int32) * tm
    tile_expert = jnp.minimum(jnp.sum(p_end[None, :] <= tile_start[:, None], axis=1, dtype=jnp.int32),
                              N_EXPERTS - 1)
    tile_count = jnp.clip(counts[tile_expert] - (tile_start - p_off[tile_expert]), 0, tm).astype(jnp.int32)
    n_used = (p_end[-1] // tm).astype(jnp.int32).reshape(1)
    in_tile = jnp.arange(tm, dtype=jnp.int32)[None, :]
    real = in_tile < tile_count[:, None]
    s_idx = jnp.clip((off[tile_expert] + tile_start - p_off[tile_expert])[:, None] + in_tile, 0, n_assign - 1)
    src = jnp.where(real, order[s_idx] % n, 0).reshape(p_rows)
    dest = jnp.where(real, order[s_idx], 0).reshape(p_rows)
    w_row = jnp.where(real, w_sorted[s_idx], 0.0).reshape(p_rows)
    return (src.reshape(nt, 1, tm), dest.reshape(nt, 1, tm), w_row.reshape(p_rows, 1), tile_expert,
            tile_count, n_used)


def _moe_kernel(te_ref, tc_ref, nu_ref, src_ref, dst_ref, wrow_ref, x_hbm, wg_ref, wu_ref, wd_ref, y_hbm,
                xbuf, ybuf, wgb, wub, wdb, gsem, ssem):
    tm = MOE_TILE
    i = pl.program_id(0)
    n_used = nu_ref[0]
    last = pl.num_programs(0) - 1

    def tile_rows(r):
        return pl.ds(pl.multiple_of(r * FEAT_ROWS, FEAT_ROWS), FEAT_ROWS)

    def gather_copy(slot, r, tok):
        return pltpu.make_async_copy(x_hbm.at[tile_rows(tok)], xbuf.at[slot, tile_rows(r)], gsem.at[slot])

    def scatter_copy(slot, r, row):
        return pltpu.make_async_copy(ybuf.at[slot, tile_rows(r)], y_hbm.at[tile_rows(row)], ssem.at[slot])

    def start_all(make):
        def body(r2, c):
            make(2 * r2).start(priority=0)
            make(2 * r2 + 1).start(priority=1)
            return c
        lax.fori_loop(0, tm // 2, body, 0, unroll=4)

    def start_rows(count, make):
        @pl.when(count == tm)
        def _():
            start_all(make)

        @pl.when(count < tm)
        def _():
            lax.fori_loop(0, count, lambda r, c: (make(r).start(), c)[1], 0)

    def wait_scatter(t):
        slot = t % 2
        count = tc_ref[t]

        @pl.when(count == tm)
        def _():
            pltpu.make_async_copy(ybuf.at[slot], y_hbm.at[pl.ds(0, tm * FEAT_ROWS)], ssem.at[slot]).wait()

        @pl.when(count < tm)
        def _():
            lax.fori_loop(0, count, lambda r, c: (scatter_copy(slot, r, 0).wait(), c)[1], 0)

    @pl.when(i < n_used)
    def _():
        slot = i % 2
        start_all(lambda r: gather_copy(slot, r, src_ref[0, r]))

    @pl.when((i >= 1) & (i <= n_used))
    def _():
        t = i - 1
        slot = t % 2
        pltpu.make_async_copy(x_hbm.at[pl.ds(0, tm * FEAT_ROWS)], xbuf.at[slot], gsem.at[slot]).wait()

        @pl.when(t >= 2)
        def _():
            wait_scatter(t - 2)

        new_expert = jnp.logical_or(t == 0, te_ref[t] != te_ref[jnp.maximum(t - 1, 0)])

        @pl.when(new_expert)
        def _():
            wgb[...] = wg_ref[...].astype(BF16)
            wub[...] = wu_ref[...].astype(BF16)
            wdb[...] = wd_ref[...].astype(BF16)

        x = _load_token_tiles(xbuf, tm, lead=(slot,)).astype(BF16)
        g = jnp.dot(x, wgb[...], preferred_element_type=F32)
        u = jnp.dot(x, wub[...], preferred_element_type=F32)
        hid = (g * _sigmoid(g) * u).astype(BF16)
        y = jnp.dot(hid, wdb[...], preferred_element_type=F32)
        _store_token_tiles(ybuf, y * wrow_ref[...], lead=(slot,))

        start_rows(tc_ref[t], lambda r: scatter_copy(slot, r, dst_ref[0, r]))

    @pl.when(i == last)
    def _():
        wait_scatter(n_used - 1)

        @pl.when(n_used >= 2)
        def _():
            wait_scatter(n_used - 2)


def _moe_experts(n, hn, plan, w_gate, w_up, w_down, layer):
    tm = MOE_TILE
    src, dest, w_row, tile_expert, tile_count, n_used = plan
    nt = src.shape[0]
    prev = lambda i: jnp.maximum(i - 1, 0)
    grid_spec = pltpu.PrefetchScalarGridSpec(
        num_scalar_prefetch=3,
        grid=(nt + 1,),
        in_specs=[
            pl.BlockSpec((None, 1, tm), lambda i, te, tc, nu: (jnp.minimum(i, nt - 1), 0, 0), memory_space=pltpu.SMEM),
            pl.BlockSpec((None, 1, tm), lambda i, te, tc, nu: (prev(i), 0, 0), memory_space=pltpu.SMEM),
            pl.BlockSpec((tm, 1), lambda i, te, tc, nu: (prev(i), 0)),
            pl.BlockSpec(memory_space=pl.ANY),
            pl.BlockSpec((None, None, D_MODEL, D_EXPERT), lambda i, te, tc, nu: (layer, te[prev(i)], 0, 0)),
            pl.BlockSpec((None, None, D_MODEL, D_EXPERT), lambda i, te, tc, nu: (layer, te[prev(i)], 0, 0)),
            pl.BlockSpec((None, None, D_EXPERT, D_MODEL), lambda i, te, tc, nu: (layer, te[prev(i)], 0, 0)),
        ],
        out_specs=pl.BlockSpec(memory_space=pl.ANY),
        scratch_shapes=[
            pltpu.VMEM((2, tm * FEAT_ROWS, LANES), F32),
            pltpu.VMEM((2, tm * FEAT_ROWS, LANES), F32),
            pltpu.VMEM((D_MODEL, D_EXPERT), BF16),
            pltpu.VMEM((D_MODEL, D_EXPERT), BF16),
            pltpu.VMEM((D_EXPERT, D_MODEL), BF16),
            pltpu.SemaphoreType.DMA((2,)),
            pltpu.SemaphoreType.DMA((2,)),
        ],
    )
    return pl.pallas_call(
        _moe_kernel,
        grid_spec=grid_spec,
        out_shape=jax.ShapeDtypeStruct((2 * n * FEAT_ROWS, LANES), F32),
        compiler_params=_cparams(("arbitrary",)),
        name="moe_experts",
    )(tile_expert, tile_count, n_used, src, dest, w_row, hn, w_gate, w_up, w_down)


def _gla_kernel(q_ref, k_ref, v_ref, g_ref, lf_ref, lb_ref, gain_ref, o_ref,
                bf_s, bb_s, s_all, stf, stb, *, seq):
    c_len = C_CHUNK
    nc = seq // c_len
    scale = C_DK ** -0.5
    row = lax.broadcasted_iota(jnp.int32, (c_len, C_DK), 0)
    shifts = [1 << s for s in range(int(math.log2(c_len)))]

    def rows(c):
        return pl.ds(pl.multiple_of(c * c_len, c_len), c_len)

    def cum_body(c, carry):
        y = lf_ref[rows(c), :]
        for s in shifts:
            y = y + jnp.where(row >= s, pltpu.roll(y, s, 0), 0.0)
        bf_s[rows(c), :] = y
        y = lb_ref[rows(c), :]
        for s in shifts:
            y = y + jnp.where(row < c_len - s, pltpu.roll(y, c_len - s, 0), 0.0)
        bb_s[rows(c), :] = y
        return carry

    lax.fori_loop(0, nc, cum_body, 0)

    stf[...] = jnp.zeros_like(stf)
    stb[...] = jnp.zeros_like(stb)

    def state_step(c, b_s, last_row, st, lane0):
        k = k_ref[rows(c), :].astype(F32)
        bc = b_s[rows(c), :]
        bl = bc[last_row:last_row + 1, :]
        kd = (k * jnp.exp(bl - bc)).astype(BF16)
        kv_t = lax.dot_general(v_ref[rows(c), :], kd, TN_DIMS, preferred_element_type=F32)
        s_all[c, :, lane0:lane0 + C_DK] = st[...].astype(BF16)
        st[...] = st[...] * jnp.exp(bl) + kv_t

    def state_body(i, carry):
        state_step(i, bf_s, c_len - 1, stf, 0)
        state_step(nc - 1 - i, bb_s, 0, stb, C_DK)
        return carry

    lax.fori_loop(0, nc, state_body, 0, unroll=2)

    ri = lax.broadcasted_iota(jnp.int32, (c_len, c_len), 0)
    ci = lax.broadcasted_iota(jnp.int32, (c_len, c_len), 1)

    def out_body(c, carry):
        q = q_ref[rows(c), :].astype(F32) * scale
        k = k_ref[rows(c), :].astype(F32)
        bcf = bf_s[rows(c), :]
        bcb = bb_s[rows(c), :]
        qf = (q * jnp.exp(bcf)).astype(BF16)
        kf = (k * jnp.exp(-bcf)).astype(BF16)
        qb = (q * jnp.exp(bcb)).astype(BF16)
        kb = (k * jnp.exp(-bcb)).astype(BF16)
        att_f = lax.dot_general(qf, kf, NT_DIMS, preferred_element_type=F32)
        att_b = lax.dot_general(qb, kb, NT_DIMS, preferred_element_type=F32)
        att = jnp.where(ri >= ci, att_f, att_b).astype(BF16)
        q_both = jnp.concatenate([qf, qb], axis=1)
        o = (jnp.dot(att, v_ref[rows(c), :], preferred_element_type=F32)
             + lax.dot_general(q_both, s_all[c], NT_DIMS, preferred_element_type=F32))
        g = g_ref[rows(c), :].astype(F32)
        o_ref[rows(c), :] = (_rms(o, gain_ref[...]) * (g * _sigmoid(g))).astype(BF16)
        return carry

    lax.fori_loop(0, nc, out_body, 0, unroll=4)


def _gla(proj3, lf3, lb3, out_norm):
    b, seq, _ = proj3.shape
    kw = C_HEADS * C_DK
    k_blk = kw // C_DK
    v_blk = 2 * kw // C_DV
    g_blk = v_blk + C_HEADS
    nc = seq // C_CHUNK
    return pl.pallas_call(
        functools.partial(_gla_kernel, seq=seq),
        grid=(b, C_HEADS),
        in_specs=[
            pl.BlockSpec((None, seq, C_DK), lambda i, h: (i, 0, h)),
            pl.BlockSpec((None, seq, C_DK), lambda i, h: (i, 0, k_blk + h)),
            pl.BlockSpec((None, seq, C_DV), lambda i, h: (i, 0, v_blk + h)),
            pl.BlockSpec((None, seq, C_DV), lambda i, h: (i, 0, g_blk + h)),
            pl.BlockSpec((None, seq, C_DK), lambda i, h: (i, 0, h)),
            pl.BlockSpec((None, seq, C_DK), lambda i, h: (i, 0, h)),
            pl.BlockSpec((1, C_DV), lambda i, h: (0, 0)),
        ],
        out_specs=pl.BlockSpec((None, seq, C_DV), lambda i, h: (i, 0, h)),
        out_shape=jax.ShapeDtypeStruct((b, seq, C_HEADS * C_DV), BF16),
        scratch_shapes=[
            pltpu.VMEM((seq, C_DK), F32),
            pltpu.VMEM((seq, C_DK), F32),
            pltpu.VMEM((nc, C_DV, 2 * C_DK), BF16),
            pltpu.VMEM((C_DV, C_DK), F32),
            pltpu.VMEM((C_DV, C_DK), F32),
        ],
        compiler_params=_cparams(("parallel", "parallel")),
        name="gla",
    )(proj3, proj3, proj3, proj3, lf3, lb3, out_norm.reshape(1, C_DV))


def _final_kernel(x_ref, y0_ref, y1_ref, gain_ref, o_ref):
    tm = x_ref.shape[0]
    x = x_ref[...] + _load_token_tiles(y0_ref, tm) + _load_token_tiles(y1_ref, tm)
    o_ref[...] = _rms(x, gain_ref[...])


def _final_norm(n, x, y, gain):
    tm = ROW_TILE
    return pl.pallas_call(
        _final_kernel,
        grid=(n // tm,),
        in_specs=[_row_spec(tm, D_MODEL), _row_spec(tm * FEAT_ROWS, LANES),
                  _row_spec(tm * FEAT_ROWS, LANES, n // tm), _full_spec((1, D_MODEL))],
        out_specs=_row_spec(tm, D_MODEL),
        out_shape=jax.ShapeDtypeStruct((n, D_MODEL), F32),
        compiler_params=_cparams(("parallel",)),
        name="final_norm",
    )(x, y, y, gain.reshape(1, D_MODEL))


def _moe_layer(n, hn, ids_pad, wts_pad, w_gate, w_up, w_down, layer):
    plan = _moe_plan(ids_pad[:, :2], wts_pad[:, :2], n)
    return _moe_experts(n, hn, plan, w_gate, w_up, w_down, layer)


def kernel(x, t5_bias, norm_mix, norm_ffn, norm_final, ev_w_in, ev_lambda, ev_subln, ev_rpb, ev_w_out,
           od_w_in, od_w_gk_fwd, od_b_gk_fwd, od_w_gk_bwd, od_b_gk_bwd, od_out_norm, od_w_out,
           moe_w_grp, moe_b_grp, moe_w_exp, moe_b_exp, moe_w_gate, moe_w_up, moe_w_down):
    b, seq, d = x.shape
    n = b * seq
    rows = seq // GRID_W
    assert d == D_MODEL and n % ROW_TILE == 0 and seq % ATT_TQ == 0 and seq % C_CHUNK == 0
    assert rows % NA_QROWS == 0 and rows >= NA_KROWS and MOE_TILE <= ROW_TILE
    x2 = x.reshape(n, d)

    (proj,) = _in_proj_call(n, [(x2, 0)], norm_mix[0], ev_w_in[0].astype(BF16))
    proj3 = proj.reshape(b, seq, proj.shape[1])
    lam_init = 0.8 - 0.6 * math.exp(-0.3 * 0)
    lp = ev_lambda[0].astype(F32)
    lam = (jnp.exp(jnp.sum(lp[0] * lp[1])) - jnp.exp(jnp.sum(lp[2] * lp[3])) + lam_init).reshape(1)
    strip = _t5_strip(t5_bias, seq, ATT_TQ)
    a_diff = _diff_attn(proj3, lam, strip, ev_subln[0], 1.0 - lam_init)

    starts, classes, _ = _na_block_geometry(rows)
    table = _na_table(ev_rpb[0], rows)
    a_na = _na_attn(proj3, table, starts, classes)

    w_out = ev_w_out[0].astype(BF16)
    w_r, b_r = _router_params(moe_w_grp[0], moe_b_grp[0], moe_w_exp[0], moe_b_exp[0])
    x1, hn, ids, wts = _out_proj_router(
        n, x2, [a_diff.reshape(n, A_W), a_na.reshape(n, B_W)], [w_out[:A_W], w_out[A_W:]],
        norm_ffn[0], w_r, b_r)
    y = _moe_layer(n, hn, ids, wts, moe_w_gate, moe_w_up, moe_w_down, 0)

    kw = C_HEADS * C_DK
    main_w = 2 * kw + 2 * C_HEADS * C_DV
    w_in = od_w_in[0]
    w_gates = jnp.pad(w_in[:, main_w:], ((0, 0), (0, LANES - 2 * C_GATE_RANK))).astype(BF16)
    wf = jnp.pad(od_w_gk_fwd[0], ((0, LANES - C_GATE_RANK), (0, 0))).astype(BF16)
    wb = jnp.pad(od_w_gk_bwd[0], ((C_GATE_RANK, LANES - 2 * C_GATE_RANK), (0, 0))).astype(BF16)
    x2b, proj, lf, lb = _in_proj_call(
        n, [(x1, 0), (y, 0), (y, n)], norm_mix[1], w_in[:, :main_w].astype(BF16),
        gates=[w_gates, wf, od_b_gk_fwd[0].reshape(1, kw), wb, od_b_gk_bwd[0].reshape(1, kw)])
    a_gla = _gla(proj.reshape(b, seq, main_w), lf.reshape(b, seq, kw), lb.reshape(b, seq, kw), od_out_norm[0])

    w_r, b_r = _router_params(moe_w_grp[1], moe_b_grp[1], moe_w_exp[1], moe_b_exp[1])
    x3, hn, ids, wts = _out_proj_router(
        n, x2b, [a_gla.reshape(n, C_HEADS * C_DV)], [od_w_out[0].astype(BF16)], norm_ffn[1], w_r, b_r)
    y = _moe_layer(n, hn, ids, wts, moe_w_gate, moe_w_up, moe_w_down, 1)

    return _final_norm(n, x3, y, norm_final).reshape(b, seq, d)


def _na_table(rpb, rows):
    kh = min(NA_MAX_ROWS, rows)
    starts, classes, reps = _na_block_geometry(rows)
    qn, kn = NA_QROWS * GRID_W, NA_KROWS * GRID_W
    q_row, q_col = jnp.arange(qn) // GRID_W, jnp.arange(qn) % GRID_W
    k_row, k_col = jnp.arange(kn) // GRID_W, jnp.arange(kn) % GRID_W
    col_start = jnp.clip(q_col - NA_COLS // 2, 0, GRID_W - NA_COLS)
    col_ok = (k_col[None, :] >= col_start[:, None]) & (k_col[None, :] < col_start[:, None] + NA_COLS)
    side = GRID_W - NA_COLS
    col_bias = _toeplitz(jnp.pad(rpb.astype(F32), ((0, 0), (0, 0), (side, side))), GRID_W, GRID_W)
    per_class = []
    for cls in range(len(reps)):
        j = classes.index(cls)
        q_abs = j * NA_QROWS + q_row
        k_abs = starts[j] + k_row
        row_start = jnp.clip(q_abs - kh // 2, 0, rows - kh)
        row_ok = (k_abs[None, :] >= row_start[:, None]) & (k_abs[None, :] < row_start[:, None] + kh)
        dr = jnp.clip(starts[j] + jnp.arange(NA_KROWS)[None, :] - (j * NA_QROWS + jnp.arange(NA_QROWS))[:, None]
                      + NA_MAX_ROWS - 1, 0, 2 * NA_MAX_ROWS - 2)
        blocks = col_bias[:, dr]
        bias = jnp.transpose(blocks, (0, 1, 3, 2, 4)).reshape(B_HEADS, qn, kn)
        per_class.append(jnp.where((row_ok & col_ok)[None], bias, -jnp.inf))
    t = jnp.stack(per_class, axis=1)
    return jnp.transpose(t.reshape(B_HEADS // 2, 2, len(reps), qn, kn), (0, 2, 1, 3, 4))
```

```python
import functools
import math

import jax
import jax.numpy as jnp
from jax import lax
from jax.experimental import pallas as pl
from jax.experimental.pallas import tpu as pltpu

D_MODEL = 1024
GRID_W = 64
A_HEADS = 4
A_DH = 64
B_HEADS = 8
B_DH = 64
NA_MAX_ROWS = 8
NA_COLS = 16
T5_BUCKETS = 32
T5_MAX_DIST = 128
C_HEADS = 4
C_DK = 128
C_DV = 256
C_GATE_RANK = 16
C_GATE_NORM = 16.0
C_CHUNK = 64
N_GROUPS = 4
EXP_PER_GROUP = 8
N_EXPERTS = N_GROUPS * EXP_PER_GROUP
D_EXPERT = 512
EPS = 1e-6

A_W = A_HEADS * 2 * A_DH
B_W = B_HEADS * B_DH
LANES = 128
FEAT_ROWS = D_MODEL // LANES
VMEM_LIMIT = 56 * 1024 * 1024

ROW_TILE = 512
ATT_TQ = 256
NA_QROWS = 4
NA_KROWS = 12
MOE_TILE = 256
MOE_COLS = 256

F32 = jnp.float32
BF16 = jnp.bfloat16
NT_DIMS = (((1,), (1,)), ((), ()))
TN_DIMS = (((0,), (0,)), ((), ()))


def _cparams(sem):
    return pltpu.CompilerParams(dimension_semantics=sem, vmem_limit_bytes=VMEM_LIMIT)


def _load_token_tiles(ref, tm, lead=()):
    return jnp.concatenate(
        [ref[lead + (pl.ds(j, tm, stride=FEAT_ROWS), slice(None))] for j in range(FEAT_ROWS)], axis=1)


def _store_token_tiles(ref, val, lead=()):
    tm = val.shape[0]
    for j in range(FEAT_ROWS):
        ref[lead + (pl.ds(j, tm, stride=FEAT_ROWS), slice(None))] = val[:, j * LANES:(j + 1) * LANES]


def _rms(x, gain):
    return x * lax.rsqrt(jnp.mean(x * x, axis=-1, keepdims=True) + EPS) * gain


def _sigmoid(x):
    return 1.0 / (1.0 + jnp.exp(-x))


def _log_sigmoid(z):
    return jnp.minimum(z, 0.0) - jnp.log1p(jnp.exp(-jnp.abs(z)))


def _in_proj_kernel(*refs, n_parts, gates):
    parts = refs[:n_parts]
    gain_ref, w_ref = refs[n_parts], refs[n_parts + 1]
    pos = n_parts + 2
    if gates:
        wg_ref, wf_ref, bf_ref, wb_ref, bb_ref = refs[pos:pos + 5]
        pos += 5
    outs = refs[pos:]
    x = parts[0][...]
    for p in parts[1:]:
        x = x + _load_token_tiles(p, x.shape[0])
    oi = 0
    if n_parts > 1:
        outs[0][...] = x
        oi = 1
    h = _rms(x, gain_ref[...]).astype(BF16)
    proj_ref = outs[oi]
    n_out = proj_ref.shape[1]
    step = 512
    for j in range(n_out // step):
        proj_ref[:, j * step:(j + 1) * step] = jnp.dot(
            h, w_ref[:, j * step:(j + 1) * step], preferred_element_type=F32).astype(BF16)
    if gates:
        lf_ref, lb_ref = outs[oi + 1], outs[oi + 2]
        g = jnp.dot(h, wg_ref[...], preferred_element_type=F32).astype(BF16)
        zf = jnp.dot(g, wf_ref[...], preferred_element_type=F32) + bf_ref[...]
        zb = jnp.dot(g, wb_ref[...], preferred_element_type=F32) + bb_ref[...]
        lf_ref[...] = _log_sigmoid(zf) / C_GATE_NORM
        lb_ref[...] = _log_sigmoid(zb) / C_GATE_NORM


def _row_spec(tm, width, block_off=0):
    return pl.BlockSpec((tm, width), lambda i, o=block_off: (i + o, 0))


def _full_spec(shape):
    nd = len(shape)
    return pl.BlockSpec(shape, lambda i, _nd=nd: (0,) * _nd)


def _in_proj_call(n, parts, gain, w, gates=None):
    tm = ROW_TILE
    n_parts = len(parts)
    n_out = w.shape[1]
    in_specs = [_row_spec(tm, D_MODEL)] + [_row_spec(tm * FEAT_ROWS, LANES, off // tm) for _, off in parts[1:]]
    args = [a for a, _ in parts]
    in_specs += [_full_spec((1, D_MODEL)), _full_spec(w.shape)]
    args += [gain.reshape(1, D_MODEL), w]
    out_shape, out_specs = [], []
    if n_parts > 1:
        out_shape.append(jax.ShapeDtypeStruct((n, D_MODEL), F32))
        out_specs.append(_row_spec(tm, D_MODEL))
    out_shape.append(jax.ShapeDtypeStruct((n, n_out), BF16))
    out_specs.append(_row_spec(tm, n_out))
    if gates is not None:
        for a in gates:
            in_specs.append(_full_spec(a.shape))
            args.append(a)
        kw = C_HEADS * C_DK
        out_shape += [jax.ShapeDtypeStruct((n, kw), F32)] * 2
        out_specs += [_row_spec(tm, kw)] * 2
    return pl.pallas_call(
        functools.partial(_in_proj_kernel, n_parts=n_parts, gates=gates is not None),
        grid=(n // tm,),
        in_specs=in_specs,
        out_specs=out_specs,
        out_shape=out_shape,
        compiler_params=_cparams(("parallel",)),
        name="in_proj_gla" if gates is not None else "in_proj_attn",
    )(*args)


def _diff_attn_kernel(lam_ref, q_ref, k_ref, v_ref, bias_ref, gain_ref, o_ref, *, seq, out_scale):
    tq = q_ref.shape[0]
    qb = pl.program_id(2)
    q = q_ref[...]
    k = k_ref[...]
    lane = lax.broadcasted_iota(jnp.int32, q.shape, 1)
    scale = A_DH ** -0.5
    zero = jnp.zeros_like(q)
    off = pl.multiple_of((seq - tq) - qb * tq, LANES)
    bias = bias_ref[:, pl.ds(off, seq)]

    def softmax_map(qm):
        s = lax.dot_general(qm * scale, k, NT_DIMS, preferred_element_type=F32) + bias
        m = jnp.max(s, axis=-1, keepdims=True)
        p = jnp.exp(s - m)
        return p / jnp.sum(p, axis=-1, keepdims=True)

    p0 = softmax_map(jnp.where(lane < A_DH, q, zero))
    p1 = softmax_map(jnp.where(lane >= A_DH, q, zero))
    attn = (p0 - lam_ref[0] * p1).astype(BF16)
    o = jnp.dot(attn, v_ref[...], preferred_element_type=F32)
    o_ref[...] = (_rms(o, gain_ref[...]) * out_scale).astype(BF16)


def _t5_bucket(rel):
    half = T5_BUCKETS // 2
    max_exact = half // 2
    sign_off = jnp.where(rel > 0, half, 0)
    n = jnp.abs(rel)
    nf = jnp.maximum(n, 1).astype(F32)
    large = max_exact + (jnp.log(nf / max_exact) / math.log(T5_MAX_DIST / max_exact)
                         * (half - max_exact)).astype(jnp.int32)
    large = jnp.minimum(large, half - 1)
    return sign_off + jnp.where(n < max_exact, n, large)


def _t5_strip(t5_bias, seq, tq):
    n_rel = 2 * seq - 1
    vec = t5_bias[_t5_bucket(jnp.arange(n_rel) - (seq - 1))].astype(F32).T
    return _toeplitz(vec, tq, 2 * seq - tq)


def _toeplitz(vec, n_rows, n_cols):
    n = vec.shape[-1]
    lead = vec.shape[:-1]
    assert n_cols + n_rows - 1 <= n and n_cols <= n - 1
    rolled = jnp.roll(vec, -(n_rows - 1), axis=-1)
    reps = -(-(n_rows * (n - 1)) // n)
    flat = jnp.broadcast_to(rolled[..., None, :], lead + (reps, n)).reshape(lead + (reps * n,))
    return flat[..., :n_rows * (n - 1)].reshape(lead + (n_rows, n - 1))[..., :n_cols]


def _diff_attn(proj3, lam, strip, subln, out_scale):
    b, seq, _ = proj3.shape
    tq = ATT_TQ
    kblk, vblk = A_W // LANES, 2 * A_W // LANES
    return pl.pallas_call(
        functools.partial(_diff_attn_kernel, seq=seq, out_scale=out_scale),
        grid=(A_HEADS, b, seq // tq),
        in_specs=[
            pl.BlockSpec(memory_space=pltpu.SMEM),
            pl.BlockSpec((None, tq, LANES), lambda h, i, j: (i, j, h)),
            pl.BlockSpec((None, seq, LANES), lambda h, i, j: (i, 0, kblk + h)),
            pl.BlockSpec((None, seq, LANES), lambda h, i, j: (i, 0, vblk + h)),
            pl.BlockSpec((None, tq, 2 * seq - tq), lambda h, i, j: (h, 0, 0)),
            pl.BlockSpec((1, LANES), lambda h, i, j: (0, 0)),
        ],
        out_specs=pl.BlockSpec((None, tq, LANES), lambda h, i, j: (i, j, h)),
        out_shape=jax.ShapeDtypeStruct((b, seq, A_W), BF16),
        compiler_params=_cparams(("parallel", "parallel", "parallel")),
        name="diff_attn",
    )(lam, proj3, proj3, proj3, strip, subln.reshape(1, LANES))


def _na_block_geometry(rows):
    nblk = rows // NA_QROWS
    kh = min(NA_MAX_ROWS, rows)
    starts, classes, reps = [], [], []
    for j in range(nblk):
        ks = min(max(j * NA_QROWS - kh // 2, 0), rows - NA_KROWS)
        rel = (j * NA_QROWS - ks,) + tuple(
            min(max(r - kh // 2, 0), rows - kh) - ks for r in range(j * NA_QROWS, (j + 1) * NA_QROWS))
        starts.append(ks)
        if rel not in reps:
            reps.append(rel)
        classes.append(reps.index(rel))
    return starts, classes, reps


def _na_kernel(q_ref, k_ref, v_ref, tab_ref, o_ref, *, starts, classes):
    qn, kn = NA_QROWS * GRID_W, NA_KROWS * GRID_W
    scale = B_DH ** -0.5
    lane = lax.broadcasted_iota(jnp.int32, (qn, LANES), 1)
    for j, (ks, cls) in enumerate(zip(starts, classes)):
        q = q_ref[j * qn:(j + 1) * qn, :]
        kw = k_ref[ks * GRID_W:ks * GRID_W + kn, :]
        vw = v_ref[ks * GRID_W:ks * GRID_W + kn, :]
        zero = jnp.zeros_like(q)
        outs = []
        for hl in range(2):
            in_head = (lane >= hl * B_DH) & (lane < (hl + 1) * B_DH)
            qm = jnp.where(in_head, q, zero) * scale
            s = lax.dot_general(qm, kw, NT_DIMS, preferred_element_type=F32) + tab_ref[cls, hl]
            m = jnp.max(s, axis=-1, keepdims=True)
            p = jnp.exp(s - m)
            denom = jnp.sum(p, axis=-1, keepdims=True)
            outs.append(jnp.dot(p.astype(BF16), vw, preferred_element_type=F32) / denom)
        o_ref[j * qn:(j + 1) * qn, :] = jnp.where(lane < B_DH, outs[0], outs[1]).astype(BF16)


def _na_attn(proj3, table, starts, classes):
    b, seq, _ = proj3.shape
    qblk = 3 * A_W // LANES
    kblk = qblk + B_W // LANES
    vblk = kblk + B_W // LANES
    ncls = table.shape[1]
    qn, kn = NA_QROWS * GRID_W, NA_KROWS * GRID_W
    return pl.pallas_call(
        functools.partial(_na_kernel, starts=tuple(starts), classes=tuple(classes)),
        grid=(B_HEADS // 2, b),
        in_specs=[
            pl.BlockSpec((None, seq, LANES), lambda h, i: (i, 0, qblk + h)),
            pl.BlockSpec((None, seq, LANES), lambda h, i: (i, 0, kblk + h)),
            pl.BlockSpec((None, seq, LANES), lambda h, i: (i, 0, vblk + h)),
            pl.BlockSpec((None, ncls, 2, qn, kn), lambda h, i: (h, 0, 0, 0, 0)),
        ],
        out_specs=pl.BlockSpec((None, seq, LANES), lambda h, i: (i, 0, h)),
        out_shape=jax.ShapeDtypeStruct((b, seq, B_W), BF16),
        compiler_params=_cparams(("parallel", "parallel")),
        name="na_attn",
    )(proj3, proj3, proj3, table)


def _out_proj_router_kernel(*refs, n_a):
    x_ref = refs[0]
    a_refs = refs[1:1 + n_a]
    w_refs = refs[1 + n_a:1 + 2 * n_a]
    gain_ref, wr_ref, br_ref, xo_ref, hn_ref, ids_ref, wts_ref = refs[1 + 2 * n_a:]
    acc = x_ref[...]
    for a, w in zip(a_refs, w_refs):
        acc = acc + jnp.dot(a[...], w[...], preferred_element_type=F32)
    xo_ref[...] = acc
    h = _rms(acc, gain_ref[...])
    _store_token_tiles(hn_ref, h)
    h_hi = h.astype(BF16)
    h_lo = (h - h_hi.astype(F32)).astype(BF16)
    hi = jnp.dot(h_hi, wr_ref[...], preferred_element_type=F32)
    logits = (hi[:, :LANES] + hi[:, LANES:]
              + jnp.dot(h_lo, wr_ref[:, :LANES], preferred_element_type=F32) + br_ref[...])
    lane = lax.broadcasted_iota(jnp.int32, logits.shape, 1)
    neg = jnp.float32(-jnp.inf)
    big = jnp.int32(LANES)

    def masked_softmax(mask):
        z = jnp.where(mask, logits, neg)
        e = jnp.exp(z - jnp.max(z, axis=-1, keepdims=True))
        return e / jnp.sum(e, axis=-1, keepdims=True)

    def top1(p, mask):
        w = jnp.max(jnp.where(mask, p, -1.0), axis=-1, keepdims=True)
        idx = jnp.min(jnp.where(mask & (p == w), lane, big), axis=-1, keepdims=True)
        return w, idx

    is_grp = lane < N_GROUPS
    g_w, g_idx = top1(masked_softmax(is_grp), is_grp)
    e_lane = lane - N_GROUPS
    in_grp = (e_lane >= g_idx * EXP_PER_GROUP) & (e_lane < (g_idx + 1) * EXP_PER_GROUP)
    p_e = masked_softmax(in_grp)
    w1, i1 = top1(p_e, in_grp)
    rest = in_grp & (lane != i1)
    w2, i2 = top1(p_e, rest)
    denom = w1 + w2
    ids_ref[...] = jnp.where(lane == 0, i1 - N_GROUPS, i2 - N_GROUPS)
    wts_ref[...] = jnp.where(lane == 0, g_w * (w1 / denom), g_w * (w2 / denom))


def _out_proj_router(n, x, acts, ws, gain, w_router, b_router):
    tm = ROW_TILE
    n_a = len(acts)
    in_specs = [_row_spec(tm, D_MODEL)]
    in_specs += [_row_spec(tm, a.shape[1]) for a in acts]
    in_specs += [_full_spec(w.shape) for w in ws]
    in_specs += [_full_spec((1, D_MODEL)), _full_spec(w_router.shape), _full_spec((1, LANES))]
    return pl.pallas_call(
        functools.partial(_out_proj_router_kernel, n_a=n_a),
        grid=(n // tm,),
        in_specs=in_specs,
        out_specs=[_row_spec(tm, D_MODEL), _row_spec(tm * FEAT_ROWS, LANES), _row_spec(tm, LANES),
                   _row_spec(tm, LANES)],
        out_shape=[jax.ShapeDtypeStruct((n, D_MODEL), F32), jax.ShapeDtypeStruct((n * FEAT_ROWS, LANES), F32),
                   jax.ShapeDtypeStruct((n, LANES), jnp.int32), jax.ShapeDtypeStruct((n, LANES), F32)],
        compiler_params=_cparams(("parallel",)),
        name="out_proj_router",
    )(x, *acts, *ws, gain.reshape(1, D_MODEL), w_router, b_router)


def _router_params(w_grp, b_grp, w_exp, b_exp):
    pad = LANES - N_GROUPS - N_EXPERTS
    w = jnp.concatenate([w_grp, w_exp, jnp.zeros((D_MODEL, pad), F32)], axis=1)
    b = jnp.concatenate([b_grp, b_exp, jnp.zeros((pad,), F32)]).reshape(1, LANES)
    w_hi = w.astype(BF16)
    w_lo = (w - w_hi.astype(F32)).astype(BF16)
    return jnp.concatenate([w_hi, w_lo], axis=1), b


def _moe_plan(ids, wts, n):
    tm = MOE_TILE
    n_assign = 2 * n
    p_rows = n_assign + N_EXPERTS * tm
    e_flat = ids.T.reshape(-1)
    w_flat = wts.T.reshape(-1)
    idx_bits = max(n_assign - 1, 1).bit_length()
    assert idx_bits + N_EXPERTS.bit_length() <= 31
    key = jnp.sort((e_flat << idx_bits) | jnp.arange(n_assign, dtype=jnp.int32))
    order = key & ((1 << idx_bits) - 1)
    experts = jnp.arange(N_EXPERTS, dtype=jnp.int32)
    counts = jnp.sum((key >> idx_bits)[None, :] == experts[:, None], axis=1, dtype=jnp.int32)
    off = jnp.cumsum(counts) - counts
    padded = (counts + tm - 1) // tm * tm
    p_end = jnp.cumsum(padded)
    p_off = p_end - padded
    nt = p_rows // tm
    tile_start = jnp.arange(nt, dtype=jnp.int32) * tm
    tile_expert = jnp.minimum(jnp.sum(p_end[None, :] <= tile_start[:, None], axis=1, dtype=jnp.int32),
                              N_EXPERTS - 1)
    tile_count = jnp.clip(counts[tile_expert] - (tile_start - p_off[tile_expert]), 0, tm).astype(jnp.int32)
    n_used = (p_end[-1] // tm).astype(jnp.int32).reshape(1)
    in_tile = jnp.arange(tm, dtype=jnp.int32)[None, :]
    real = in_tile < tile_count[:, None]
    s_idx = jnp.clip((off[tile_expert] + tile_start - p_off[tile_expert])[:, None] + in_tile, 0, n_assign - 1)
    assign = jnp.where(real, order[s_idx], 0)
    src = (assign % n).reshape(p_rows)
    dest = assign.reshape(p_rows)
    w_row = jnp.where(real, w_flat[assign], 0.0).reshape(p_rows)
    return (src.reshape(nt, 1, tm), dest.reshape(nt, 1, tm), w_row.reshape(p_rows, 1), tile_expert,
            tile_count, n_used)


def _moe_kernel(te_ref, tc_ref, nu_ref, src_ref, dst_ref, wrow_ref, x_hbm, wg_ref, wu_ref, wd_ref, y_hbm,
                xbuf, ybuf, wgb, wub, wdb, gsem, ssem):
    tm = MOE_TILE
    i = pl.program_id(0)
    n_used = nu_ref[0]
    last = pl.num_programs(0) - 1

    def tile_rows(r):
        return pl.ds(pl.multiple_of(r * FEAT_ROWS, FEAT_ROWS), FEAT_ROWS)

    def gather_copy(slot, r, tok):
        return pltpu.make_async_copy(x_hbm.at[tile_rows(tok)], xbuf.at[slot, tile_rows(r)], gsem.at[slot])

    def scatter_copy(slot, r, row):
        return pltpu.make_async_copy(ybuf.at[slot, tile_rows(r)], y_hbm.at[tile_rows(row)], ssem.at[slot])

    def start_all(make):
        def body(r2, c):
            make(2 * r2).start(priority=0)
            make(2 * r2 + 1).start(priority=1)
            return c
        lax.fori_loop(0, tm // 2, body, 0, unroll=4)

    def start_rows(count, make):
        @pl.when(count == tm)
        def _():
            start_all(make)

        @pl.when(count < tm)
        def _():
            lax.fori_loop(0, count, lambda r, c: (make(r).start(), c)[1], 0)

    def wait_scatter(t):
        slot = t % 2
        count = tc_ref[t]

        @pl.when(count == tm)
        def _():
            pltpu.make_async_copy(ybuf.at[slot], y_hbm.at[pl.ds(0, tm * FEAT_ROWS)], ssem.at[slot]).wait()

        @pl.when(count < tm)
        def _():
            lax.fori_loop(0, count, lambda r, c: (scatter_copy(slot, r, 0).wait(), c)[1], 0)

    t = i - 1
    ts = i - 2
    do_gather = i < n_used
    do_compute = (i >= 1) & (i <= n_used)
    do_scatter = (i >= 2) & (i <= n_used + 1)
    slot_c = t % 2
    slot_o = i % 2
    n_groups = 2 * (D_EXPERT // MOE_COLS) + D_MODEL // MOE_COLS
    per_group = tm // n_groups

    def compute(dma_group):
        x = _load_token_tiles(xbuf, tm, lead=(slot_c,)).astype(BF16)
        hid = []
        k = 0
        for c in range(D_EXPERT // MOE_COLS):
            cols = slice(c * MOE_COLS, (c + 1) * MOE_COLS)
            g = jnp.dot(x, wgb[:, cols], preferred_element_type=F32)
            dma_group(k)
            u = jnp.dot(x, wub[:, cols], preferred_element_type=F32)
            hid.append((g * _sigmoid(g) * u).astype(BF16))
            dma_group(k + 1)
            k += 2
        hid = jnp.concatenate(hid, axis=1)
        w_row = wrow_ref[...]
        for c in range(D_MODEL // MOE_COLS):
            y = jnp.dot(hid, wdb[:, c * MOE_COLS:(c + 1) * MOE_COLS], preferred_element_type=F32) * w_row
            for j in range(MOE_COLS // LANES):
                ybuf[slot_c, pl.ds(c * (MOE_COLS // LANES) + j, tm, stride=FEAT_ROWS), :] = (
                    y[:, j * LANES:(j + 1) * LANES])
            dma_group(k)
            k += 1

    @pl.when(do_compute)
    def _():
        pltpu.make_async_copy(x_hbm.at[pl.ds(0, tm * FEAT_ROWS)], xbuf.at[slot_c], gsem.at[slot_c]).wait()

        @pl.when(t >= 2)
        def _():
            wait_scatter(t - 2)

        @pl.when(jnp.logical_or(t == 0, te_ref[t] != te_ref[jnp.maximum(t - 1, 0)]))
        def _():
            wgb[...] = wg_ref[...].astype(BF16)
            wub[...] = wu_ref[...].astype(BF16)
            wdb[...] = wd_ref[...].astype(BF16)

    steady = do_gather & do_compute & do_scatter & (tc_ref[jnp.maximum(ts, 0)] == tm)

    @pl.when(steady)
    def _():
        def dma_group(k):
            for r in range(k * per_group, (k + 1) * per_group):
                gather_copy(slot_o, r, src_ref[0, r]).start(priority=r % 2)
                scatter_copy(slot_o, r, dst_ref[0, r]).start(priority=(r + 1) % 2)
        compute(dma_group)

    @pl.when(jnp.logical_not(steady))
    def _():
        @pl.when(do_gather)
        def _():
            start_all(lambda r: gather_copy(slot_o, r, src_ref[0, r]))

        @pl.when(do_compute)
        def _():
            compute(lambda k: None)

        @pl.when(do_scatter)
        def _():
            start_rows(tc_ref[jnp.maximum(ts, 0)], lambda r: scatter_copy(slot_o, r, dst_ref[0, r]))

    @pl.when(i == last)
    def _():
        wait_scatter(n_used - 1)
        wait_scatter(n_used - 2)


def _moe_experts(n, hn, plan, w_gate, w_up, w_down, layer):
    tm = MOE_TILE
    src, dest, w_row, tile_expert, tile_count, n_used = plan
    nt = src.shape[0]
    assert nt >= 2
    prev = lambda i: jnp.clip(i - 1, 0, nt - 1)
    grid_spec = pltpu.PrefetchScalarGridSpec(
        num_scalar_prefetch=3,
        grid=(nt + 2,),
        in_specs=[
            pl.BlockSpec((None, 1, tm), lambda i, te, tc, nu: (jnp.minimum(i, nt - 1), 0, 0), memory_space=pltpu.SMEM),
            pl.BlockSpec((None, 1, tm), lambda i, te, tc, nu: (jnp.clip(i - 2, 0, nt - 1), 0, 0),
                         memory_space=pltpu.SMEM),
            pl.BlockSpec((tm, 1), lambda i, te, tc, nu: (prev(i), 0)),
            pl.BlockSpec(memory_space=pl.ANY),
            pl.BlockSpec((None, None, D_MODEL, D_EXPERT), lambda i, te, tc, nu: (layer, te[prev(i)], 0, 0)),
            pl.BlockSpec((None, None, D_MODEL, D_EXPERT), lambda i, te, tc, nu: (layer, te[prev(i)], 0, 0)),
            pl.BlockSpec((None, None, D_EXPERT, D_MODEL), lambda i, te, tc, nu: (layer, te[prev(i)], 0, 0)),
        ],
        out_specs=pl.BlockSpec(memory_space=pl.ANY),
        scratch_shapes=[
            pltpu.VMEM((2, tm * FEAT_ROWS, LANES), F32),
            pltpu.VMEM((2, tm * FEAT_ROWS, LANES), F32),
            pltpu.VMEM((D_MODEL, D_EXPERT), BF16),
            pltpu.VMEM((D_MODEL, D_EXPERT), BF16),
            pltpu.VMEM((D_EXPERT, D_MODEL), BF16),
            pltpu.SemaphoreType.DMA((2,)),
            pltpu.SemaphoreType.DMA((2,)),
        ],
    )
    return pl.pallas_call(
        _moe_kernel,
        grid_spec=grid_spec,
        out_shape=jax.ShapeDtypeStruct((2 * n * FEAT_ROWS, LANES), F32),
        compiler_params=_cparams(("arbitrary",)),
        name="moe_experts",
    )(tile_expert, tile_count, n_used, src, dest, w_row, hn, w_gate, w_up, w_down)


def _gla_kernel(q_ref, k_ref, v_ref, g_ref, lf_ref, lb_ref, gain_ref, o_ref,
                bf_s, bb_s, s_all, stf, stb, *, seq):
    c_len = C_CHUNK
    nc = seq // c_len
    scale = C_DK ** -0.5
    row = lax.broadcasted_iota(jnp.int32, (c_len, C_DK), 0)
    shifts = [1 << s for s in range(int(math.log2(c_len)))]

    def rows(c):
        return pl.ds(pl.multiple_of(c * c_len, c_len), c_len)

    def cum_body(c, carry):
        y = lf_ref[rows(c), :]
        for s in shifts:
            y = y + jnp.where(row >= s, pltpu.roll(y, s, 0), 0.0)
        bf_s[rows(c), :] = y
        y = lb_ref[rows(c), :]
        for s in shifts:
            y = y + jnp.where(row < c_len - s, pltpu.roll(y, c_len - s, 0), 0.0)
        bb_s[rows(c), :] = y
        return carry

    lax.fori_loop(0, nc, cum_body, 0)

    stf[...] = jnp.zeros_like(stf)
    stb[...] = jnp.zeros_like(stb)

    def state_step(c, b_s, last_row, st, lane0):
        k = k_ref[rows(c), :].astype(F32)
        bc = b_s[rows(c), :]
        bl = bc[last_row:last_row + 1, :]
        kd = (k * jnp.exp(bl - bc)).astype(BF16)
        kv_t = lax.dot_general(v_ref[rows(c), :], kd, TN_DIMS, preferred_element_type=F32)
        s_all[c, :, lane0:lane0 + C_DK] = st[...].astype(BF16)
        st[...] = st[...] * jnp.exp(bl) + kv_t

    def state_body(i, carry):
        state_step(i, bf_s, c_len - 1, stf, 0)
        state_step(nc - 1 - i, bb_s, 0, stb, C_DK)
        return carry

    lax.fori_loop(0, nc, state_body, 0, unroll=2)

    ri = lax.broadcasted_iota(jnp.int32, (c_len, c_len), 0)
    ci = lax.broadcasted_iota(jnp.int32, (c_len, c_len), 1)

    def out_body(c, carry):
        q = q_ref[rows(c), :].astype(F32) * scale
        k = k_ref[rows(c), :].astype(F32)
        bcf = bf_s[rows(c), :]
        bcb = bb_s[rows(c), :]
        qf = (q * jnp.exp(bcf)).astype(BF16)
        kf = (k * jnp.exp(-bcf)).astype(BF16)
        qb = (q * jnp.exp(bcb)).astype(BF16)
        kb = (k * jnp.exp(-bcb)).astype(BF16)
        att_f = lax.dot_general(qf, kf, NT_DIMS, preferred_element_type=F32)
        att_b = lax.dot_general(qb, kb, NT_DIMS, preferred_element_type=F32)
        att = jnp.where(ri >= ci, att_f, att_b).astype(BF16)
        q_both = jnp.concatenate([qf, qb], axis=1)
        o = (jnp.dot(att, v_ref[rows(c), :], preferred_element_type=F32)
             + lax.dot_general(q_both, s_all[c], NT_DIMS, preferred_element_type=F32))
        g = g_ref[rows(c), :].astype(F32)
        o_ref[rows(c), :] = (_rms(o, gain_ref[...]) * (g * _sigmoid(g))).astype(BF16)
        return carry

    lax.fori_loop(0, nc, out_body, 0, unroll=4)


def _gla(proj3, lf3, lb3, out_norm):
    b, seq, _ = proj3.shape
    kw = C_HEADS * C_DK
    k_blk = kw // C_DK
    v_blk = 2 * kw // C_DV
    g_blk = v_blk + C_HEADS
    nc = seq // C_CHUNK
    return pl.pallas_call(
        functools.partial(_gla_kernel, seq=seq),
        grid=(b, C_HEADS),
        in_specs=[
            pl.BlockSpec((None, seq, C_DK), lambda i, h: (i, 0, h)),
            pl.BlockSpec((None, seq, C_DK), lambda i, h: (i, 0, k_blk + h)),
            pl.BlockSpec((None, seq, C_DV), lambda i, h: (i, 0, v_blk + h)),
            pl.BlockSpec((None, seq, C_DV), lambda i, h: (i, 0, g_blk + h)),
            pl.BlockSpec((None, seq, C_DK), lambda i, h: (i, 0, h)),
            pl.BlockSpec((None, seq, C_DK), lambda i, h: (i, 0, h)),
            pl.BlockSpec((1, C_DV), lambda i, h: (0, 0)),
        ],
        out_specs=pl.BlockSpec((None, seq, C_DV), lambda i, h: (i, 0, h)),
        out_shape=jax.ShapeDtypeStruct((b, seq, C_HEADS * C_DV), BF16),
        scratch_shapes=[
            pltpu.VMEM((seq, C_DK), F32),
            pltpu.VMEM((seq, C_DK), F32),
            pltpu.VMEM((nc, C_DV, 2 * C_DK), BF16),
            pltpu.VMEM((C_DV, C_DK), F32),
            pltpu.VMEM((C_DV, C_DK), F32),
        ],
        compiler_params=_cparams(("parallel", "parallel")),
        name="gla",
    )(proj3, proj3, proj3, proj3, lf3, lb3, out_norm.reshape(1, C_DV))


def _final_kernel(x_ref, y0_ref, y1_ref, gain_ref, o_ref):
    tm = x_ref.shape[0]
    x = x_ref[...] + _load_token_tiles(y0_ref, tm) + _load_token_tiles(y1_ref, tm)
    o_ref[...] = _rms(x, gain_ref[...])


def _final_norm(n, x, y, gain):
    tm = ROW_TILE
    return pl.pallas_call(
        _final_kernel,
        grid=(n // tm,),
        in_specs=[_row_spec(tm, D_MODEL), _row_spec(tm * FEAT_ROWS, LANES),
                  _row_spec(tm * FEAT_ROWS, LANES, n // tm), _full_spec((1, D_MODEL))],
        out_specs=_row_spec(tm, D_MODEL),
        out_shape=jax.ShapeDtypeStruct((n, D_MODEL), F32),
        compiler_params=_cparams(("parallel",)),
        name="final_norm",
    )(x, y, y, gain.reshape(1, D_MODEL))


def _moe_layer(n, hn, ids_pad, wts_pad, w_gate, w_up, w_down, layer):
    plan = _moe_plan(ids_pad[:, :2], wts_pad[:, :2], n)
    return _moe_experts(n, hn, plan, w_gate, w_up, w_down, layer)


def kernel(x, t5_bias, norm_mix, norm_ffn, norm_final, ev_w_in, ev_lambda, ev_subln, ev_rpb, ev_w_out,
           od_w_in, od_w_gk_fwd, od_b_gk_fwd, od_w_gk_bwd, od_b_gk_bwd, od_out_norm, od_w_out,
           moe_w_grp, moe_b_grp, moe_w_exp, moe_b_exp, moe_w_gate, moe_w_up, moe_w_down):
    b, seq, d = x.shape
    n = b * seq
    rows = seq // GRID_W
    assert d == D_MODEL and n % ROW_TILE == 0 and seq % ATT_TQ == 0 and seq % C_CHUNK == 0
    assert rows % NA_QROWS == 0 and rows >= NA_KROWS and MOE_TILE <= ROW_TILE
    x2 = x.reshape(n, d)

    (proj,) = _in_proj_call(n, [(x2, 0)], norm_mix[0], ev_w_in[0].astype(BF16))
    proj3 = proj.reshape(b, seq, proj.shape[1])
    lam_init = 0.8 - 0.6 * math.exp(-0.3 * 0)
    lp = ev_lambda[0].astype(F32)
    lam = (jnp.exp(jnp.sum(lp[0] * lp[1])) - jnp.exp(jnp.sum(lp[2] * lp[3])) + lam_init).reshape(1)
    strip = _t5_strip(t5_bias, seq, ATT_TQ)
    a_diff = _diff_attn(proj3, lam, strip, ev_subln[0], 1.0 - lam_init)

    starts, classes, _ = _na_block_geometry(rows)
    table = _na_table(ev_rpb[0], rows)
    a_na = _na_attn(proj3, table, starts, classes)

    w_out = ev_w_out[0].astype(BF16)
    w_r, b_r = _router_params(moe_w_grp[0], moe_b_grp[0], moe_w_exp[0], moe_b_exp[0])
    x1, hn, ids, wts = _out_proj_router(
        n, x2, [a_diff.reshape(n, A_W), a_na.reshape(n, B_W)], [w_out[:A_W], w_out[A_W:]],
        norm_ffn[0], w_r, b_r)
    y = _moe_layer(n, hn, ids, wts, moe_w_gate, moe_w_up, moe_w_down, 0)

    kw = C_HEADS * C_DK
    main_w = 2 * kw + 2 * C_HEADS * C_DV
    w_in = od_w_in[0]
    w_gates = jnp.pad(w_in[:, main_w:], ((0, 0), (0, LANES - 2 * C_GATE_RANK))).astype(BF16)
    wf = jnp.pad(od_w_gk_fwd[0], ((0, LANES - C_GATE_RANK), (0, 0))).astype(BF16)
    wb = jnp.pad(od_w_gk_bwd[0], ((C_GATE_RANK, LANES - 2 * C_GATE_RANK), (0, 0))).astype(BF16)
    x2b, proj, lf, lb = _in_proj_call(
        n, [(x1, 0), (y, 0), (y, n)], norm_mix[1], w_in[:, :main_w].astype(BF16),
        gates=[w_gates, wf, od_b_gk_fwd[0].reshape(1, kw), wb, od_b_gk_bwd[0].reshape(1, kw)])
    a_gla = _gla(proj.reshape(b, seq, main_w), lf.reshape(b, seq, kw), lb.reshape(b, seq, kw), od_out_norm[0])

    w_r, b_r = _router_params(moe_w_grp[1], moe_b_grp[1], moe_w_exp[1], moe_b_exp[1])
    x3, hn, ids, wts = _out_proj_router(
        n, x2b, [a_gla.reshape(n, C_HEADS * C_DV)], [od_w_out[0].astype(BF16)], norm_ffn[1], w_r, b_r)
    y = _moe_layer(n, hn, ids, wts, moe_w_gate, moe_w_up, moe_w_down, 1)

    return _final_norm(n, x3, y, norm_final).reshape(b, seq, d)


def _na_table(rpb, rows):
    kh = min(NA_MAX_ROWS, rows)
    starts, classes, reps = _na_block_geometry(rows)
    qn, kn = NA_QROWS * GRID_W, NA_KROWS * GRID_W
    q_row, q_col = jnp.arange(qn) // GRID_W, jnp.arange(qn) % GRID_W
    k_row, k_col = jnp.arange(kn) // GRID_W, jnp.arange(kn) % GRID_W
    col_start = jnp.clip(q_col - NA_COLS // 2, 0, GRID_W - NA_COLS)
    col_ok = (k_col[None, :] >= col_start[:, None]) & (k_col[None, :] < col_start[:, None] + NA_COLS)
    side = GRID_W - NA_COLS
    col_bias = _toeplitz(jnp.pad(rpb.astype(F32), ((0, 0), (0, 0), (side, side))), GRID_W, GRID_W)
    per_class = []
    for cls in range(len(reps)):
        j = classes.index(cls)
        q_abs = j * NA_QROWS + q_row
        k_abs = starts[j] + k_row
        row_start = jnp.clip(q_abs - kh // 2, 0, rows - kh)
        row_ok = (k_abs[None, :] >= row_start[:, None]) & (k_abs[None, :] < row_start[:, None] + kh)
        dr = jnp.clip(starts[j] + jnp.arange(NA_KROWS)[None, :] - (j * NA_QROWS + jnp.arange(NA_QROWS))[:, None]
                      + NA_MAX_ROWS - 1, 0, 2 * NA_MAX_ROWS - 2)
        blocks = col_bias[:, dr]
        bias = jnp.transpose(blocks, (0, 1, 3, 2, 4)).reshape(B_HEADS, qn, kn)
        per_class.append(jnp.where((row_ok & col_ok)[None], bias, -jnp.inf))
    t = jnp.stack(per_class, axis=1)
    return jnp.transpose(t.reshape(B_HEADS // 2, 2, len(reps), qn, kn), (0, 2, 1, 3, 4))
```

```python
import functools
import math

import jax
import jax.numpy as jnp
from jax import lax
from jax.experimental import pallas as pl
from jax.experimental.pallas import tpu as pltpu

D_MODEL = 1024
GRID_W = 64
A_HEADS = 4
A_DH = 64
B_HEADS = 8
B_DH = 64
NA_MAX_ROWS = 8
NA_COLS = 16
T5_BUCKETS = 32
T5_MAX_DIST = 128
C_HEADS = 4
C_DK = 128
C_DV = 256
C_GATE_RANK = 16
C_GATE_NORM = 16.0
C_CHUNK = 64
N_GROUPS = 4
EXP_PER_GROUP = 8
N_EXPERTS = N_GROUPS * EXP_PER_GROUP
D_EXPERT = 512
EPS = 1e-6

A_W = A_HEADS * 2 * A_DH
B_W = B_HEADS * B_DH
LANES = 128
FEAT_ROWS = D_MODEL // LANES
VMEM_LIMIT = 56 * 1024 * 1024

ROW_TILE = 512
ATT_TQ = 256
NA_QROWS = 4
NA_KROWS = 12
MOE_TILE = 256
ROUTE_ROWS = 8

F32 = jnp.float32
BF16 = jnp.bfloat16
NT_DIMS = (((1,), (1,)), ((), ()))
TN_DIMS = (((0,), (0,)), ((), ()))


def _cparams(sem):
    return pltpu.CompilerParams(dimension_semantics=sem, vmem_limit_bytes=VMEM_LIMIT)


def _load_token_tiles(ref, tm, lead=()):
    return jnp.concatenate(
        [ref[lead + (pl.ds(j, tm, stride=FEAT_ROWS), slice(None))] for j in range(FEAT_ROWS)], axis=1)


def _store_token_tiles(ref, val, lead=()):
    tm = val.shape[0]
    for j in range(FEAT_ROWS):
        ref[lead + (pl.ds(j, tm, stride=FEAT_ROWS), slice(None))] = val[:, j * LANES:(j + 1) * LANES]


def _rms(x, gain):
    return x * lax.rsqrt(jnp.mean(x * x, axis=-1, keepdims=True) + EPS) * gain


def _sigmoid(x):
    return 1.0 / (1.0 + jnp.exp(-x))


def _log_sigmoid(z):
    return jnp.minimum(z, 0.0) - jnp.log1p(jnp.exp(-jnp.abs(z)))


def _in_proj_kernel(*refs, n_parts, gates):
    parts = refs[:n_parts]
    gain_ref, w_ref = refs[n_parts], refs[n_parts + 1]
    pos = n_parts + 2
    if gates:
        wg_ref = refs[pos]
        pos += 1
    outs = refs[pos:]
    x = parts[0][...]
    for p in parts[1:]:
        x = x + _load_token_tiles(p, x.shape[0])
    oi = 0
    if n_parts > 1:
        outs[0][...] = x
        oi = 1
    h = _rms(x, gain_ref[...]).astype(BF16)
    proj_ref = outs[oi]
    n_out = proj_ref.shape[1]
    step = 512
    for j in range(n_out // step):
        proj_ref[:, j * step:(j + 1) * step] = jnp.dot(
            h, w_ref[:, j * step:(j + 1) * step], preferred_element_type=F32).astype(BF16)
    if gates:
        outs[oi + 1][...] = jnp.dot(h, wg_ref[...], preferred_element_type=F32).astype(BF16)


def _row_spec(tm, width, block_off=0):
    return pl.BlockSpec((tm, width), lambda i, o=block_off: (i + o, 0))


def _full_spec(shape):
    nd = len(shape)
    return pl.BlockSpec(shape, lambda i, _nd=nd: (0,) * _nd)


def _in_proj_call(n, parts, gain, w, gates=None):
    tm = ROW_TILE
    n_parts = len(parts)
    n_out = w.shape[1]
    in_specs = [_row_spec(tm, D_MODEL)] + [_row_spec(tm * FEAT_ROWS, LANES, off // tm) for _, off in parts[1:]]
    args = [a for a, _ in parts]
    in_specs += [_full_spec((1, D_MODEL)), _full_spec(w.shape)]
    args += [gain.reshape(1, D_MODEL), w]
    out_shape, out_specs = [], []
    if n_parts > 1:
        out_shape.append(jax.ShapeDtypeStruct((n, D_MODEL), F32))
        out_specs.append(_row_spec(tm, D_MODEL))
    out_shape.append(jax.ShapeDtypeStruct((n, n_out), BF16))
    out_specs.append(_row_spec(tm, n_out))
    if gates is not None:
        in_specs.append(_full_spec(gates.shape))
        args.append(gates)
        out_shape.append(jax.ShapeDtypeStruct((n, LANES), BF16))
        out_specs.append(_row_spec(tm, LANES))
    return pl.pallas_call(
        functools.partial(_in_proj_kernel, n_parts=n_parts, gates=gates is not None),
        grid=(n // tm,),
        in_specs=in_specs,
        out_specs=out_specs,
        out_shape=out_shape,
        compiler_params=_cparams(("parallel",)),
        name="in_proj_gla" if gates is not None else "in_proj_attn",
    )(*args)


def _diff_attn_kernel(lam_ref, q_ref, k_ref, v_ref, bias_ref, gain_ref, o_ref, *, seq, out_scale):
    tq = q_ref.shape[0]
    qb = pl.program_id(2)
    q = q_ref[...]
    k = k_ref[...]
    lane = lax.broadcasted_iota(jnp.int32, q.shape, 1)
    scale = A_DH ** -0.5
    zero = jnp.zeros_like(q)
    off = pl.multiple_of((seq - tq) - qb * tq, LANES)
    bias = bias_ref[:, pl.ds(off, seq)]

    def softmax_map(qm):
        s = lax.dot_general(qm * scale, k, NT_DIMS, preferred_element_type=F32) + bias
        m = jnp.max(s, axis=-1, keepdims=True)
        p = jnp.exp(s - m)
        return p / jnp.sum(p, axis=-1, keepdims=True)

    p0 = softmax_map(jnp.where(lane < A_DH, q, zero))
    p1 = softmax_map(jnp.where(lane >= A_DH, q, zero))
    attn = (p0 - lam_ref[0] * p1).astype(BF16)
    o = jnp.dot(attn, v_ref[...], preferred_element_type=F32)
    o_ref[...] = (_rms(o, gain_ref[...]) * out_scale).astype(BF16)


def _t5_bucket(rel):
    half = T5_BUCKETS // 2
    max_exact = half // 2
    sign_off = jnp.where(rel > 0, half, 0)
    n = jnp.abs(rel)
    nf = jnp.maximum(n, 1).astype(F32)
    large = max_exact + (jnp.log(nf / max_exact) / math.log(T5_MAX_DIST / max_exact)
                         * (half - max_exact)).astype(jnp.int32)
    large = jnp.minimum(large, half - 1)
    return sign_off + jnp.where(n < max_exact, n, large)


def _t5_strip(t5_bias, seq, tq):
    n_rel = 2 * seq - 1
    vec = t5_bias[_t5_bucket(jnp.arange(n_rel) - (seq - 1))].astype(F32).T
    return _toeplitz(vec, tq, 2 * seq - tq)


def _toeplitz(vec, n_rows, n_cols):
    n = vec.shape[-1]
    lead = vec.shape[:-1]
    assert n_cols + n_rows - 1 <= n and n_cols <= n - 1
    rolled = jnp.roll(vec, -(n_rows - 1), axis=-1)
    reps = -(-(n_rows * (n - 1)) // n)
    flat = jnp.broadcast_to(rolled[..., None, :], lead + (reps, n)).reshape(lead + (reps * n,))
    return flat[..., :n_rows * (n - 1)].reshape(lead + (n_rows, n - 1))[..., :n_cols]


def _diff_attn(proj3, lam, strip, subln, out_scale):
    b, seq, _ = proj3.shape
    tq = ATT_TQ
    kblk, vblk = A_W // LANES, 2 * A_W // LANES
    return pl.pallas_call(
        functools.partial(_diff_attn_kernel, seq=seq, out_scale=out_scale),
        grid=(A_HEADS, b, seq // tq),
        in_specs=[
            pl.BlockSpec(memory_space=pltpu.SMEM),
            pl.BlockSpec((None, tq, LANES), lambda h, i, j: (i, j, h)),
            pl.BlockSpec((None, seq, LANES), lambda h, i, j: (i, 0, kblk + h)),
            pl.BlockSpec((None, seq, LANES), lambda h, i, j: (i, 0, vblk + h)),
            pl.BlockSpec((None, tq, 2 * seq - tq), lambda h, i, j: (h, 0, 0)),
            pl.BlockSpec((1, LANES), lambda h, i, j: (0, 0)),
        ],
        out_specs=pl.BlockSpec((None, tq, LANES), lambda h, i, j: (i, j, h)),
        out_shape=jax.ShapeDtypeStruct((b, seq, A_W), BF16),
        compiler_params=_cparams(("parallel", "parallel", "parallel")),
        name="diff_attn",
    )(lam, proj3, proj3, proj3, strip, subln.reshape(1, LANES))


def _na_block_geometry(rows):
    nblk = rows // NA_QROWS
    kh = min(NA_MAX_ROWS, rows)
    starts, classes, reps = [], [], []
    for j in range(nblk):
        ks = min(max(j * NA_QROWS - kh // 2, 0), rows - NA_KROWS)
        rel = (j * NA_QROWS - ks,) + tuple(
            min(max(r - kh // 2, 0), rows - kh) - ks for r in range(j * NA_QROWS, (j + 1) * NA_QROWS))
        starts.append(ks)
        if rel not in reps:
            reps.append(rel)
        classes.append(reps.index(rel))
    return starts, classes, reps


def _na_kernel(q_ref, k_ref, v_ref, tab_ref, o_ref, *, starts, classes):
    qn, kn = NA_QROWS * GRID_W, NA_KROWS * GRID_W
    scale = B_DH ** -0.5
    lane = lax.broadcasted_iota(jnp.int32, (qn, LANES), 1)
    for j, (ks, cls) in enumerate(zip(starts, classes)):
        q = q_ref[j * qn:(j + 1) * qn, :]
        kw = k_ref[ks * GRID_W:ks * GRID_W + kn, :]
        vw = v_ref[ks * GRID_W:ks * GRID_W + kn, :]
        zero = jnp.zeros_like(q)
        outs = []
        for hl in range(2):
            in_head = (lane >= hl * B_DH) & (lane < (hl + 1) * B_DH)
            qm = jnp.where(in_head, q, zero) * scale
            s = lax.dot_general(qm, kw, NT_DIMS, preferred_element_type=F32) + tab_ref[cls, hl]
            m = jnp.max(s, axis=-1, keepdims=True)
            p = jnp.exp(s - m)
            denom = jnp.sum(p, axis=-1, keepdims=True)
            outs.append(jnp.dot(p.astype(BF16), vw, preferred_element_type=F32) / denom)
        o_ref[j * qn:(j + 1) * qn, :] = jnp.where(lane < B_DH, outs[0], outs[1]).astype(BF16)


def _na_attn(proj3, table, starts, classes):
    b, seq, _ = proj3.shape
    qblk = 3 * A_W // LANES
    kblk = qblk + B_W // LANES
    vblk = kblk + B_W // LANES
    ncls = table.shape[1]
    qn, kn = NA_QROWS * GRID_W, NA_KROWS * GRID_W
    return pl.pallas_call(
        functools.partial(_na_kernel, starts=tuple(starts), classes=tuple(classes)),
        grid=(B_HEADS // 2, b),
        in_specs=[
            pl.BlockSpec((None, seq, LANES), lambda h, i: (i, 0, qblk + h)),
            pl.BlockSpec((None, seq, LANES), lambda h, i: (i, 0, kblk + h)),
            pl.BlockSpec((None, seq, LANES), lambda h, i: (i, 0, vblk + h)),
            pl.BlockSpec((None, ncls, 2, qn, kn), lambda h, i: (h, 0, 0, 0, 0)),
        ],
        out_specs=pl.BlockSpec((None, seq, LANES), lambda h, i: (i, 0, h)),
        out_shape=jax.ShapeDtypeStruct((b, seq, B_W), BF16),
        compiler_params=_cparams(("parallel", "parallel")),
        name="na_attn",
    )(proj3, proj3, proj3, table)


def _out_proj_router_kernel(*refs, n_a):
    x_ref = refs[0]
    a_refs = refs[1:1 + n_a]
    w_refs = refs[1 + n_a:1 + 2 * n_a]
    gain_ref, wr_ref, br_ref, xo_ref, hn_ref, ids_ref, wts_ref = refs[1 + 2 * n_a:]
    acc = x_ref[...]
    for a, w in zip(a_refs, w_refs):
        acc = acc + jnp.dot(a[...], w[...], preferred_element_type=F32)
    xo_ref[...] = acc
    h = _rms(acc, gain_ref[...])
    _store_token_tiles(hn_ref, h)
    h_hi = h.astype(BF16)
    h_lo = (h - h_hi.astype(F32)).astype(BF16)
    hi = jnp.dot(h_hi, wr_ref[...], preferred_element_type=F32)
    logits = (hi[:, :LANES] + hi[:, LANES:]
              + jnp.dot(h_lo, wr_ref[:, :LANES], preferred_element_type=F32) + br_ref[...])
    lane = lax.broadcasted_iota(jnp.int32, logits.shape, 1)
    neg = jnp.float32(-jnp.inf)
    big = jnp.int32(LANES)

    def masked_softmax(mask):
        z = jnp.where(mask, logits, neg)
        e = jnp.exp(z - jnp.max(z, axis=-1, keepdims=True))
        return e / jnp.sum(e, axis=-1, keepdims=True)

    def top1(p, mask):
        w = jnp.max(jnp.where(mask, p, -1.0), axis=-1, keepdims=True)
        idx = jnp.min(jnp.where(mask & (p == w), lane, big), axis=-1, keepdims=True)
        return w, idx

    is_grp = lane < N_GROUPS
    g_w, g_idx = top1(masked_softmax(is_grp), is_grp)
    e_lane = lane - N_GROUPS
    in_grp = (e_lane >= g_idx * EXP_PER_GROUP) & (e_lane < (g_idx + 1) * EXP_PER_GROUP)
    p_e = masked_softmax(in_grp)
    w1, i1 = top1(p_e, in_grp)
    rest = in_grp & (lane != i1)
    w2, i2 = top1(p_e, rest)
    denom = w1 + w2
    ids = jnp.where(lane == 0, i1 - N_GROUPS, i2 - N_GROUPS)
    wts = jnp.where(lane == 0, g_w * (w1 / denom), g_w * (w2 / denom))
    ids_ref[...] = ids.T[:ids_ref.shape[0]]
    wts_ref[...] = wts.T[:wts_ref.shape[0]]


def _out_proj_router(n, x, acts, ws, gain, w_router, b_router):
    tm = ROW_TILE
    n_a = len(acts)
    in_specs = [_row_spec(tm, D_MODEL)]
    in_specs += [_row_spec(tm, a.shape[1]) for a in acts]
    in_specs += [_full_spec(w.shape) for w in ws]
    in_specs += [_full_spec((1, D_MODEL)), _full_spec(w_router.shape), _full_spec((1, LANES))]
    return pl.pallas_call(
        functools.partial(_out_proj_router_kernel, n_a=n_a),
        grid=(n // tm,),
        in_specs=in_specs,
        out_specs=[_row_spec(tm, D_MODEL), _row_spec(tm * FEAT_ROWS, LANES),
                   pl.BlockSpec((ROUTE_ROWS, tm), lambda i: (0, i)), pl.BlockSpec((ROUTE_ROWS, tm), lambda i: (0, i))],
        out_shape=[jax.ShapeDtypeStruct((n, D_MODEL), F32), jax.ShapeDtypeStruct((n * FEAT_ROWS, LANES), F32),
                   jax.ShapeDtypeStruct((ROUTE_ROWS, n), jnp.int32), jax.ShapeDtypeStruct((ROUTE_ROWS, n), F32)],
        compiler_params=_cparams(("parallel",)),
        name="out_proj_router",
    )(x, *acts, *ws, gain.reshape(1, D_MODEL), w_router, b_router)


def _router_params(w_grp, b_grp, w_exp, b_exp):
    pad = LANES - N_GROUPS - N_EXPERTS
    w = jnp.concatenate([w_grp, w_exp, jnp.zeros((D_MODEL, pad), F32)], axis=1)
    b = jnp.concatenate([b_grp, b_exp, jnp.zeros((pad,), F32)]).reshape(1, LANES)
    w_hi = w.astype(BF16)
    w_lo = (w - w_hi.astype(F32)).astype(BF16)
    return jnp.concatenate([w_hi, w_lo], axis=1), b


def _moe_plan(ids, wts, n):
    tm = MOE_TILE
    n_assign = 2 * n
    p_rows = n_assign + N_EXPERTS * tm
    e_flat = ids.reshape(-1)
    w_flat = wts.reshape(-1)
    idx_bits = max(n_assign - 1, 1).bit_length()
    assert idx_bits + N_EXPERTS.bit_length() <= 31
    key = jnp.sort((e_flat << idx_bits) | jnp.arange(n_assign, dtype=jnp.int32))
    order = key & ((1 << idx_bits) - 1)
    experts = jnp.arange(N_EXPERTS, dtype=jnp.int32)
    counts = jnp.sum((key >> idx_bits)[None, :] == experts[:, None], axis=1, dtype=jnp.int32)
    off = jnp.cumsum(counts) - counts
    padded = (counts + tm - 1) // tm * tm
    p_end = jnp.cumsum(padded)
    p_off = p_end - padded
    nt = p_rows // tm
    tile_start = jnp.arange(nt, dtype=jnp.int32) * tm
    tile_expert = jnp.minimum(jnp.sum(p_end[None, :] <= tile_start[:, None], axis=1, dtype=jnp.int32),
                              N_EXPERTS - 1)
    tile_count = jnp.clip(counts[tile_expert] - (tile_start - p_off[tile_expert]), 0, tm).astype(jnp.int32)
    n_used = (p_end[-1] // tm).astype(jnp.int32).reshape(1)
    in_tile = jnp.arange(tm, dtype=jnp.int32)[None, :]
    real = in_tile < tile_count[:, None]
    s_idx = jnp.clip((off[tile_expert] + tile_start - p_off[tile_expert])[:, None] + in_tile, 0, n_assign - 1)
    assign = jnp.where(real, order[s_idx], 0)
    src = (assign % n).reshape(p_rows)
    dest = assign.reshape(p_rows)
    w_row = jnp.where(real, w_flat[assign], 0.0).reshape(p_rows)
    return (src.reshape(nt, 1, tm), dest.reshape(nt, 1, tm), w_row.reshape(p_rows, 1), tile_expert,
            tile_count, n_used)


def _moe_kernel(te_ref, tc_ref, nu_ref, src_ref, dst_ref, wrow_ref, x_hbm, wg_ref, wu_ref, wd_ref, y_hbm,
                xbuf, ybuf, wgb, wub, wdb, gsem, ssem):
    tm = MOE_TILE
    i = pl.program_id(0)
    n_used = nu_ref[0]
    last = pl.num_programs(0) - 1

    def tile_rows(r):
        return pl.ds(pl.multiple_of(r * FEAT_ROWS, FEAT_ROWS), FEAT_ROWS)

    def gather_copy(slot, r, tok):
        return pltpu.make_async_copy(x_hbm.at[tile_rows(tok)], xbuf.at[slot, tile_rows(r)], gsem.at[slot])

    def scatter_copy(slot, r, row):
        return pltpu.make_async_copy(ybuf.at[slot, tile_rows(r)], y_hbm.at[tile_rows(row)], ssem.at[slot])

    def start_all(make):
        def body(r2, c):
            make(2 * r2).start(priority=0)
            make(2 * r2 + 1).start(priority=1)
            return c
        lax.fori_loop(0, tm // 2, body, 0, unroll=4)

    def start_rows(count, make):
        @pl.when(count == tm)
        def _():
            start_all(make)

        @pl.when(count < tm)
        def _():
            lax.fori_loop(0, count, lambda r, c: (make(r).start(), c)[1], 0)

    def wait_scatter(t):
        slot = t % 2
        count = tc_ref[t]

        @pl.when(count == tm)
        def _():
            pltpu.make_async_copy(ybuf.at[slot], y_hbm.at[pl.ds(0, tm * FEAT_ROWS)], ssem.at[slot]).wait()

        @pl.when(count < tm)
        def _():
            lax.fori_loop(0, count, lambda r, c: (scatter_copy(slot, r, 0).wait(), c)[1], 0)

    @pl.when(i < n_used)
    def _():
        slot = i % 2
        start_all(lambda r: gather_copy(slot, r, src_ref[0, r]))

    @pl.when((i >= 1) & (i <= n_used))
    def _():
        t = i - 1
        slot = t % 2
        pltpu.make_async_copy(x_hbm.at[pl.ds(0, tm * FEAT_ROWS)], xbuf.at[slot], gsem.at[slot]).wait()

        @pl.when(t >= 2)
        def _():
            wait_scatter(t - 2)

        @pl.when(jnp.logical_or(t == 0, te_ref[t] != te_ref[jnp.maximum(t - 1, 0)]))
        def _():
            wgb[...] = wg_ref[...].astype(BF16)
            wub[...] = wu_ref[...].astype(BF16)
            wdb[...] = wd_ref[...].astype(BF16)

        x = _load_token_tiles(xbuf, tm, lead=(slot,)).astype(BF16)
        g = jnp.dot(x, wgb[...], preferred_element_type=F32)
        u = jnp.dot(x, wub[...], preferred_element_type=F32)
        hid = (g * _sigmoid(g) * u).astype(BF16)
        y = jnp.dot(hid, wdb[...], preferred_element_type=F32)
        _store_token_tiles(ybuf, y * wrow_ref[...], lead=(slot,))

        start_rows(tc_ref[t], lambda r: scatter_copy(slot, r, dst_ref[0, r]))

    @pl.when(i == last)
    def _():
        wait_scatter(n_used - 1)

        @pl.when(n_used >= 2)
        def _():
            wait_scatter(n_used - 2)


def _moe_experts(n, hn, plan, w_gate, w_up, w_down, layer):
    tm = MOE_TILE
    src, dest, w_row, tile_expert, tile_count, n_used = plan
    nt = src.shape[0]
    prev = lambda i: jnp.maximum(i - 1, 0)
    grid_spec = pltpu.PrefetchScalarGridSpec(
        num_scalar_prefetch=3,
        grid=(nt + 1,),
        in_specs=[
            pl.BlockSpec((None, 1, tm), lambda i, te, tc, nu: (jnp.minimum(i, nt - 1), 0, 0), memory_space=pltpu.SMEM),
            pl.BlockSpec((None, 1, tm), lambda i, te, tc, nu: (prev(i), 0, 0), memory_space=pltpu.SMEM),
            pl.BlockSpec((tm, 1), lambda i, te, tc, nu: (prev(i), 0)),
            pl.BlockSpec(memory_space=pl.ANY),
            pl.BlockSpec((None, None, D_MODEL, D_EXPERT), lambda i, te, tc, nu: (layer, te[prev(i)], 0, 0)),
            pl.BlockSpec((None, None, D_MODEL, D_EXPERT), lambda i, te, tc, nu: (layer, te[prev(i)], 0, 0)),
            pl.BlockSpec((None, None, D_EXPERT, D_MODEL), lambda i, te, tc, nu: (layer, te[prev(i)], 0, 0)),
        ],
        out_specs=pl.BlockSpec(memory_space=pl.ANY),
        scratch_shapes=[
            pltpu.VMEM((2, tm * FEAT_ROWS, LANES), F32),
            pltpu.VMEM((2, tm * FEAT_ROWS, LANES), F32),
            pltpu.VMEM((D_MODEL, D_EXPERT), BF16),
            pltpu.VMEM((D_MODEL, D_EXPERT), BF16),
            pltpu.VMEM((D_EXPERT, D_MODEL), BF16),
            pltpu.SemaphoreType.DMA((2,)),
            pltpu.SemaphoreType.DMA((2,)),
        ],
    )
    return pl.pallas_call(
        _moe_kernel,
        grid_spec=grid_spec,
        out_shape=jax.ShapeDtypeStruct((2 * n * FEAT_ROWS, LANES), F32),
        compiler_params=_cparams(("arbitrary",)),
        name="moe_experts",
    )(tile_expert, tile_count, n_used, src, dest, w_row, hn, w_gate, w_up, w_down)


def _gla_kernel(q_ref, k_ref, v_ref, g_ref, gin_ref, wf_ref, bf_ref, wb_ref, bb_ref, gain_ref, o_ref,
                bf_s, bb_s, s_all, stf, stb, *, seq):
    c_len = C_CHUNK
    nc = seq // c_len
    scale = C_DK ** -0.5
    row = lax.broadcasted_iota(jnp.int32, (c_len, C_DK), 0)
    shifts = [1 << s for s in range(int(math.log2(c_len)))]

    def rows(c):
        return pl.ds(pl.multiple_of(c * c_len, c_len), c_len)

    gin = gin_ref[...]
    bf_s[...] = _log_sigmoid(jnp.dot(gin, wf_ref[...], preferred_element_type=F32) + bf_ref[...]) / C_GATE_NORM
    bb_s[...] = _log_sigmoid(jnp.dot(gin, wb_ref[...], preferred_element_type=F32) + bb_ref[...]) / C_GATE_NORM

    def cum_body(c, carry):
        y = bf_s[rows(c), :]
        for s in shifts:
            y = y + jnp.where(row >= s, pltpu.roll(y, s, 0), 0.0)
        bf_s[rows(c), :] = y
        y = bb_s[rows(c), :]
        for s in shifts:
            y = y + jnp.where(row < c_len - s, pltpu.roll(y, c_len - s, 0), 0.0)
        bb_s[rows(c), :] = y
        return carry

    lax.fori_loop(0, nc, cum_body, 0)

    stf[...] = jnp.zeros_like(stf)
    stb[...] = jnp.zeros_like(stb)

    def state_step(c, b_s, last_row, st, lane0):
        k = k_ref[rows(c), :].astype(F32)
        bc = b_s[rows(c), :]
        bl = bc[last_row:last_row + 1, :]
        kd = (k * jnp.exp(bl - bc)).astype(BF16)
        kv_t = lax.dot_general(v_ref[rows(c), :], kd, TN_DIMS, preferred_element_type=F32)
        s_all[c, :, lane0:lane0 + C_DK] = st[...].astype(BF16)
        st[...] = st[...] * jnp.exp(bl) + kv_t

    def state_body(i, carry):
        state_step(i, bf_s, c_len - 1, stf, 0)
        state_step(nc - 1 - i, bb_s, 0, stb, C_DK)
        return carry

    lax.fori_loop(0, nc, state_body, 0, unroll=2)

    ri = lax.broadcasted_iota(jnp.int32, (c_len, c_len), 0)
    ci = lax.broadcasted_iota(jnp.int32, (c_len, c_len), 1)

    def out_body(c, carry):
        q = q_ref[rows(c), :].astype(F32) * scale
        k = k_ref[rows(c), :].astype(F32)
        bcf = bf_s[rows(c), :]
        bcb = bb_s[rows(c), :]
        qf = (q * jnp.exp(bcf)).astype(BF16)
        kf = (k * jnp.exp(-bcf)).astype(BF16)
        qb = (q * jnp.exp(bcb)).astype(BF16)
        kb = (k * jnp.exp(-bcb)).astype(BF16)
        att_f = lax.dot_general(qf, kf, NT_DIMS, preferred_element_type=F32)
        att_b = lax.dot_general(qb, kb, NT_DIMS, preferred_element_type=F32)
        att = jnp.where(ri >= ci, att_f, att_b).astype(BF16)
        q_both = jnp.concatenate([qf, qb], axis=1)
        o = (jnp.dot(att, v_ref[rows(c), :], preferred_element_type=F32)
             + lax.dot_general(q_both, s_all[c], NT_DIMS, preferred_element_type=F32))
        g = g_ref[rows(c), :].astype(F32)
        o_ref[rows(c), :] = (_rms(o, gain_ref[...]) * (g * _sigmoid(g))).astype(BF16)
        return carry

    lax.fori_loop(0, nc, out_body, 0, unroll=4)


def _gla(proj3, gin3, wf, b_f, wb, b_b, out_norm):
    b, seq, _ = proj3.shape
    kw = C_HEADS * C_DK
    k_blk = kw // C_DK
    v_blk = 2 * kw // C_DV
    g_blk = v_blk + C_HEADS
    nc = seq // C_CHUNK
    return pl.pallas_call(
        functools.partial(_gla_kernel, seq=seq),
        grid=(b, C_HEADS),
        in_specs=[
            pl.BlockSpec((None, seq, C_DK), lambda i, h: (i, 0, h)),
            pl.BlockSpec((None, seq, C_DK), lambda i, h: (i, 0, k_blk + h)),
            pl.BlockSpec((None, seq, C_DV), lambda i, h: (i, 0, v_blk + h)),
            pl.BlockSpec((None, seq, C_DV), lambda i, h: (i, 0, g_blk + h)),
            pl.BlockSpec((None, seq, LANES), lambda i, h: (i, 0, 0)),
            pl.BlockSpec((LANES, C_DK), lambda i, h: (0, h)),
            pl.BlockSpec((1, C_DK), lambda i, h: (0, h)),
            pl.BlockSpec((LANES, C_DK), lambda i, h: (0, h)),
            pl.BlockSpec((1, C_DK), lambda i, h: (0, h)),
            pl.BlockSpec((1, C_DV), lambda i, h: (0, 0)),
        ],
        out_specs=pl.BlockSpec((None, seq, C_DV), lambda i, h: (i, 0, h)),
        out_shape=jax.ShapeDtypeStruct((b, seq, C_HEADS * C_DV), BF16),
        scratch_shapes=[
            pltpu.VMEM((seq, C_DK), F32),
            pltpu.VMEM((seq, C_DK), F32),
            pltpu.VMEM((nc, C_DV, 2 * C_DK), BF16),
            pltpu.VMEM((C_DV, C_DK), F32),
            pltpu.VMEM((C_DV, C_DK), F32),
        ],
        compiler_params=_cparams(("parallel", "parallel")),
        name="gla",
    )(proj3, proj3, proj3, proj3, gin3, wf, b_f, wb, b_b, out_norm.reshape(1, C_DV))


def _final_kernel(x_ref, y0_ref, y1_ref, gain_ref, o_ref):
    tm = x_ref.shape[0]
    x = x_ref[...] + _load_token_tiles(y0_ref, tm) + _load_token_tiles(y1_ref, tm)
    o_ref[...] = _rms(x, gain_ref[...])


def _final_norm(n, x, y, gain):
    tm = ROW_TILE
    return pl.pallas_call(
        _final_kernel,
        grid=(n // tm,),
        in_specs=[_row_spec(tm, D_MODEL), _row_spec(tm * FEAT_ROWS, LANES),
                  _row_spec(tm * FEAT_ROWS, LANES, n // tm), _full_spec((1, D_MODEL))],
        out_specs=_row_spec(tm, D_MODEL),
        out_shape=jax.ShapeDtypeStruct((n, D_MODEL), F32),
        compiler_params=_cparams(("parallel",)),
        name="final_norm",
    )(x, y, y, gain.reshape(1, D_MODEL))


def _moe_layer(n, hn, ids_pad, wts_pad, w_gate, w_up, w_down, layer):
    plan = _moe_plan(ids_pad[:2], wts_pad[:2], n)
    return _moe_experts(n, hn, plan, w_gate, w_up, w_down, layer)


def kernel(x, t5_bias, norm_mix, norm_ffn, norm_final, ev_w_in, ev_lambda, ev_subln, ev_rpb, ev_w_out,
           od_w_in, od_w_gk_fwd, od_b_gk_fwd, od_w_gk_bwd, od_b_gk_bwd, od_out_norm, od_w_out,
           moe_w_grp, moe_b_grp, moe_w_exp, moe_b_exp, moe_w_gate, moe_w_up, moe_w_down):
    b, seq, d = x.shape
    n = b * seq
    rows = seq // GRID_W
    assert d == D_MODEL and n % ROW_TILE == 0 and seq % ATT_TQ == 0 and seq % C_CHUNK == 0
    assert rows % NA_QROWS == 0 and rows >= NA_KROWS and MOE_TILE <= ROW_TILE
    x2 = x.reshape(n, d)

    (proj,) = _in_proj_call(n, [(x2, 0)], norm_mix[0], ev_w_in[0].astype(BF16))
    proj3 = proj.reshape(b, seq, proj.shape[1])
    lam_init = 0.8 - 0.6 * math.exp(-0.3 * 0)
    lp = ev_lambda[0].astype(F32)
    lam = (jnp.exp(jnp.sum(lp[0] * lp[1])) - jnp.exp(jnp.sum(lp[2] * lp[3])) + lam_init).reshape(1)
    strip = _t5_strip(t5_bias, seq, ATT_TQ)
    a_diff = _diff_attn(proj3, lam, strip, ev_subln[0], 1.0 - lam_init)

    starts, classes, _ = _na_block_geometry(rows)
    table = _na_table(ev_rpb[0], rows)
    a_na = _na_attn(proj3, table, starts, classes)

    w_out = ev_w_out[0].astype(BF16)
    w_r, b_r = _router_params(moe_w_grp[0], moe_b_grp[0], moe_w_exp[0], moe_b_exp[0])
    x1, hn, ids, wts = _out_proj_router(
        n, x2, [a_diff.reshape(n, A_W), a_na.reshape(n, B_W)], [w_out[:A_W], w_out[A_W:]],
        norm_ffn[0], w_r, b_r)
    y = _moe_layer(n, hn, ids, wts, moe_w_gate, moe_w_up, moe_w_down, 0)

    kw = C_HEADS * C_DK
    main_w = 2 * kw + 2 * C_HEADS * C_DV
    w_in = od_w_in[0]
    w_gates = jnp.pad(w_in[:, main_w:], ((0, 0), (0, LANES - 2 * C_GATE_RANK))).astype(BF16)
    wf = jnp.pad(od_w_gk_fwd[0], ((0, LANES - C_GATE_RANK), (0, 0))).astype(BF16)
    wb = jnp.pad(od_w_gk_bwd[0], ((C_GATE_RANK, LANES - 2 * C_GATE_RANK), (0, 0))).astype(BF16)
    x2b, proj, gin = _in_proj_call(
        n, [(x1, 0), (y, 0), (y, n)], norm_mix[1], w_in[:, :main_w].astype(BF16), gates=w_gates)
    a_gla = _gla(proj.reshape(b, seq, main_w), gin.reshape(b, seq, LANES), wf, od_b_gk_fwd[0].reshape(1, kw),
                 wb, od_b_gk_bwd[0].reshape(1, kw), od_out_norm[0])

    w_r, b_r = _router_params(moe_w_grp[1], moe_b_grp[1], moe_w_exp[1], moe_b_exp[1])
    x3, hn, ids, wts = _out_proj_router(
        n, x2b, [a_gla.reshape(n, C_HEADS * C_DV)], [od_w_out[0].astype(BF16)], norm_ffn[1], w_r, b_r)
    y = _moe_layer(n, hn, ids, wts, moe_w_gate, moe_w_up, moe_w_down, 1)

    return _final_norm(n, x3, y, norm_final).reshape(b, seq, d)


def _na_table(rpb, rows):
    kh = min(NA_MAX_ROWS, rows)
    starts, classes, reps = _na_block_geometry(rows)
    qn, kn = NA_QROWS * GRID_W, NA_KROWS * GRID_W
    q_row, q_col = jnp.arange(qn) // GRID_W, jnp.arange(qn) % GRID_W
    k_row, k_col = jnp.arange(kn) // GRID_W, jnp.arange(kn) % GRID_W
    col_start = jnp.clip(q_col - NA_COLS // 2, 0, GRID_W - NA_COLS)
    col_ok = (k_col[None, :] >= col_start[:, None]) & (k_col[None, :] < col_start[:, None] + NA_COLS)
    side = GRID_W - NA_COLS
    col_bias = _toeplitz(jnp.pad(rpb.astype(F32), ((0, 0), (0, 0), (side, side))), GRID_W, GRID_W)
    per_class = []
    for cls in range(len(reps)):
        j = classes.index(cls)
        q_abs = j * NA_QROWS + q_row
        k_abs = starts[j] + k_row
        row_start = jnp.clip(q_abs - kh // 2, 0, rows - kh)
        row_ok = (k_abs[None, :] >= row_start[:, None]) & (k_abs[None, :] < row_start[:, None] + kh)
        dr = jnp.clip(starts[j] + jnp.arange(NA_KROWS)[None, :] - (j * NA_QROWS + jnp.arange(NA_QROWS))[:, None]
                      + NA_MAX_ROWS - 1, 0, 2 * NA_MAX_ROWS - 2)
        blocks = col_bias[:, dr]
        bias = jnp.transpose(blocks, (0, 1, 3, 2, 4)).reshape(B_HEADS, qn, kn)
        per_class.append(jnp.where((row_ok & col_ok)[None], bias, -jnp.inf))
    t = jnp.stack(per_class, axis=1)
    return jnp.transpose(t.reshape(B_HEADS // 2, 2, len(reps), qn, kn), (0, 2, 1, 3, 4))
```

```python
import functools
import math

import jax
import jax.numpy as jnp
from jax import lax
from jax.experimental import pallas as pl
from jax.experimental.pallas import tpu as pltpu

D_MODEL = 1024
GRID_W = 64
A_HEADS = 4
A_DH = 64
B_HEADS = 8
B_DH = 64
NA_MAX_ROWS = 8
NA_COLS = 16
T5_BUCKETS = 32
T5_MAX_DIST = 128
C_HEADS = 4
C_DK = 128
C_DV = 256
C_GATE_RANK = 16
C_GATE_NORM = 16.0
C_CHUNK = 64
N_GROUPS = 4
EXP_PER_GROUP = 8
N_EXPERTS = N_GROUPS * EXP_PER_GROUP
D_EXPERT = 512
EPS = 1e-6

A_W = A_HEADS * 2 * A_DH
B_W = B_HEADS * B_DH
LANES = 128
FEAT_ROWS = D_MODEL // LANES
VMEM_LIMIT = 56 * 1024 * 1024

ROW_TILE = 512
ATT_TQ = 256
NA_QROWS = 4
NA_KROWS = 12
MOE_TILE = 256
ROUTE_ROWS = 8

F32 = jnp.float32
BF16 = jnp.bfloat16
NT_DIMS = (((1,), (1,)), ((), ()))
TN_DIMS = (((0,), (0,)), ((), ()))


def _cparams(sem):
    return pltpu.CompilerParams(dimension_semantics=sem, vmem_limit_bytes=VMEM_LIMIT)


def _load_token_tiles(ref, tm, lead=()):
    return jnp.concatenate(
        [ref[lead + (pl.ds(j, tm, stride=FEAT_ROWS), slice(None))] for j in range(FEAT_ROWS)], axis=1)


def _store_token_tiles(ref, val, lead=()):
    tm = val.shape[0]
    for j in range(FEAT_ROWS):
        ref[lead + (pl.ds(j, tm, stride=FEAT_ROWS), slice(None))] = val[:, j * LANES:(j + 1) * LANES]


def _rms(x, gain):
    return x * lax.rsqrt(jnp.mean(x * x, axis=-1, keepdims=True) + EPS) * gain


def _sigmoid(x):
    return 1.0 / (1.0 + jnp.exp(-x))


def _log_sigmoid(z):
    return jnp.minimum(z, 0.0) - jnp.log(1.0 + jnp.exp(-jnp.abs(z)))


def _in_proj_kernel(*refs, n_parts, gates):
    parts = refs[:n_parts]
    gain_ref, w_ref = refs[n_parts], refs[n_parts + 1]
    pos = n_parts + 2
    if gates:
        wg_ref = refs[pos]
        pos += 1
    outs = refs[pos:]
    x = parts[0][...]
    for p in parts[1:]:
        x = x + _load_token_tiles(p, x.shape[0])
    oi = 0
    if n_parts > 1:
        outs[0][...] = x
        oi = 1
    h = _rms(x, gain_ref[...]).astype(BF16)
    proj_ref = outs[oi]
    n_out = proj_ref.shape[1]
    step = 512
    for j in range(n_out // step):
        proj_ref[:, j * step:(j + 1) * step] = jnp.dot(
            h, w_ref[:, j * step:(j + 1) * step], preferred_element_type=F32).astype(BF16)
    if gates:
        outs[oi + 1][...] = jnp.dot(h, wg_ref[...], preferred_element_type=F32).astype(BF16)


def _row_spec(tm, width, block_off=0):
    return pl.BlockSpec((tm, width), lambda i, o=block_off: (i + o, 0))


def _full_spec(shape):
    nd = len(shape)
    return pl.BlockSpec(shape, lambda i, _nd=nd: (0,) * _nd)


def _in_proj_call(n, parts, gain, w, gates=None):
    tm = ROW_TILE
    n_parts = len(parts)
    n_out = w.shape[1]
    in_specs = [_row_spec(tm, D_MODEL)] + [_row_spec(tm * FEAT_ROWS, LANES, off // tm) for _, off in parts[1:]]
    args = [a for a, _ in parts]
    in_specs += [_full_spec((1, D_MODEL)), _full_spec(w.shape)]
    args += [gain.reshape(1, D_MODEL), w]
    out_shape, out_specs = [], []
    if n_parts > 1:
        out_shape.append(jax.ShapeDtypeStruct((n, D_MODEL), F32))
        out_specs.append(_row_spec(tm, D_MODEL))
    out_shape.append(jax.ShapeDtypeStruct((n, n_out), BF16))
    out_specs.append(_row_spec(tm, n_out))
    if gates is not None:
        in_specs.append(_full_spec(gates.shape))
        args.append(gates)
        out_shape.append(jax.ShapeDtypeStruct((n, LANES), BF16))
        out_specs.append(_row_spec(tm, LANES))
    return pl.pallas_call(
        functools.partial(_in_proj_kernel, n_parts=n_parts, gates=gates is not None),
        grid=(n // tm,),
        in_specs=in_specs,
        out_specs=out_specs,
        out_shape=out_shape,
        compiler_params=_cparams(("parallel",)),
        name="in_proj_gla" if gates is not None else "in_proj_attn",
    )(*args)


def _diff_attn_kernel(lam_ref, q_ref, k_ref, v_ref, bias_ref, gain_ref, o_ref, *, seq, out_scale):
    tq = q_ref.shape[0]
    qb = pl.program_id(2)
    q = q_ref[...]
    k = k_ref[...]
    lane = lax.broadcasted_iota(jnp.int32, q.shape, 1)
    scale = A_DH ** -0.5
    zero = jnp.zeros_like(q)
    off = pl.multiple_of((seq - tq) - qb * tq, LANES)
    bias = bias_ref[:, pl.ds(off, seq)]

    def exp_scores(qm):
        s = lax.dot_general(qm * scale, k, NT_DIMS, preferred_element_type=F32) + bias
        e = jnp.exp(s - jnp.max(s, axis=-1, keepdims=True))
        return e, jnp.sum(e, axis=-1, keepdims=True)

    e0, l0 = exp_scores(jnp.where(lane < A_DH, q, zero))
    e1, l1 = exp_scores(jnp.where(lane >= A_DH, q, zero))
    attn = (e0 - (lam_ref[0] * l0 / l1) * e1).astype(BF16)
    o = jnp.dot(attn, v_ref[...], preferred_element_type=F32) / l0
    o_ref[...] = (_rms(o, gain_ref[...]) * out_scale).astype(BF16)


def _t5_bucket(rel):
    half = T5_BUCKETS // 2
    max_exact = half // 2
    sign_off = jnp.where(rel > 0, half, 0)
    n = jnp.abs(rel)
    nf = jnp.maximum(n, 1).astype(F32)
    large = max_exact + (jnp.log(nf / max_exact) / math.log(T5_MAX_DIST / max_exact)
                         * (half - max_exact)).astype(jnp.int32)
    large = jnp.minimum(large, half - 1)
    return sign_off + jnp.where(n < max_exact, n, large)


def _t5_strip(t5_bias, seq, tq):
    n_rel = 2 * seq - 1
    vec = t5_bias[_t5_bucket(jnp.arange(n_rel) - (seq - 1))].astype(F32).T
    return _toeplitz(vec, tq, 2 * seq - tq)


def _toeplitz(vec, n_rows, n_cols):
    n = vec.shape[-1]
    lead = vec.shape[:-1]
    assert n_cols + n_rows - 1 <= n and n_cols <= n - 1
    rolled = jnp.roll(vec, -(n_rows - 1), axis=-1)
    reps = -(-(n_rows * (n - 1)) // n)
    flat = jnp.broadcast_to(rolled[..., None, :], lead + (reps, n)).reshape(lead + (reps * n,))
    return flat[..., :n_rows * (n - 1)].reshape(lead + (n_rows, n - 1))[..., :n_cols]


def _diff_attn(proj3, lam, strip, subln, out_scale):
    b, seq, _ = proj3.shape
    tq = ATT_TQ
    kblk, vblk = A_W // LANES, 2 * A_W // LANES
    return pl.pallas_call(
        functools.partial(_diff_attn_kernel, seq=seq, out_scale=out_scale),
        grid=(A_HEADS, b, seq // tq),
        in_specs=[
            pl.BlockSpec(memory_space=pltpu.SMEM),
            pl.BlockSpec((None, tq, LANES), lambda h, i, j: (i, j, h)),
            pl.BlockSpec((None, seq, LANES), lambda h, i, j: (i, 0, kblk + h)),
            pl.BlockSpec((None, seq, LANES), lambda h, i, j: (i, 0, vblk + h)),
            pl.BlockSpec((None, tq, 2 * seq - tq), lambda h, i, j: (h, 0, 0)),
            pl.BlockSpec((1, LANES), lambda h, i, j: (0, 0)),
        ],
        out_specs=pl.BlockSpec((None, tq, LANES), lambda h, i, j: (i, j, h)),
        out_shape=jax.ShapeDtypeStruct((b, seq, A_W), BF16),
        compiler_params=_cparams(("parallel", "parallel", "parallel")),
        name="diff_attn",
    )(lam, proj3, proj3, proj3, strip, subln.reshape(1, LANES))


def _na_block_geometry(rows):
    nblk = rows // NA_QROWS
    kh = min(NA_MAX_ROWS, rows)
    starts, classes, reps = [], [], []
    for j in range(nblk):
        ks = min(max(j * NA_QROWS - kh // 2, 0), rows - NA_KROWS)
        rel = (j * NA_QROWS - ks,) + tuple(
            min(max(r - kh // 2, 0), rows - kh) - ks for r in range(j * NA_QROWS, (j + 1) * NA_QROWS))
        starts.append(ks)
        if rel not in reps:
            reps.append(rel)
        classes.append(reps.index(rel))
    return starts, classes, reps


def _na_kernel(q_ref, k_ref, v_ref, tab_ref, o_ref, *, starts, classes):
    qn, kn = NA_QROWS * GRID_W, NA_KROWS * GRID_W
    scale = B_DH ** -0.5
    lane = lax.broadcasted_iota(jnp.int32, (qn, LANES), 1)
    for j, (ks, cls) in enumerate(zip(starts, classes)):
        q = q_ref[j * qn:(j + 1) * qn, :]
        kw = k_ref[ks * GRID_W:ks * GRID_W + kn, :]
        vw = v_ref[ks * GRID_W:ks * GRID_W + kn, :]
        zero = jnp.zeros_like(q)
        outs = []
        for hl in range(2):
            in_head = (lane >= hl * B_DH) & (lane < (hl + 1) * B_DH)
            qm = jnp.where(in_head, q, zero) * scale
            s = lax.dot_general(qm, kw, NT_DIMS, preferred_element_type=F32) + tab_ref[cls, hl]
            m = jnp.max(s, axis=-1, keepdims=True)
            p = jnp.exp(s - m)
            denom = jnp.sum(p, axis=-1, keepdims=True)
            outs.append(jnp.dot(p.astype(BF16), vw, preferred_element_type=F32) / denom)
        o_ref[j * qn:(j + 1) * qn, :] = jnp.where(lane < B_DH, outs[0], outs[1]).astype(BF16)


def _na_attn(proj3, table, starts, classes):
    b, seq, _ = proj3.shape
    qblk = 3 * A_W // LANES
    kblk = qblk + B_W // LANES
    vblk = kblk + B_W // LANES
    ncls = table.shape[1]
    qn, kn = NA_QROWS * GRID_W, NA_KROWS * GRID_W
    return pl.pallas_call(
        functools.partial(_na_kernel, starts=tuple(starts), classes=tuple(classes)),
        grid=(B_HEADS // 2, b),
        in_specs=[
            pl.BlockSpec((None, seq, LANES), lambda h, i: (i, 0, qblk + h)),
            pl.BlockSpec((None, seq, LANES), lambda h, i: (i, 0, kblk + h)),
            pl.BlockSpec((None, seq, LANES), lambda h, i: (i, 0, vblk + h)),
            pl.BlockSpec((None, ncls, 2, qn, kn), lambda h, i: (h, 0, 0, 0, 0)),
        ],
        out_specs=pl.BlockSpec((None, seq, LANES), lambda h, i: (i, 0, h)),
        out_shape=jax.ShapeDtypeStruct((b, seq, B_W), BF16),
        compiler_params=_cparams(("parallel", "parallel")),
        name="na_attn",
    )(proj3, proj3, proj3, table)


def _out_proj_router_kernel(*refs, n_a):
    x_ref = refs[0]
    a_refs = refs[1:1 + n_a]
    w_refs = refs[1 + n_a:1 + 2 * n_a]
    gain_ref, wr_ref, br_ref, xo_ref, hn_ref, ids_ref, wts_ref = refs[1 + 2 * n_a:]
    acc = x_ref[...]
    for a, w in zip(a_refs, w_refs):
        acc = acc + jnp.dot(a[...], w[...], preferred_element_type=F32)
    xo_ref[...] = acc
    h = _rms(acc, gain_ref[...])
    _store_token_tiles(hn_ref, h)
    h_hi = h.astype(BF16)
    h_lo = (h - h_hi.astype(F32)).astype(BF16)
    hi = jnp.dot(h_hi, wr_ref[...], preferred_element_type=F32)
    logits = (hi[:, :LANES] + hi[:, LANES:]
              + jnp.dot(h_lo, wr_ref[:, :LANES], preferred_element_type=F32) + br_ref[...])
    lane = lax.broadcasted_iota(jnp.int32, logits.shape, 1)
    neg = jnp.float32(-jnp.inf)
    big = jnp.int32(LANES)

    def masked_softmax(mask):
        z = jnp.where(mask, logits, neg)
        e = jnp.exp(z - jnp.max(z, axis=-1, keepdims=True))
        return e / jnp.sum(e, axis=-1, keepdims=True)

    def top1(p, mask):
        w = jnp.max(jnp.where(mask, p, -1.0), axis=-1, keepdims=True)
        idx = jnp.min(jnp.where(mask & (p == w), lane, big), axis=-1, keepdims=True)
        return w, idx

    is_grp = lane < N_GROUPS
    g_w, g_idx = top1(masked_softmax(is_grp), is_grp)
    e_lane = lane - N_GROUPS
    in_grp = (e_lane >= g_idx * EXP_PER_GROUP) & (e_lane < (g_idx + 1) * EXP_PER_GROUP)
    p_e = masked_softmax(in_grp)
    w1, i1 = top1(p_e, in_grp)
    rest = in_grp & (lane != i1)
    w2, i2 = top1(p_e, rest)
    denom = w1 + w2
    ids = jnp.where(lane == 0, i1 - N_GROUPS, i2 - N_GROUPS)
    wts = jnp.where(lane == 0, g_w * (w1 / denom), g_w * (w2 / denom))
    ids_ref[...] = ids.T[:ids_ref.shape[0]]
    wts_ref[...] = wts.T[:wts_ref.shape[0]]


def _out_proj_router(n, x, acts, ws, gain, w_router, b_router):
    tm = ROW_TILE
    n_a = len(acts)
    in_specs = [_row_spec(tm, D_MODEL)]
    in_specs += [_row_spec(tm, a.shape[1]) for a in acts]
    in_specs += [_full_spec(w.shape) for w in ws]
    in_specs += [_full_spec((1, D_MODEL)), _full_spec(w_router.shape), _full_spec((1, LANES))]
    return pl.pallas_call(
        functools.partial(_out_proj_router_kernel, n_a=n_a),
        grid=(n // tm,),
        in_specs=in_specs,
        out_specs=[_row_spec(tm, D_MODEL), _row_spec(tm * FEAT_ROWS, LANES),
                   pl.BlockSpec((ROUTE_ROWS, tm), lambda i: (0, i)), pl.BlockSpec((ROUTE_ROWS, tm), lambda i: (0, i))],
        out_shape=[jax.ShapeDtypeStruct((n, D_MODEL), F32), jax.ShapeDtypeStruct((n * FEAT_ROWS, LANES), F32),
                   jax.ShapeDtypeStruct((ROUTE_ROWS, n), jnp.int32), jax.ShapeDtypeStruct((ROUTE_ROWS, n), F32)],
        compiler_params=_cparams(("parallel",)),
        name="out_proj_router",
    )(x, *acts, *ws, gain.reshape(1, D_MODEL), w_router, b_router)


def _router_params(w_grp, b_grp, w_exp, b_exp):
    pad = LANES - N_GROUPS - N_EXPERTS
    w = jnp.concatenate([w_grp, w_exp, jnp.zeros((D_MODEL, pad), F32)], axis=1)
    b = jnp.concatenate([b_grp, b_exp, jnp.zeros((pad,), F32)]).reshape(1, LANES)
    w_hi = w.astype(BF16)
    w_lo = (w - w_hi.astype(F32)).astype(BF16)
    return jnp.concatenate([w_hi, w_lo], axis=1), b


def _moe_plan(ids, wts, n):
    tm = MOE_TILE
    n_assign = 2 * n
    p_rows = n_assign + N_EXPERTS * tm
    e_flat = ids.reshape(-1)
    w_flat = wts.reshape(-1)
    idx_bits = max(n_assign - 1, 1).bit_length()
    assert idx_bits + N_EXPERTS.bit_length() <= 31
    key = jnp.sort((e_flat << idx_bits) | jnp.arange(n_assign, dtype=jnp.int32))
    order = key & ((1 << idx_bits) - 1)
    experts = jnp.arange(N_EXPERTS, dtype=jnp.int32)
    counts = jnp.sum((key >> idx_bits)[None, :] == experts[:, None], axis=1, dtype=jnp.int32)
    off = jnp.cumsum(counts) - counts
    padded = (counts + tm - 1) // tm * tm
    p_end = jnp.cumsum(padded)
    p_off = p_end - padded
    nt = p_rows // tm
    tile_start = jnp.arange(nt, dtype=jnp.int32) * tm
    tile_expert = jnp.minimum(jnp.sum(p_end[None, :] <= tile_start[:, None], axis=1, dtype=jnp.int32),
                              N_EXPERTS - 1)
    tile_count = jnp.clip(counts[tile_expert] - (tile_start - p_off[tile_expert]), 0, tm).astype(jnp.int32)
    n_used = (p_end[-1] // tm).astype(jnp.int32).reshape(1)
    in_tile = jnp.arange(tm, dtype=jnp.int32)[None, :]
    real = in_tile < tile_count[:, None]
    s_idx = jnp.clip((off[tile_expert] + tile_start - p_off[tile_expert])[:, None] + in_tile, 0, n_assign - 1)
    assign = jnp.where(real, order[s_idx], 0)
    src = (assign % n).reshape(p_rows)
    dest = assign.reshape(p_rows)
    w_row = jnp.where(real, w_flat[assign], 0.0).reshape(p_rows)
    return (src.reshape(nt, 1, tm), dest.reshape(nt, 1, tm), w_row.reshape(p_rows, 1), tile_expert,
            tile_count, n_used)


def _moe_kernel(te_ref, tc_ref, nu_ref, src_ref, dst_ref, wrow_ref, x_hbm, wg_ref, wu_ref, wd_ref, y_hbm,
                xbuf, ybuf, wgb, wub, wdb, gsem, ssem):
    tm = MOE_TILE
    i = pl.program_id(0)
    n_used = nu_ref[0]
    last = pl.num_programs(0) - 1

    def tile_rows(r):
        return pl.ds(pl.multiple_of(r * FEAT_ROWS, FEAT_ROWS), FEAT_ROWS)

    def gather_copy(slot, r, tok):
        return pltpu.make_async_copy(x_hbm.at[tile_rows(tok)], xbuf.at[slot, tile_rows(r)], gsem.at[slot])

    def scatter_copy(slot, r, row):
        return pltpu.make_async_copy(ybuf.at[slot, tile_rows(r)], y_hbm.at[tile_rows(row)], ssem.at[slot])

    def start_all(make):
        def body(r2, c):
            make(2 * r2).start(priority=0)
            make(2 * r2 + 1).start(priority=1)
            return c
        lax.fori_loop(0, tm // 2, body, 0, unroll=4)

    def start_rows(count, make):
        @pl.when(count == tm)
        def _():
            start_all(make)

        @pl.when(count < tm)
        def _():
            lax.fori_loop(0, count, lambda r, c: (make(r).start(), c)[1], 0)

    def wait_scatter(t):
        slot = t % 2
        count = tc_ref[t]

        @pl.when(count == tm)
        def _():
            pltpu.make_async_copy(ybuf.at[slot], y_hbm.at[pl.ds(0, tm * FEAT_ROWS)], ssem.at[slot]).wait()

        @pl.when(count < tm)
        def _():
            lax.fori_loop(0, count, lambda r, c: (scatter_copy(slot, r, 0).wait(), c)[1], 0)

    @pl.when(i < n_used)
    def _():
        slot = i % 2
        start_all(lambda r: gather_copy(slot, r, src_ref[0, r]))

    @pl.when((i >= 1) & (i <= n_used))
    def _():
        t = i - 1
        slot = t % 2
        pltpu.make_async_copy(x_hbm.at[pl.ds(0, tm * FEAT_ROWS)], xbuf.at[slot], gsem.at[slot]).wait()

        @pl.when(t >= 2)
        def _():
            wait_scatter(t - 2)

        @pl.when(jnp.logical_or(t == 0, te_ref[t] != te_ref[jnp.maximum(t - 1, 0)]))
        def _():
            wgb[...] = wg_ref[...].astype(BF16)
            wub[...] = wu_ref[...].astype(BF16)
            wdb[...] = wd_ref[...].astype(BF16)

        x = _load_token_tiles(xbuf, tm, lead=(slot,)).astype(BF16)
        g = jnp.dot(x, wgb[...], preferred_element_type=F32)
        u = jnp.dot(x, wub[...], preferred_element_type=F32)
        hid = (g * _sigmoid(g) * u).astype(BF16)
        y = jnp.dot(hid, wdb[...], preferred_element_type=F32)
        _store_token_tiles(ybuf, y * wrow_ref[...], lead=(slot,))

        start_rows(tc_ref[t], lambda r: scatter_copy(slot, r, dst_ref[0, r]))

    @pl.when(i == last)
    def _():
        wait_scatter(n_used - 1)

        @pl.when(n_used >= 2)
        def _():
            wait_scatter(n_used - 2)


def _moe_experts(n, hn, plan, w_gate, w_up, w_down, layer):
    tm = MOE_TILE
    src, dest, w_row, tile_expert, tile_count, n_used = plan
    nt = src.shape[0]
    prev = lambda i: jnp.maximum(i - 1, 0)
    grid_spec = pltpu.PrefetchScalarGridSpec(
        num_scalar_prefetch=3,
        grid=(nt + 1,),
        in_specs=[
            pl.BlockSpec((None, 1, tm), lambda i, te, tc, nu: (jnp.minimum(i, nt - 1), 0, 0), memory_space=pltpu.SMEM),
            pl.BlockSpec((None, 1, tm), lambda i, te, tc, nu: (prev(i), 0, 0), memory_space=pltpu.SMEM),
            pl.BlockSpec((tm, 1), lambda i, te, tc, nu: (prev(i), 0)),
            pl.BlockSpec(memory_space=pl.ANY),
            pl.BlockSpec((None, None, D_MODEL, D_EXPERT), lambda i, te, tc, nu: (layer, te[prev(i)], 0, 0)),
            pl.BlockSpec((None, None, D_MODEL, D_EXPERT), lambda i, te, tc, nu: (layer, te[prev(i)], 0, 0)),
            pl.BlockSpec((None, None, D_EXPERT, D_MODEL), lambda i, te, tc, nu: (layer, te[prev(i)], 0, 0)),
        ],
        out_specs=pl.BlockSpec(memory_space=pl.ANY),
        scratch_shapes=[
            pltpu.VMEM((2, tm * FEAT_ROWS, LANES), F32),
            pltpu.VMEM((2, tm * FEAT_ROWS, LANES), F32),
            pltpu.VMEM((D_MODEL, D_EXPERT), BF16),
            pltpu.VMEM((D_MODEL, D_EXPERT), BF16),
            pltpu.VMEM((D_EXPERT, D_MODEL), BF16),
            pltpu.SemaphoreType.DMA((2,)),
            pltpu.SemaphoreType.DMA((2,)),
        ],
    )
    return pl.pallas_call(
        _moe_kernel,
        grid_spec=grid_spec,
        out_shape=jax.ShapeDtypeStruct((2 * n * FEAT_ROWS, LANES), F32),
        compiler_params=_cparams(("arbitrary",)),
        name="moe_experts",
    )(tile_expert, tile_count, n_used, src, dest, w_row, hn, w_gate, w_up, w_down)


def _gla_kernel(q_ref, k_ref, v_ref, g_ref, gin_ref, wf_ref, bf_ref, wb_ref, bb_ref, gain_ref, o_ref,
                bf_s, bb_s, s_all, stf, stb, *, seq):
    c_len = C_CHUNK
    nc = seq // c_len
    scale = C_DK ** -0.5
    row = lax.broadcasted_iota(jnp.int32, (c_len, C_DK), 0)
    shifts = [1 << s for s in range(int(math.log2(c_len)))]

    def rows(c):
        return pl.ds(pl.multiple_of(c * c_len, c_len), c_len)

    gin = gin_ref[...]
    bf_s[...] = _log_sigmoid(jnp.dot(gin, wf_ref[...], preferred_element_type=F32) + bf_ref[...]) / C_GATE_NORM
    bb_s[...] = _log_sigmoid(jnp.dot(gin, wb_ref[...], preferred_element_type=F32) + bb_ref[...]) / C_GATE_NORM

    def cum_body(c, carry):
        y = bf_s[rows(c), :]
        for s in shifts:
            y = y + jnp.where(row >= s, pltpu.roll(y, s, 0), 0.0)
        bf_s[rows(c), :] = y
        y = bb_s[rows(c), :]
        for s in shifts:
            y = y + jnp.where(row < c_len - s, pltpu.roll(y, c_len - s, 0), 0.0)
        bb_s[rows(c), :] = y
        return carry

    lax.fori_loop(0, nc, cum_body, 0)

    stf[...] = jnp.zeros_like(stf)
    stb[...] = jnp.zeros_like(stb)

    def state_step(c, b_s, last_row, st, lane0):
        k = k_ref[rows(c), :].astype(F32)
        bc = b_s[rows(c), :]
        bl = bc[last_row:last_row + 1, :]
        kd = (k * jnp.exp(bl - bc)).astype(BF16)
        kv_t = lax.dot_general(v_ref[rows(c), :], kd, TN_DIMS, preferred_element_type=F32)
        s_all[c, :, lane0:lane0 + C_DK] = st[...].astype(BF16)
        st[...] = st[...] * jnp.exp(bl) + kv_t

    def state_body(i, carry):
        state_step(i, bf_s, c_len - 1, stf, 0)
        state_step(nc - 1 - i, bb_s, 0, stb, C_DK)
        return carry

    lax.fori_loop(0, nc, state_body, 0, unroll=4)

    ri = lax.broadcasted_iota(jnp.int32, (c_len, c_len), 0)
    ci = lax.broadcasted_iota(jnp.int32, (c_len, c_len), 1)

    def out_body(c, carry):
        q = q_ref[rows(c), :].astype(F32) * scale
        k = k_ref[rows(c), :].astype(F32)
        bcf = bf_s[rows(c), :]
        bcb = bb_s[rows(c), :]
        qf = (q * jnp.exp(bcf)).astype(BF16)
        kf = (k * jnp.exp(-bcf)).astype(BF16)
        qb = (q * jnp.exp(bcb)).astype(BF16)
        kb = (k * jnp.exp(-bcb)).astype(BF16)
        att_f = lax.dot_general(qf, kf, NT_DIMS, preferred_element_type=F32)
        att_b = lax.dot_general(qb, kb, NT_DIMS, preferred_element_type=F32)
        att = jnp.where(ri >= ci, att_f, att_b).astype(BF16)
        q_both = jnp.concatenate([qf, qb], axis=1)
        o = (jnp.dot(att, v_ref[rows(c), :], preferred_element_type=F32)
             + lax.dot_general(q_both, s_all[c], NT_DIMS, preferred_element_type=F32))
        g = g_ref[rows(c), :].astype(F32)
        o_ref[rows(c), :] = (_rms(o, gain_ref[...]) * (g * _sigmoid(g))).astype(BF16)
        return carry

    lax.fori_loop(0, nc, out_body, 0, unroll=8)


def _gla(proj3, gin3, wf, b_f, wb, b_b, out_norm):
    b, seq, _ = proj3.shape
    kw = C_HEADS * C_DK
    k_blk = kw // C_DK
    v_blk = 2 * kw // C_DV
    g_blk = v_blk + C_HEADS
    nc = seq // C_CHUNK
    return pl.pallas_call(
        functools.partial(_gla_kernel, seq=seq),
        grid=(b, C_HEADS),
        in_specs=[
            pl.BlockSpec((None, seq, C_DK), lambda i, h: (i, 0, h)),
            pl.BlockSpec((None, seq, C_DK), lambda i, h: (i, 0, k_blk + h)),
            pl.BlockSpec((None, seq, C_DV), lambda i, h: (i, 0, v_blk + h)),
            pl.BlockSpec((None, seq, C_DV), lambda i, h: (i, 0, g_blk + h)),
            pl.BlockSpec((None, seq, LANES), lambda i, h: (i, 0, 0)),
            pl.BlockSpec((LANES, C_DK), lambda i, h: (0, h)),
            pl.BlockSpec((1, C_DK), lambda i, h: (0, h)),
            pl.BlockSpec((LANES, C_DK), lambda i, h: (0, h)),
            pl.BlockSpec((1, C_DK), lambda i, h: (0, h)),
            pl.BlockSpec((1, C_DV), lambda i, h: (0, 0)),
        ],
        out_specs=pl.BlockSpec((None, seq, C_DV), lambda i, h: (i, 0, h)),
        out_shape=jax.ShapeDtypeStruct((b, seq, C_HEADS * C_DV), BF16),
        scratch_shapes=[
            pltpu.VMEM((seq, C_DK), F32),
            pltpu.VMEM((seq, C_DK), F32),
            pltpu.VMEM((nc, C_DV, 2 * C_DK), BF16),
            pltpu.VMEM((C_DV, C_DK), F32),
            pltpu.VMEM((C_DV, C_DK), F32),
        ],
        compiler_params=_cparams(("parallel", "parallel")),
        name="gla",
    )(proj3, proj3, proj3, proj3, gin3, wf, b_f, wb, b_b, out_norm.reshape(1, C_DV))


def _final_kernel(x_ref, y0_ref, y1_ref, gain_ref, o_ref):
    tm = x_ref.shape[0]
    x = x_ref[...] + _load_token_tiles(y0_ref, tm) + _load_token_tiles(y1_ref, tm)
    o_ref[...] = _rms(x, gain_ref[...])


def _final_norm(n, x, y, gain):
    tm = ROW_TILE
    return pl.pallas_call(
        _final_kernel,
        grid=(n // tm,),
        in_specs=[_row_spec(tm, D_MODEL), _row_spec(tm * FEAT_ROWS, LANES),
                  _row_spec(tm * FEAT_ROWS, LANES, n // tm), _full_spec((1, D_MODEL))],
        out_specs=_row_spec(tm, D_MODEL),
        out_shape=jax.ShapeDtypeStruct((n, D_MODEL), F32),
        compiler_params=_cparams(("parallel",)),
        name="final_norm",
    )(x, y, y, gain.reshape(1, D_MODEL))


def _moe_layer(n, hn, ids_pad, wts_pad, w_gate, w_up, w_down, layer):
    plan = _moe_plan(ids_pad[:2], wts_pad[:2], n)
    return _moe_experts(n, hn, plan, w_gate, w_up, w_down, layer)


def kernel(x, t5_bias, norm_mix, norm_ffn, norm_final, ev_w_in, ev_lambda, ev_subln, ev_rpb, ev_w_out,
           od_w_in, od_w_gk_fwd, od_b_gk_fwd, od_w_gk_bwd, od_b_gk_bwd, od_out_norm, od_w_out,
           moe_w_grp, moe_b_grp, moe_w_exp, moe_b_exp, moe_w_gate, moe_w_up, moe_w_down):
    b, seq, d = x.shape
    n = b * seq
    rows = seq // GRID_W
    assert d == D_MODEL and n % ROW_TILE == 0 and seq % ATT_TQ == 0 and seq % C_CHUNK == 0
    assert rows % NA_QROWS == 0 and rows >= NA_KROWS and MOE_TILE <= ROW_TILE
    x2 = x.reshape(n, d)

    (proj,) = _in_proj_call(n, [(x2, 0)], norm_mix[0], ev_w_in[0].astype(BF16))
    proj3 = proj.reshape(b, seq, proj.shape[1])
    lam_init = 0.8 - 0.6 * math.exp(-0.3 * 0)
    lp = ev_lambda[0].astype(F32)
    lam = (jnp.exp(jnp.sum(lp[0] * lp[1])) - jnp.exp(jnp.sum(lp[2] * lp[3])) + lam_init).reshape(1)
    strip = _t5_strip(t5_bias, seq, ATT_TQ)
    a_diff = _diff_attn(proj3, lam, strip, ev_subln[0], 1.0 - lam_init)

    starts, classes, _ = _na_block_geometry(rows)
    table = _na_table(ev_rpb[0], rows)
    a_na = _na_attn(proj3, table, starts, classes)

    w_out = ev_w_out[0].astype(BF16)
    w_r, b_r = _router_params(moe_w_grp[0], moe_b_grp[0], moe_w_exp[0], moe_b_exp[0])
    x1, hn, ids, wts = _out_proj_router(
        n, x2, [a_diff.reshape(n, A_W), a_na.reshape(n, B_W)], [w_out[:A_W], w_out[A_W:]],
        norm_ffn[0], w_r, b_r)
    y = _moe_layer(n, hn, ids, wts, moe_w_gate, moe_w_up, moe_w_down, 0)

    kw = C_HEADS * C_DK
    main_w = 2 * kw + 2 * C_HEADS * C_DV
    w_in = od_w_in[0]
    w_gates = jnp.pad(w_in[:, main_w:], ((0, 0), (0, LANES - 2 * C_GATE_RANK))).astype(BF16)
    wf = jnp.pad(od_w_gk_fwd[0], ((0, LANES - C_GATE_RANK), (0, 0))).astype(BF16)
    wb = jnp.pad(od_w_gk_bwd[0], ((C_GATE_RANK, LANES - 2 * C_GATE_RANK), (0, 0))).astype(BF16)
    x2b, proj, gin = _in_proj_call(
        n, [(x1, 0), (y, 0), (y, n)], norm_mix[1], w_in[:, :main_w].astype(BF16), gates=w_gates)
    a_gla = _gla(proj.reshape(b, seq, main_w), gin.reshape(b, seq, LANES), wf, od_b_gk_fwd[0].reshape(1, kw),
                 wb, od_b_gk_bwd[0].reshape(1, kw), od_out_norm[0])

    w_r, b_r = _router_params(moe_w_grp[1], moe_b_grp[1], moe_w_exp[1], moe_b_exp[1])
    x3, hn, ids, wts = _out_proj_router(
        n, x2b, [a_gla.reshape(n, C_HEADS * C_DV)], [od_w_out[0].astype(BF16)], norm_ffn[1], w_r, b_r)
    y = _moe_layer(n, hn, ids, wts, moe_w_gate, moe_w_up, moe_w_down, 1)

    return _final_norm(n, x3, y, norm_final).reshape(b, seq, d)


def _na_table(rpb, rows):
    kh = min(NA_MAX_ROWS, rows)
    starts, classes, reps = _na_block_geometry(rows)
    qn, kn = NA_QROWS * GRID_W, NA_KROWS * GRID_W
    q_row, q_col = jnp.arange(qn) // GRID_W, jnp.arange(qn) % GRID_W
    k_row, k_col = jnp.arange(kn) // GRID_W, jnp.arange(kn) % GRID_W
    col_start = jnp.clip(q_col - NA_COLS // 2, 0, GRID_W - NA_COLS)
    col_ok = (k_col[None, :] >= col_start[:, None]) & (k_col[None, :] < col_start[:, None] + NA_COLS)
    side = GRID_W - NA_COLS
    col_bias = _toeplitz(jnp.pad(rpb.astype(F32), ((0, 0), (0, 0), (side, side))), GRID_W, GRID_W)
    per_class = []
    for cls in range(len(reps)):
        j = classes.index(cls)
        q_abs = j * NA_QROWS + q_row
        k_abs = starts[j] + k_row
        row_start = jnp.clip(q_abs - kh // 2, 0, rows - kh)
        row_ok = (k_abs[None, :] >= row_start[:, None]) & (k_abs[None, :] < row_start[:, None] + kh)
        dr = jnp.clip(starts[j] + jnp.arange(NA_KROWS)[None, :] - (j * NA_QROWS + jnp.arange(NA_QROWS))[:, None]
                      + NA_MAX_ROWS - 1, 0, 2 * NA_MAX_ROWS - 2)
        blocks = col_bias[:, dr]
        bias = jnp.transpose(blocks, (0, 1, 3, 2, 4)).reshape(B_HEADS, qn, kn)
        per_class.append(jnp.where((row_ok & col_ok)[None], bias, -jnp.inf))
    t = jnp.stack(per_class, axis=1)
    return jnp.transpose(t.reshape(B_HEADS // 2, 2, len(reps), qn, kn), (0, 2, 1, 3, 4))
```

```python
import functools
import math

import jax
import jax.numpy as jnp
from jax import lax
from jax.experimental import pallas as pl
from jax.experimental.pallas import tpu as pltpu

D_MODEL = 1024
GRID_W = 64
A_HEADS = 4
A_DH = 64
B_HEADS = 8
B_DH = 64
NA_MAX_ROWS = 8
NA_COLS = 16
T5_BUCKETS = 32
T5_MAX_DIST = 128
C_HEADS = 4
C_DK = 128
C_DV = 256
C_GATE_RANK = 16
C_GATE_NORM = 16.0
C_CHUNK = 64
N_GROUPS = 4
EXP_PER_GROUP = 8
N_EXPERTS = N_GROUPS * EXP_PER_GROUP
D_EXPERT = 512
EPS = 1e-6

A_W = A_HEADS * 2 * A_DH
B_W = B_HEADS * B_DH
LANES = 128
FEAT_ROWS = D_MODEL // LANES
VMEM_LIMIT = 56 * 1024 * 1024

ROW_TILE = 512
ATT_TQ = 256
NA_QROWS = 4
NA_KROWS = 12
MOE_TILE = 256
ROUTE_ROWS = 8

F32 = jnp.float32
BF16 = jnp.bfloat16
NT_DIMS = (((1,), (1,)), ((), ()))
TN_DIMS = (((0,), (0,)), ((), ()))


def _cparams(sem):
    return pltpu.CompilerParams(dimension_semantics=sem, vmem_limit_bytes=VMEM_LIMIT)


def _load_token_tiles(ref, tm, lead=()):
    return jnp.concatenate(
        [ref[lead + (pl.ds(j, tm, stride=FEAT_ROWS), slice(None))] for j in range(FEAT_ROWS)], axis=1)


def _store_token_tiles(ref, val, lead=()):
    tm = val.shape[0]
    for j in range(FEAT_ROWS):
        ref[lead + (pl.ds(j, tm, stride=FEAT_ROWS), slice(None))] = val[:, j * LANES:(j + 1) * LANES]


def _rms(x, gain):
    return x * lax.rsqrt(jnp.mean(x * x, axis=-1, keepdims=True) + EPS) * gain


def _sigmoid(x):
    return 1.0 / (1.0 + jnp.exp(-x))


def _log_sigmoid(z):
    return jnp.minimum(z, 0.0) - jnp.log(1.0 + jnp.exp(-jnp.abs(z)))


def _in_proj_kernel(*refs, n_parts, gates):
    parts = refs[:n_parts]
    gain_ref, w_ref = refs[n_parts], refs[n_parts + 1]
    pos = n_parts + 2
    if gates:
        wg_ref = refs[pos]
        pos += 1
    outs = refs[pos:]
    x = parts[0][...]
    for p in parts[1:]:
        x = x + _load_token_tiles(p, x.shape[0])
    oi = 0
    if n_parts > 1:
        outs[0][...] = x
        oi = 1
    h = _rms(x, gain_ref[...]).astype(BF16)
    proj_ref = outs[oi]
    n_out = proj_ref.shape[1]
    step = 512
    for j in range(n_out // step):
        proj_ref[:, j * step:(j + 1) * step] = jnp.dot(
            h, w_ref[:, j * step:(j + 1) * step], preferred_element_type=F32).astype(BF16)
    if gates:
        outs[oi + 1][...] = jnp.dot(h, wg_ref[...], preferred_element_type=F32).astype(BF16)


def _row_spec(tm, width, block_off=0):
    return pl.BlockSpec((tm, width), lambda i, o=block_off: (i + o, 0))


def _full_spec(shape):
    nd = len(shape)
    return pl.BlockSpec(shape, lambda i, _nd=nd: (0,) * _nd)


def _in_proj_call(n, parts, gain, w, gates=None):
    tm = ROW_TILE
    n_parts = len(parts)
    n_out = w.shape[1]
    in_specs = [_row_spec(tm, D_MODEL)] + [_row_spec(tm * FEAT_ROWS, LANES, off // tm) for _, off in parts[1:]]
    args = [a for a, _ in parts]
    in_specs += [_full_spec((1, D_MODEL)), _full_spec(w.shape)]
    args += [gain.reshape(1, D_MODEL), w]
    out_shape, out_specs = [], []
    if n_parts > 1:
        out_shape.append(jax.ShapeDtypeStruct((n, D_MODEL), F32))
        out_specs.append(_row_spec(tm, D_MODEL))
    out_shape.append(jax.ShapeDtypeStruct((n, n_out), BF16))
    out_specs.append(_row_spec(tm, n_out))
    if gates is not None:
        in_specs.append(_full_spec(gates.shape))
        args.append(gates)
        out_shape.append(jax.ShapeDtypeStruct((n, LANES), BF16))
        out_specs.append(_row_spec(tm, LANES))
    return pl.pallas_call(
        functools.partial(_in_proj_kernel, n_parts=n_parts, gates=gates is not None),
        grid=(n // tm,),
        in_specs=in_specs,
        out_specs=out_specs,
        out_shape=out_shape,
        compiler_params=_cparams(("parallel",)),
        name="in_proj_gla" if gates is not None else "in_proj_attn",
    )(*args)


def _diff_attn_kernel(lam_ref, q_ref, k_ref, v_ref, bias_ref, gain_ref, o_ref, *, seq, out_scale):
    tq = q_ref.shape[0]
    qb = pl.program_id(2)
    q = q_ref[...]
    k = k_ref[...]
    lane = lax.broadcasted_iota(jnp.int32, q.shape, 1)
    scale = A_DH ** -0.5
    zero = jnp.zeros_like(q)
    off = pl.multiple_of((seq - tq) - qb * tq, LANES)
    bias = bias_ref[:, pl.ds(off, seq)]

    def exp_scores(qm):
        s = lax.dot_general(qm * scale, k, NT_DIMS, preferred_element_type=F32) + bias
        e = jnp.exp(s - jnp.max(s, axis=-1, keepdims=True))
        return e, jnp.sum(e, axis=-1, keepdims=True)

    e0, l0 = exp_scores(jnp.where(lane < A_DH, q, zero))
    e1, l1 = exp_scores(jnp.where(lane >= A_DH, q, zero))
    attn = (e0 - (lam_ref[0] * l0 / l1) * e1).astype(BF16)
    o = jnp.dot(attn, v_ref[...], preferred_element_type=F32) / l0
    o_ref[...] = (_rms(o, gain_ref[...]) * out_scale).astype(BF16)


def _t5_bucket(rel):
    half = T5_BUCKETS // 2
    max_exact = half // 2
    sign_off = jnp.where(rel > 0, half, 0)
    n = jnp.abs(rel)
    nf = jnp.maximum(n, 1).astype(F32)
    large = max_exact + (jnp.log(nf / max_exact) / math.log(T5_MAX_DIST / max_exact)
                         * (half - max_exact)).astype(jnp.int32)
    large = jnp.minimum(large, half - 1)
    return sign_off + jnp.where(n < max_exact, n, large)


def _t5_strip(t5_bias, seq, tq):
    n_rel = 2 * seq - 1
    vec = t5_bias[_t5_bucket(jnp.arange(n_rel) - (seq - 1))].astype(F32).T
    return _toeplitz(vec, tq, 2 * seq - tq)


def _toeplitz(vec, n_rows, n_cols):
    assert n_cols + n_rows - 1 <= vec.shape[-1]
    n = -(-max(vec.shape[-1] - 1, n_cols) // LANES) * LANES + 1
    vec = jnp.pad(vec, [(0, 0)] * (vec.ndim - 1) + [(0, n - vec.shape[-1])])
    lead = vec.shape[:-1]
    rolled = jnp.roll(vec, -(n_rows - 1), axis=-1)
    reps = -(-(n_rows * (n - 1)) // n)
    flat = jnp.broadcast_to(rolled[..., None, :], lead + (reps, n)).reshape(lead + (reps * n,))
    return flat[..., :n_rows * (n - 1)].reshape(lead + (n_rows, n - 1))[..., :n_cols]


def _diff_attn(proj3, lam, strip, subln, out_scale):
    b, seq, _ = proj3.shape
    tq = ATT_TQ
    kblk, vblk = A_W // LANES, 2 * A_W // LANES
    return pl.pallas_call(
        functools.partial(_diff_attn_kernel, seq=seq, out_scale=out_scale),
        grid=(A_HEADS, b, seq // tq),
        in_specs=[
            pl.BlockSpec(memory_space=pltpu.SMEM),
            pl.BlockSpec((None, tq, LANES), lambda h, i, j: (i, j, h)),
            pl.BlockSpec((None, seq, LANES), lambda h, i, j: (i, 0, kblk + h)),
            pl.BlockSpec((None, seq, LANES), lambda h, i, j: (i, 0, vblk + h)),
            pl.BlockSpec((None, tq, 2 * seq - tq), lambda h, i, j: (h, 0, 0)),
            pl.BlockSpec((1, LANES), lambda h, i, j: (0, 0)),
        ],
        out_specs=pl.BlockSpec((None, tq, LANES), lambda h, i, j: (i, j, h)),
        out_shape=jax.ShapeDtypeStruct((b, seq, A_W), BF16),
        compiler_params=_cparams(("parallel", "parallel", "parallel")),
        name="diff_attn",
    )(lam, proj3, proj3, proj3, strip, subln.reshape(1, LANES))


def _na_block_geometry(rows):
    nblk = rows // NA_QROWS
    kh = min(NA_MAX_ROWS, rows)
    starts, classes, reps = [], [], []
    for j in range(nblk):
        ks = min(max(j * NA_QROWS - kh // 2, 0), rows - NA_KROWS)
        rel = (j * NA_QROWS - ks,) + tuple(
            min(max(r - kh // 2, 0), rows - kh) - ks for r in range(j * NA_QROWS, (j + 1) * NA_QROWS))
        starts.append(ks)
        if rel not in reps:
            reps.append(rel)
        classes.append(reps.index(rel))
    return starts, classes, reps


def _na_kernel(q_ref, k_ref, v_ref, tab_ref, o_ref, *, starts, classes):
    qn, kn = NA_QROWS * GRID_W, NA_KROWS * GRID_W
    scale = B_DH ** -0.5
    lane = lax.broadcasted_iota(jnp.int32, (qn, LANES), 1)
    for j, (ks, cls) in enumerate(zip(starts, classes)):
        q = q_ref[j * qn:(j + 1) * qn, :]
        kw = k_ref[ks * GRID_W:ks * GRID_W + kn, :]
        vw = v_ref[ks * GRID_W:ks * GRID_W + kn, :]
        zero = jnp.zeros_like(q)
        outs = []
        for hl in range(2):
            in_head = (lane >= hl * B_DH) & (lane < (hl + 1) * B_DH)
            qm = jnp.where(in_head, q, zero) * scale
            s = lax.dot_general(qm, kw, NT_DIMS, preferred_element_type=F32) + tab_ref[hl, cls]
            m = jnp.max(s, axis=-1, keepdims=True)
            p = jnp.exp(s - m)
            denom = jnp.sum(p, axis=-1, keepdims=True)
            outs.append(jnp.dot(p.astype(BF16), vw, preferred_element_type=F32) / denom)
        o_ref[j * qn:(j + 1) * qn, :] = jnp.where(lane < B_DH, outs[0], outs[1]).astype(BF16)


def _na_attn(proj3, table, starts, classes):
    b, seq, _ = proj3.shape
    qblk = 3 * A_W // LANES
    kblk = qblk + B_W // LANES
    vblk = kblk + B_W // LANES
    ncls = table.shape[2]
    qn, kn = NA_QROWS * GRID_W, NA_KROWS * GRID_W
    return pl.pallas_call(
        functools.partial(_na_kernel, starts=tuple(starts), classes=tuple(classes)),
        grid=(B_HEADS // 2, b),
        in_specs=[
            pl.BlockSpec((None, seq, LANES), lambda h, i: (i, 0, qblk + h)),
            pl.BlockSpec((None, seq, LANES), lambda h, i: (i, 0, kblk + h)),
            pl.BlockSpec((None, seq, LANES), lambda h, i: (i, 0, vblk + h)),
            pl.BlockSpec((None, 2, ncls, qn, kn), lambda h, i: (h, 0, 0, 0, 0)),
        ],
        out_specs=pl.BlockSpec((None, seq, LANES), lambda h, i: (i, 0, h)),
        out_shape=jax.ShapeDtypeStruct((b, seq, B_W), BF16),
        compiler_params=_cparams(("parallel", "parallel")),
        name="na_attn",
    )(proj3, proj3, proj3, table)


def _out_proj_router_kernel(*refs, n_a):
    x_ref = refs[0]
    a_refs = refs[1:1 + n_a]
    w_refs = refs[1 + n_a:1 + 2 * n_a]
    gain_ref, wr_ref, br_ref, xo_ref, hn_ref, ids_ref, wts_ref = refs[1 + 2 * n_a:]
    acc = x_ref[...]
    for a, w in zip(a_refs, w_refs):
        acc = acc + jnp.dot(a[...], w[...], preferred_element_type=F32)
    xo_ref[...] = acc
    h = _rms(acc, gain_ref[...])
    _store_token_tiles(hn_ref, h)
    h_hi = h.astype(BF16)
    h_lo = (h - h_hi.astype(F32)).astype(BF16)
    hi = jnp.dot(h_hi, wr_ref[...], preferred_element_type=F32)
    logits = (hi[:, :LANES] + hi[:, LANES:]
              + jnp.dot(h_lo, wr_ref[:, :LANES], preferred_element_type=F32) + br_ref[...])
    lane = lax.broadcasted_iota(jnp.int32, logits.shape, 1)
    neg = jnp.float32(-jnp.inf)
    big = jnp.int32(LANES)

    def masked_softmax(mask):
        z = jnp.where(mask, logits, neg)
        e = jnp.exp(z - jnp.max(z, axis=-1, keepdims=True))
        return e / jnp.sum(e, axis=-1, keepdims=True)

    def top1(p, mask):
        w = jnp.max(jnp.where(mask, p, -1.0), axis=-1, keepdims=True)
        idx = jnp.min(jnp.where(mask & (p == w), lane, big), axis=-1, keepdims=True)
        return w, idx

    is_grp = lane < N_GROUPS
    g_w, g_idx = top1(masked_softmax(is_grp), is_grp)
    e_lane = lane - N_GROUPS
    in_grp = (e_lane >= g_idx * EXP_PER_GROUP) & (e_lane < (g_idx + 1) * EXP_PER_GROUP)
    p_e = masked_softmax(in_grp)
    w1, i1 = top1(p_e, in_grp)
    rest = in_grp & (lane != i1)
    w2, i2 = top1(p_e, rest)
    denom = w1 + w2
    ids = jnp.where(lane == 0, i1 - N_GROUPS, i2 - N_GROUPS)
    wts = jnp.where(lane == 0, g_w * (w1 / denom), g_w * (w2 / denom))
    ids_ref[...] = ids.T[:ids_ref.shape[0]]
    wts_ref[...] = wts.T[:wts_ref.shape[0]]


def _out_proj_router(n, x, acts, ws, gain, w_router, b_router):
    tm = ROW_TILE
    n_a = len(acts)
    in_specs = [_row_spec(tm, D_MODEL)]
    in_specs += [_row_spec(tm, a.shape[1]) for a in acts]
    in_specs += [_full_spec(w.shape) for w in ws]
    in_specs += [_full_spec((1, D_MODEL)), _full_spec(w_router.shape), _full_spec((1, LANES))]
    return pl.pallas_call(
        functools.partial(_out_proj_router_kernel, n_a=n_a),
        grid=(n // tm,),
        in_specs=in_specs,
        out_specs=[_row_spec(tm, D_MODEL), _row_spec(tm * FEAT_ROWS, LANES),
                   pl.BlockSpec((ROUTE_ROWS, tm), lambda i: (0, i)), pl.BlockSpec((ROUTE_ROWS, tm), lambda i: (0, i))],
        out_shape=[jax.ShapeDtypeStruct((n, D_MODEL), F32), jax.ShapeDtypeStruct((n * FEAT_ROWS, LANES), F32),
                   jax.ShapeDtypeStruct((ROUTE_ROWS, n), jnp.int32), jax.ShapeDtypeStruct((ROUTE_ROWS, n), F32)],
        compiler_params=_cparams(("parallel",)),
        name="out_proj_router",
    )(x, *acts, *ws, gain.reshape(1, D_MODEL), w_router, b_router)


def _router_params(w_grp, b_grp, w_exp, b_exp):
    pad = LANES - N_GROUPS - N_EXPERTS
    w = jnp.concatenate([w_grp, w_exp, jnp.zeros((D_MODEL, pad), F32)], axis=1)
    b = jnp.concatenate([b_grp, b_exp, jnp.zeros((pad,), F32)]).reshape(1, LANES)
    w_hi = w.astype(BF16)
    w_lo = (w - w_hi.astype(F32)).astype(BF16)
    return jnp.concatenate([w_hi, w_lo], axis=1), b


def _moe_plan(ids, wts, n):
    tm = MOE_TILE
    n_assign = 2 * n
    p_rows = n_assign + N_EXPERTS * tm
    e_flat = ids.reshape(-1)
    w_flat = wts.reshape(-1)
    idx_bits = max(n_assign - 1, 1).bit_length()
    assert idx_bits + N_EXPERTS.bit_length() <= 31
    key = jnp.sort((e_flat << idx_bits) | jnp.arange(n_assign, dtype=jnp.int32))
    order = key & ((1 << idx_bits) - 1)
    experts = jnp.arange(N_EXPERTS, dtype=jnp.int32)
    counts = jnp.sum((key >> idx_bits)[None, :] == experts[:, None], axis=1, dtype=jnp.int32)
    off = jnp.cumsum(counts) - counts
    padded = (counts + tm - 1) // tm * tm
    p_end = jnp.cumsum(padded)
    p_off = p_end - padded
    nt = p_rows // tm
    tile_start = jnp.arange(nt, dtype=jnp.int32) * tm
    tile_expert = jnp.minimum(jnp.sum(p_end[None, :] <= tile_start[:, None], axis=1, dtype=jnp.int32),
                              N_EXPERTS - 1)
    tile_expert, counts, p_off, off = lax.optimization_barrier((tile_expert, counts, p_off, off))
    tile_count = jnp.clip(counts[tile_expert] - (tile_start - p_off[tile_expert]), 0, tm).astype(jnp.int32)
    n_used = (p_end[-1] // tm).astype(jnp.int32).reshape(1)
    in_tile = jnp.arange(tm, dtype=jnp.int32)[None, :]
    real = in_tile < tile_count[:, None]
    s_idx = jnp.clip((off[tile_expert] + tile_start - p_off[tile_expert])[:, None] + in_tile, 0, n_assign - 1)
    assign = jnp.where(real, order[s_idx], 0)
    src = (assign % n).reshape(p_rows)
    dest = assign.reshape(p_rows)
    w_row = jnp.where(real, w_flat[assign], 0.0).reshape(p_rows)
    return (src.reshape(nt, 1, tm), dest.reshape(nt, 1, tm), w_row.reshape(p_rows, 1), tile_expert,
            tile_count, n_used)


def _moe_kernel(te_ref, tc_ref, nu_ref, src_ref, dst_ref, wrow_ref, x_hbm, wg_ref, wu_ref, wd_ref, y_hbm,
                xbuf, ybuf, wgb, wub, wdb, gsem, ssem):
    tm = MOE_TILE
    i = pl.program_id(0)
    n_used = nu_ref[0]
    last = pl.num_programs(0) - 1

    def tile_rows(r):
        return pl.ds(pl.multiple_of(r * FEAT_ROWS, FEAT_ROWS), FEAT_ROWS)

    def gather_copy(slot, r, tok):
        return pltpu.make_async_copy(x_hbm.at[tile_rows(tok)], xbuf.at[slot, tile_rows(r)], gsem.at[slot])

    def scatter_copy(slot, r, row):
        return pltpu.make_async_copy(ybuf.at[slot, tile_rows(r)], y_hbm.at[tile_rows(row)], ssem.at[slot])

    def start_all(make):
        def body(r2, c):
            make(2 * r2).start(priority=0)
            make(2 * r2 + 1).start(priority=1)
            return c
        lax.fori_loop(0, tm // 2, body, 0, unroll=4)

    def start_rows(count, make):
        @pl.when(count == tm)
        def _():
            start_all(make)

        @pl.when(count < tm)
        def _():
            lax.fori_loop(0, count, lambda r, c: (make(r).start(), c)[1], 0)

    def wait_scatter(t):
        slot = t % 2
        count = tc_ref[t]

        @pl.when(count == tm)
        def _():
            pltpu.make_async_copy(ybuf.at[slot], y_hbm.at[pl.ds(0, tm * FEAT_ROWS)], ssem.at[slot]).wait()

        @pl.when(count < tm)
        def _():
            lax.fori_loop(0, count, lambda r, c: (scatter_copy(slot, r, 0).wait(), c)[1], 0)

    @pl.when(i < n_used)
    def _():
        slot = i % 2
        start_all(lambda r: gather_copy(slot, r, src_ref[0, r]))

    @pl.when((i >= 1) & (i <= n_used))
    def _():
        t = i - 1
        slot = t % 2
        pltpu.make_async_copy(x_hbm.at[pl.ds(0, tm * FEAT_ROWS)], xbuf.at[slot], gsem.at[slot]).wait()

        @pl.when(t >= 2)
        def _():
            wait_scatter(t - 2)

        @pl.when(jnp.logical_or(t == 0, te_ref[t] != te_ref[jnp.maximum(t - 1, 0)]))
        def _():
            wgb[...] = wg_ref[...].astype(BF16)
            wub[...] = wu_ref[...].astype(BF16)
            wdb[...] = wd_ref[...].astype(BF16)

        x = _load_token_tiles(xbuf, tm, lead=(slot,)).astype(BF16)
        g = jnp.dot(x, wgb[...], preferred_element_type=F32)
        u = jnp.dot(x, wub[...], preferred_element_type=F32)
        hid = (g * _sigmoid(g) * u).astype(BF16)
        y = jnp.dot(hid, wdb[...], preferred_element_type=F32)
        _store_token_tiles(ybuf, y * wrow_ref[...], lead=(slot,))

        start_rows(tc_ref[t], lambda r: scatter_copy(slot, r, dst_ref[0, r]))

    @pl.when(i == last)
    def _():
        wait_scatter(n_used - 1)

        @pl.when(n_used >= 2)
        def _():
            wait_scatter(n_used - 2)


def _moe_experts(n, hn, plan, w_gate, w_up, w_down, layer):
    tm = MOE_TILE
    src, dest, w_row, tile_expert, tile_count, n_used = plan
    nt = src.shape[0]
    prev = lambda i: jnp.maximum(i - 1, 0)
    grid_spec = pltpu.PrefetchScalarGridSpec(
        num_scalar_prefetch=3,
        grid=(nt + 1,),
        in_specs=[
            pl.BlockSpec((None, 1, tm), lambda i, te, tc, nu: (jnp.minimum(i, nt - 1), 0, 0), memory_space=pltpu.SMEM),
            pl.BlockSpec((None, 1, tm), lambda i, te, tc, nu: (prev(i), 0, 0), memory_space=pltpu.SMEM),
            pl.BlockSpec((tm, 1), lambda i, te, tc, nu: (prev(i), 0)),
            pl.BlockSpec(memory_space=pl.ANY),
            pl.BlockSpec((None, None, D_MODEL, D_EXPERT), lambda i, te, tc, nu: (layer, te[prev(i)], 0, 0)),
            pl.BlockSpec((None, None, D_MODEL, D_EXPERT), lambda i, te, tc, nu: (layer, te[prev(i)], 0, 0)),
            pl.BlockSpec((None, None, D_EXPERT, D_MODEL), lambda i, te, tc, nu: (layer, te[prev(i)], 0, 0)),
        ],
        out_specs=pl.BlockSpec(memory_space=pl.ANY),
        scratch_shapes=[
            pltpu.VMEM((2, tm * FEAT_ROWS, LANES), F32),
            pltpu.VMEM((2, tm * FEAT_ROWS, LANES), F32),
            pltpu.VMEM((D_MODEL, D_EXPERT), BF16),
            pltpu.VMEM((D_MODEL, D_EXPERT), BF16),
            pltpu.VMEM((D_EXPERT, D_MODEL), BF16),
            pltpu.SemaphoreType.DMA((2,)),
            pltpu.SemaphoreType.DMA((2,)),
        ],
    )
    return pl.pallas_call(
        _moe_kernel,
        grid_spec=grid_spec,
        out_shape=jax.ShapeDtypeStruct((2 * n * FEAT_ROWS, LANES), F32),
        compiler_params=_cparams(("arbitrary",)),
        name="moe_experts",
    )(tile_expert, tile_count, n_used, src, dest, w_row, hn, w_gate, w_up, w_down)


def _gla_kernel(q_ref, k_ref, v_ref, g_ref, gin_ref, wf_ref, bf_ref, wb_ref, bb_ref, gain_ref, o_ref,
                bf_s, bb_s, s_all, stf, stb, *, seq):
    c_len = C_CHUNK
    nc = seq // c_len
    scale = C_DK ** -0.5
    row = lax.broadcasted_iota(jnp.int32, (c_len, C_DK), 0)
    shifts = [1 << s for s in range(int(math.log2(c_len)))]

    def rows(c):
        return pl.ds(pl.multiple_of(c * c_len, c_len), c_len)

    gin = gin_ref[...]
    bf_s[...] = _log_sigmoid(jnp.dot(gin, wf_ref[...], preferred_element_type=F32) + bf_ref[...]) / C_GATE_NORM
    bb_s[...] = _log_sigmoid(jnp.dot(gin, wb_ref[...], preferred_element_type=F32) + bb_ref[...]) / C_GATE_NORM

    def cum_body(c, carry):
        y = bf_s[rows(c), :]
        for s in shifts:
            y = y + jnp.where(row >= s, pltpu.roll(y, s, 0), 0.0)
        bf_s[rows(c), :] = y
        y = bb_s[rows(c), :]
        for s in shifts:
            y = y + jnp.where(row < c_len - s, pltpu.roll(y, c_len - s, 0), 0.0)
        bb_s[rows(c), :] = y
        return carry

    lax.fori_loop(0, nc, cum_body, 0)

    stf[...] = jnp.zeros_like(stf)
    stb[...] = jnp.zeros_like(stb)

    def state_step(c, b_s, last_row, st, lane0):
        k = k_ref[rows(c), :].astype(F32)
        bc = b_s[rows(c), :]
        bl = bc[last_row:last_row + 1, :]
        kd = (k * jnp.exp(bl - bc)).astype(BF16)
        kv_t = lax.dot_general(v_ref[rows(c), :], kd, TN_DIMS, preferred_element_type=F32)
        s_all[c, :, lane0:lane0 + C_DK] = st[...].astype(BF16)
        st[...] = st[...] * jnp.exp(bl) + kv_t

    def state_body(i, carry):
        state_step(i, bf_s, c_len - 1, stf, 0)
        state_step(nc - 1 - i, bb_s, 0, stb, C_DK)
        return carry

    lax.fori_loop(0, nc, state_body, 0, unroll=4)

    ri = lax.broadcasted_iota(jnp.int32, (c_len, c_len), 0)
    ci = lax.broadcasted_iota(jnp.int32, (c_len, c_len), 1)

    def out_body(c, carry):
        q = q_ref[rows(c), :].astype(F32) * scale
        k = k_ref[rows(c), :].astype(F32)
        bcf = bf_s[rows(c), :]
        bcb = bb_s[rows(c), :]
        qf = (q * jnp.exp(bcf)).astype(BF16)
        kf = (k * jnp.exp(-bcf)).astype(BF16)
        qb = (q * jnp.exp(bcb)).astype(BF16)
        kb = (k * jnp.exp(-bcb)).astype(BF16)
        att_f = lax.dot_general(qf, kf, NT_DIMS, preferred_element_type=F32)
        att_b = lax.dot_general(qb, kb, NT_DIMS, preferred_element_type=F32)
        att = jnp.where(ri >= ci, att_f, att_b).astype(BF16)
        q_both = jnp.concatenate([qf, qb], axis=1)
        o = (jnp.dot(att, v_ref[rows(c), :], preferred_element_type=F32)
             + lax.dot_general(q_both, s_all[c], NT_DIMS, preferred_element_type=F32))
        g = g_ref[rows(c), :].astype(F32)
        o_ref[rows(c), :] = (_rms(o, gain_ref[...]) * (g * _sigmoid(g))).astype(BF16)
        return carry

    lax.fori_loop(0, nc, out_body, 0, unroll=8)


def _gla(proj3, gin3, wf, b_f, wb, b_b, out_norm):
    b, seq, _ = proj3.shape
    kw = C_HEADS * C_DK
    k_blk = kw // C_DK
    v_blk = 2 * kw // C_DV
    g_blk = v_blk + C_HEADS
    nc = seq // C_CHUNK
    return pl.pallas_call(
        functools.partial(_gla_kernel, seq=seq),
        grid=(b, C_HEADS),
        in_specs=[
            pl.BlockSpec((None, seq, C_DK), lambda i, h: (i, 0, h)),
            pl.BlockSpec((None, seq, C_DK), lambda i, h: (i, 0, k_blk + h)),
            pl.BlockSpec((None, seq, C_DV), lambda i, h: (i, 0, v_blk + h)),
            pl.BlockSpec((None, seq, C_DV), lambda i, h: (i, 0, g_blk + h)),
            pl.BlockSpec((None, seq, LANES), lambda i, h: (i, 0, 0)),
            pl.BlockSpec((LANES, C_DK), lambda i, h: (0, h)),
            pl.BlockSpec((1, C_DK), lambda i, h: (0, h)),
            pl.BlockSpec((LANES, C_DK), lambda i, h: (0, h)),
            pl.BlockSpec((1, C_DK), lambda i, h: (0, h)),
            pl.BlockSpec((1, C_DV), lambda i, h: (0, 0)),
        ],
        out_specs=pl.BlockSpec((None, seq, C_DV), lambda i, h: (i, 0, h)),
        out_shape=jax.ShapeDtypeStruct((b, seq, C_HEADS * C_DV), BF16),
        scratch_shapes=[
            pltpu.VMEM((seq, C_DK), F32),
            pltpu.VMEM((seq, C_DK), F32),
            pltpu.VMEM((nc, C_DV, 2 * C_DK), BF16),
            pltpu.VMEM((C_DV, C_DK), F32),
            pltpu.VMEM((C_DV, C_DK), F32),
        ],
        compiler_params=_cparams(("parallel", "parallel")),
        name="gla",
    )(proj3, proj3, proj3, proj3, gin3, wf, b_f, wb, b_b, out_norm.reshape(1, C_DV))


def _final_kernel(x_ref, y0_ref, y1_ref, gain_ref, o_ref):
    tm = x_ref.shape[0]
    x = x_ref[...] + _load_token_tiles(y0_ref, tm) + _load_token_tiles(y1_ref, tm)
    o_ref[...] = _rms(x, gain_ref[...])


def _final_norm(n, x, y, gain):
    tm = ROW_TILE
    return pl.pallas_call(
        _final_kernel,
        grid=(n // tm,),
        in_specs=[_row_spec(tm, D_MODEL), _row_spec(tm * FEAT_ROWS, LANES),
                  _row_spec(tm * FEAT_ROWS, LANES, n // tm), _full_spec((1, D_MODEL))],
        out_specs=_row_spec(tm, D_MODEL),
        out_shape=jax.ShapeDtypeStruct((n, D_MODEL), F32),
        compiler_params=_cparams(("parallel",)),
        name="final_norm",
    )(x, y, y, gain.reshape(1, D_MODEL))


def _moe_layer(n, hn, ids_pad, wts_pad, w_gate, w_up, w_down, layer):
    plan = _moe_plan(ids_pad[:2], wts_pad[:2], n)
    return _moe_experts(n, hn, plan, w_gate, w_up, w_down, layer)


def kernel(x, t5_bias, norm_mix, norm_ffn, norm_final, ev_w_in, ev_lambda, ev_subln, ev_rpb, ev_w_out,
           od_w_in, od_w_gk_fwd, od_b_gk_fwd, od_w_gk_bwd, od_b_gk_bwd, od_out_norm, od_w_out,
           moe_w_grp, moe_b_grp, moe_w_exp, moe_b_exp, moe_w_gate, moe_w_up, moe_w_down):
    b, seq, d = x.shape
    n = b * seq
    rows = seq // GRID_W
    assert d == D_MODEL and n % ROW_TILE == 0 and seq % ATT_TQ == 0 and seq % C_CHUNK == 0
    assert rows % NA_QROWS == 0 and rows >= NA_KROWS and MOE_TILE <= ROW_TILE
    x2 = x.reshape(n, d)

    (proj,) = _in_proj_call(n, [(x2, 0)], norm_mix[0], ev_w_in[0].astype(BF16))
    proj3 = proj.reshape(b, seq, proj.shape[1])
    lam_init = 0.8 - 0.6 * math.exp(-0.3 * 0)
    lp = ev_lambda[0].astype(F32)
    lam = (jnp.exp(jnp.sum(lp[0] * lp[1])) - jnp.exp(jnp.sum(lp[2] * lp[3])) + lam_init).reshape(1)
    strip = _t5_strip(t5_bias, seq, ATT_TQ)
    a_diff = _diff_attn(proj3, lam, strip, ev_subln[0], 1.0 - lam_init)

    starts, classes, _ = _na_block_geometry(rows)
    table = _na_table(ev_rpb[0], rows)
    a_na = _na_attn(proj3, table, starts, classes)

    w_out = ev_w_out[0].astype(BF16)
    w_r, b_r = _router_params(moe_w_grp[0], moe_b_grp[0], moe_w_exp[0], moe_b_exp[0])
    x1, hn, ids, wts = _out_proj_router(
        n, x2, [a_diff.reshape(n, A_W), a_na.reshape(n, B_W)], [w_out[:A_W], w_out[A_W:]],
        norm_ffn[0], w_r, b_r)
    y = _moe_layer(n, hn, ids, wts, moe_w_gate, moe_w_up, moe_w_down, 0)

    kw = C_HEADS * C_DK
    main_w = 2 * kw + 2 * C_HEADS * C_DV
    w_in = od_w_in[0]
    w_gates = jnp.pad(w_in[:, main_w:], ((0, 0), (0, LANES - 2 * C_GATE_RANK))).astype(BF16)
    wf = jnp.pad(od_w_gk_fwd[0], ((0, LANES - C_GATE_RANK), (0, 0))).astype(BF16)
    wb = jnp.pad(od_w_gk_bwd[0], ((C_GATE_RANK, LANES - 2 * C_GATE_RANK), (0, 0))).astype(BF16)
    x2b, proj, gin = _in_proj_call(
        n, [(x1, 0), (y, 0), (y, n)], norm_mix[1], w_in[:, :main_w].astype(BF16), gates=w_gates)
    a_gla = _gla(proj.reshape(b, seq, main_w), gin.reshape(b, seq, LANES), wf, od_b_gk_fwd[0].reshape(1, kw),
                 wb, od_b_gk_bwd[0].reshape(1, kw), od_out_norm[0])

    w_r, b_r = _router_params(moe_w_grp[1], moe_b_grp[1], moe_w_exp[1], moe_b_exp[1])
    x3, hn, ids, wts = _out_proj_router(
        n, x2b, [a_gla.reshape(n, C_HEADS * C_DV)], [od_w_out[0].astype(BF16)], norm_ffn[1], w_r, b_r)
    y = _moe_layer(n, hn, ids, wts, moe_w_gate, moe_w_up, moe_w_down, 1)

    return _final_norm(n, x3, y, norm_final).reshape(b, seq, d)


def _na_table(rpb, rows):
    kh = min(NA_MAX_ROWS, rows)
    starts, classes, reps = _na_block_geometry(rows)
    qn, kn = NA_QROWS * GRID_W, NA_KROWS * GRID_W
    q_row, q_col = jnp.arange(qn) // GRID_W, jnp.arange(qn) % GRID_W
    k_row, k_col = jnp.arange(kn) // GRID_W, jnp.arange(kn) % GRID_W
    col_start = jnp.clip(q_col - NA_COLS // 2, 0, GRID_W - NA_COLS)
    col_ok = (k_col[None, :] >= col_start[:, None]) & (k_col[None, :] < col_start[:, None] + NA_COLS)
    side = GRID_W - NA_COLS
    col_bias = _toeplitz(jnp.pad(rpb.astype(F32), ((0, 0), (0, 0), (side, side))), GRID_W, GRID_W)
    per_class = []
    for cls in range(len(reps)):
        j = classes.index(cls)
        q_abs = j * NA_QROWS + q_row
        k_abs = starts[j] + k_row
        row_start = jnp.clip(q_abs - kh // 2, 0, rows - kh)
        row_ok = (k_abs[None, :] >= row_start[:, None]) & (k_abs[None, :] < row_start[:, None] + kh)
        bias = jnp.concatenate([
            jnp.concatenate([
                col_bias[:, min(max(starts[j] + kr - (j * NA_QROWS + qr) + NA_MAX_ROWS - 1, 0), 2 * NA_MAX_ROWS - 2)]
                for kr in range(NA_KROWS)], axis=-1)
            for qr in range(NA_QROWS)], axis=-2)
        per_class.append(jnp.where((row_ok & col_ok)[None], bias, -jnp.inf))
    t = jnp.stack(per_class, axis=1)
    return t.reshape(B_HEADS // 2, 2, len(reps), qn, kn)
```

```python
import functools
import math

import jax
import jax.numpy as jnp
from jax import lax
from jax.experimental import pallas as pl
from jax.experimental.pallas import tpu as pltpu

D_MODEL = 1024
GRID_W = 64
A_HEADS = 4
A_DH = 64
B_HEADS = 8
B_DH = 64
NA_MAX_ROWS = 8
NA_COLS = 16
T5_BUCKETS = 32
T5_MAX_DIST = 128
C_HEADS = 4
C_DK = 128
C_DV = 256
C_GATE_RANK = 16
C_GATE_NORM = 16.0
C_CHUNK = 64
N_GROUPS = 4
EXP_PER_GROUP = 8
N_EXPERTS = N_GROUPS * EXP_PER_GROUP
D_EXPERT = 512
EPS = 1e-6

A_W = A_HEADS * 2 * A_DH
B_W = B_HEADS * B_DH
LANES = 128
FEAT_ROWS = D_MODEL // LANES
VMEM_LIMIT = 56 * 1024 * 1024

ROW_TILE = 512
ATT_TQ = 256
NA_QROWS = 4
NA_KROWS = 12
MOE_TILE = 256
ROUTE_ROWS = 8

F32 = jnp.float32
BF16 = jnp.bfloat16
NT_DIMS = (((1,), (1,)), ((), ()))
TN_DIMS = (((0,), (0,)), ((), ()))


def _cparams(sem):
    return pltpu.CompilerParams(dimension_semantics=sem, vmem_limit_bytes=VMEM_LIMIT)


def _load_token_tiles(ref, tm, lead=()):
    return jnp.concatenate(
        [ref[lead + (pl.ds(j, tm, stride=FEAT_ROWS), slice(None))] for j in range(FEAT_ROWS)], axis=1)


def _store_token_tiles(ref, val, lead=()):
    tm = val.shape[0]
    for j in range(FEAT_ROWS):
        ref[lead + (pl.ds(j, tm, stride=FEAT_ROWS), slice(None))] = val[:, j * LANES:(j + 1) * LANES]


def _rms(x, gain):
    return x * lax.rsqrt(jnp.mean(x * x, axis=-1, keepdims=True) + EPS) * gain


def _sigmoid(x):
    return 1.0 / (1.0 + jnp.exp(-x))


def _log_sigmoid(z):
    return jnp.minimum(z, 0.0) - jnp.log(1.0 + jnp.exp(-jnp.abs(z)))


def _in_proj_kernel(*refs, n_parts, gates):
    parts = refs[:n_parts]
    gain_ref, w_ref = refs[n_parts], refs[n_parts + 1]
    pos = n_parts + 2
    if gates:
        wg_ref = refs[pos]
        pos += 1
    outs = refs[pos:]
    x = parts[0][...]
    for p in parts[1:]:
        x = x + _load_token_tiles(p, x.shape[0])
    oi = 0
    if n_parts > 1:
        outs[0][...] = x
        oi = 1
    h = _rms(x, gain_ref[...]).astype(BF16)
    proj_ref = outs[oi]
    n_out = proj_ref.shape[1]
    step = 512
    for j in range(n_out // step):
        proj_ref[:, j * step:(j + 1) * step] = jnp.dot(
            h, w_ref[:, j * step:(j + 1) * step], preferred_element_type=F32).astype(BF16)
    if gates:
        outs[oi + 1][...] = jnp.dot(h, wg_ref[...], preferred_element_type=F32).astype(BF16)


def _row_spec(tm, width, block_off=0):
    return pl.BlockSpec((tm, width), lambda i, o=block_off: (i + o, 0))


def _full_spec(shape):
    nd = len(shape)
    return pl.BlockSpec(shape, lambda i, _nd=nd: (0,) * _nd)


def _in_proj_call(n, parts, gain, w, gates=None):
    tm = ROW_TILE
    n_parts = len(parts)
    n_out = w.shape[1]
    in_specs = [_row_spec(tm, D_MODEL)] + [_row_spec(tm * FEAT_ROWS, LANES, off // tm) for _, off in parts[1:]]
    args = [a for a, _ in parts]
    in_specs += [_full_spec((1, D_MODEL)), _full_spec(w.shape)]
    args += [gain.reshape(1, D_MODEL), w]
    out_shape, out_specs = [], []
    if n_parts > 1:
        out_shape.append(jax.ShapeDtypeStruct((n, D_MODEL), F32))
        out_specs.append(_row_spec(tm, D_MODEL))
    out_shape.append(jax.ShapeDtypeStruct((n, n_out), BF16))
    out_specs.append(_row_spec(tm, n_out))
    if gates is not None:
        in_specs.append(_full_spec(gates.shape))
        args.append(gates)
        out_shape.append(jax.ShapeDtypeStruct((n, LANES), BF16))
        out_specs.append(_row_spec(tm, LANES))
    return pl.pallas_call(
        functools.partial(_in_proj_kernel, n_parts=n_parts, gates=gates is not None),
        grid=(n // tm,),
        in_specs=in_specs,
        out_specs=out_specs,
        out_shape=out_shape,
        compiler_params=_cparams(("parallel",)),
        name="in_proj_gla" if gates is not None else "in_proj_attn",
    )(*args)


def _diff_attn_kernel(lam_ref, q_ref, k_ref, v_ref, bias_ref, gain_ref, o_ref, *, seq, out_scale):
    tq = q_ref.shape[0]
    qb = pl.program_id(2)
    q = q_ref[...]
    k = k_ref[...]
    lane = lax.broadcasted_iota(jnp.int32, q.shape, 1)
    scale = A_DH ** -0.5
    zero = jnp.zeros_like(q)
    off = pl.multiple_of((seq - tq) - qb * tq, LANES)
    bias = bias_ref[:, pl.ds(off, seq)]

    def exp_scores(qm):
        s = lax.dot_general(qm * scale, k, NT_DIMS, preferred_element_type=F32) + bias
        e = jnp.exp(s - jnp.max(s, axis=-1, keepdims=True))
        return e, jnp.sum(e, axis=-1, keepdims=True)

    e0, l0 = exp_scores(jnp.where(lane < A_DH, q, zero))
    e1, l1 = exp_scores(jnp.where(lane >= A_DH, q, zero))
    attn = (e0 - (lam_ref[0] * l0 / l1) * e1).astype(BF16)
    o = jnp.dot(attn, v_ref[...], preferred_element_type=F32) / l0
    o_ref[...] = (_rms(o, gain_ref[...]) * out_scale).astype(BF16)


def _t5_bucket(rel):
    half = T5_BUCKETS // 2
    max_exact = half // 2
    sign_off = jnp.where(rel > 0, half, 0)
    n = jnp.abs(rel)
    nf = jnp.maximum(n, 1).astype(F32)
    large = max_exact + (jnp.log(nf / max_exact) / math.log(T5_MAX_DIST / max_exact)
                         * (half - max_exact)).astype(jnp.int32)
    large = jnp.minimum(large, half - 1)
    return sign_off + jnp.where(n < max_exact, n, large)


def _t5_strip(t5_bias, seq, tq):
    n_rel = 2 * seq - 1
    vec = t5_bias[_t5_bucket(jnp.arange(n_rel) - (seq - 1))].astype(F32).T
    return _toeplitz(vec, tq, 2 * seq - tq)


def _toeplitz(vec, n_rows, n_cols):
    assert n_cols + n_rows - 1 <= vec.shape[-1]
    n = -(-max(vec.shape[-1] - 1, n_cols) // LANES) * LANES + 1
    vec = jnp.pad(vec, [(0, 0)] * (vec.ndim - 1) + [(0, n - vec.shape[-1])])
    lead = vec.shape[:-1]
    rolled = jnp.roll(vec, -(n_rows - 1), axis=-1)
    reps = -(-(n_rows * (n - 1)) // n)
    flat = jnp.broadcast_to(rolled[..., None, :], lead + (reps, n)).reshape(lead + (reps * n,))
    return flat[..., :n_rows * (n - 1)].reshape(lead + (n_rows, n - 1))[..., :n_cols]


def _diff_attn(proj3, lam, strip, subln, out_scale):
    b, seq, _ = proj3.shape
    tq = ATT_TQ
    kblk, vblk = A_W // LANES, 2 * A_W // LANES
    return pl.pallas_call(
        functools.partial(_diff_attn_kernel, seq=seq, out_scale=out_scale),
        grid=(A_HEADS, b, seq // tq),
        in_specs=[
            pl.BlockSpec(memory_space=pltpu.SMEM),
            pl.BlockSpec((None, tq, LANES), lambda h, i, j: (i, j, h)),
            pl.BlockSpec((None, seq, LANES), lambda h, i, j: (i, 0, kblk + h)),
            pl.BlockSpec((None, seq, LANES), lambda h, i, j: (i, 0, vblk + h)),
            pl.BlockSpec((None, tq, 2 * seq - tq), lambda h, i, j: (h, 0, 0)),
            pl.BlockSpec((1, LANES), lambda h, i, j: (0, 0)),
        ],
        out_specs=pl.BlockSpec((None, tq, LANES), lambda h, i, j: (i, j, h)),
        out_shape=jax.ShapeDtypeStruct((b, seq, A_W), BF16),
        compiler_params=_cparams(("parallel", "parallel", "parallel")),
        name="diff_attn",
    )(lam, proj3, proj3, proj3, strip, subln.reshape(1, LANES))


def _na_block_geometry(rows):
    nblk = rows // NA_QROWS
    kh = min(NA_MAX_ROWS, rows)
    starts, classes, reps = [], [], []
    for j in range(nblk):
        ks = min(max(j * NA_QROWS - kh // 2, 0), rows - NA_KROWS)
        rel = (j * NA_QROWS - ks,) + tuple(
            min(max(r - kh // 2, 0), rows - kh) - ks for r in range(j * NA_QROWS, (j + 1) * NA_QROWS))
        starts.append(ks)
        if rel not in reps:
            reps.append(rel)
        classes.append(reps.index(rel))
    return starts, classes, reps


def _na_kernel(q_ref, k_ref, v_ref, tab_ref, o_ref, *, starts, classes):
    qn, kn = NA_QROWS * GRID_W, NA_KROWS * GRID_W
    scale = B_DH ** -0.5
    lane = lax.broadcasted_iota(jnp.int32, (qn, LANES), 1)
    for j, (ks, cls) in enumerate(zip(starts, classes)):
        q = q_ref[j * qn:(j + 1) * qn, :]
        kw = k_ref[ks * GRID_W:ks * GRID_W + kn, :]
        vw = v_ref[ks * GRID_W:ks * GRID_W + kn, :]
        zero = jnp.zeros_like(q)
        outs = []
        for hl in range(2):
            in_head = (lane >= hl * B_DH) & (lane < (hl + 1) * B_DH)
            qm = jnp.where(in_head, q, zero) * scale
            s = lax.dot_general(qm, kw, NT_DIMS, preferred_element_type=F32) + tab_ref[hl, cls]
            m = jnp.max(s, axis=-1, keepdims=True)
            p = jnp.exp(s - m)
            denom = jnp.sum(p, axis=-1, keepdims=True)
            outs.append(jnp.dot(p.astype(BF16), vw, preferred_element_type=F32) / denom)
        o_ref[j * qn:(j + 1) * qn, :] = jnp.where(lane < B_DH, outs[0], outs[1]).astype(BF16)


def _na_attn(proj3, table, starts, classes):
    b, seq, _ = proj3.shape
    qblk = 3 * A_W // LANES
    kblk = qblk + B_W // LANES
    vblk = kblk + B_W // LANES
    ncls = table.shape[2]
    qn, kn = NA_QROWS * GRID_W, NA_KROWS * GRID_W
    return pl.pallas_call(
        functools.partial(_na_kernel, starts=tuple(starts), classes=tuple(classes)),
        grid=(B_HEADS // 2, b),
        in_specs=[
            pl.BlockSpec((None, seq, LANES), lambda h, i: (i, 0, qblk + h)),
            pl.BlockSpec((None, seq, LANES), lambda h, i: (i, 0, kblk + h)),
            pl.BlockSpec((None, seq, LANES), lambda h, i: (i, 0, vblk + h)),
            pl.BlockSpec((None, 2, ncls, qn, kn), lambda h, i: (h, 0, 0, 0, 0)),
        ],
        out_specs=pl.BlockSpec((None, seq, LANES), lambda h, i: (i, 0, h)),
        out_shape=jax.ShapeDtypeStruct((b, seq, B_W), BF16),
        compiler_params=_cparams(("parallel", "parallel")),
        name="na_attn",
    )(proj3, proj3, proj3, table)


def _out_proj_router_kernel(*refs, n_a):
    x_ref = refs[0]
    a_refs = refs[1:1 + n_a]
    w_refs = refs[1 + n_a:1 + 2 * n_a]
    gain_ref, wr_ref, br_ref, xo_ref, hn_ref, ids_ref, wts_ref = refs[1 + 2 * n_a:]
    acc = x_ref[...]
    for a, w in zip(a_refs, w_refs):
        acc = acc + jnp.dot(a[...], w[...], preferred_element_type=F32)
    xo_ref[...] = acc
    h = _rms(acc, gain_ref[...])
    _store_token_tiles(hn_ref, h)
    h_hi = h.astype(BF16)
    h_lo = (h - h_hi.astype(F32)).astype(BF16)
    hi = jnp.dot(h_hi, wr_ref[...], preferred_element_type=F32)
    logits = (hi[:, :LANES] + hi[:, LANES:]
              + jnp.dot(h_lo, wr_ref[:, :LANES], preferred_element_type=F32) + br_ref[...])
    lane = lax.broadcasted_iota(jnp.int32, logits.shape, 1)
    neg = jnp.float32(-jnp.inf)
    big = jnp.int32(LANES)

    def masked_softmax(mask):
        z = jnp.where(mask, logits, neg)
        e = jnp.exp(z - jnp.max(z, axis=-1, keepdims=True))
        return e / jnp.sum(e, axis=-1, keepdims=True)

    def top1(p, mask):
        w = jnp.max(jnp.where(mask, p, -1.0), axis=-1, keepdims=True)
        idx = jnp.min(jnp.where(mask & (p == w), lane, big), axis=-1, keepdims=True)
        return w, idx

    is_grp = lane < N_GROUPS
    g_w, g_idx = top1(masked_softmax(is_grp), is_grp)
    e_lane = lane - N_GROUPS
    in_grp = (e_lane >= g_idx * EXP_PER_GROUP) & (e_lane < (g_idx + 1) * EXP_PER_GROUP)
    p_e = masked_softmax(in_grp)
    w1, i1 = top1(p_e, in_grp)
    rest = in_grp & (lane != i1)
    w2, i2 = top1(p_e, rest)
    denom = w1 + w2
    ids = jnp.where(lane == 0, i1 - N_GROUPS, i2 - N_GROUPS)
    wts = jnp.where(lane == 0, g_w * (w1 / denom), g_w * (w2 / denom))
    ids_ref[...] = ids.T[:ids_ref.shape[0]]
    wts_ref[...] = wts.T[:wts_ref.shape[0]]


def _out_proj_router(n, x, acts, ws, gain, w_router, b_router):
    tm = ROW_TILE
    n_a = len(acts)
    in_specs = [_row_spec(tm, D_MODEL)]
    in_specs += [_row_spec(tm, a.shape[1]) for a in acts]
    in_specs += [_full_spec(w.shape) for w in ws]
    in_specs += [_full_spec((1, D_MODEL)), _full_spec(w_router.shape), _full_spec((1, LANES))]
    return pl.pallas_call(
        functools.partial(_out_proj_router_kernel, n_a=n_a),
        grid=(n // tm,),
        in_specs=in_specs,
        out_specs=[_row_spec(tm, D_MODEL), _row_spec(tm * FEAT_ROWS, LANES),
                   pl.BlockSpec((ROUTE_ROWS, tm), lambda i: (0, i)), pl.BlockSpec((ROUTE_ROWS, tm), lambda i: (0, i))],
        out_shape=[jax.ShapeDtypeStruct((n, D_MODEL), F32), jax.ShapeDtypeStruct((n * FEAT_ROWS, LANES), F32),
                   jax.ShapeDtypeStruct((ROUTE_ROWS, n), jnp.int32), jax.ShapeDtypeStruct((ROUTE_ROWS, n), F32)],
        compiler_params=_cparams(("parallel",)),
        name="out_proj_router",
    )(x, *acts, *ws, gain.reshape(1, D_MODEL), w_router, b_router)


def _router_params(w_grp, b_grp, w_exp, b_exp):
    pad = LANES - N_GROUPS - N_EXPERTS
    w = jnp.concatenate([w_grp, w_exp, jnp.zeros((D_MODEL, pad), F32)], axis=1)
    b = jnp.concatenate([b_grp, b_exp, jnp.zeros((pad,), F32)]).reshape(1, LANES)
    w_hi = w.astype(BF16)
    w_lo = (w - w_hi.astype(F32)).astype(BF16)
    return jnp.concatenate([w_hi, w_lo], axis=1), b


def _moe_plan(ids, wts, n):
    tm = MOE_TILE
    n_assign = 2 * n
    p_rows = n_assign + N_EXPERTS * tm
    e_flat = ids.reshape(-1)
    w_flat = wts.reshape(-1)
    idx_bits = max(n_assign - 1, 1).bit_length()
    assert idx_bits + N_EXPERTS.bit_length() <= 31
    key = jnp.sort((e_flat << idx_bits) | jnp.arange(n_assign, dtype=jnp.int32))
    order = key & ((1 << idx_bits) - 1)
    experts = jnp.arange(N_EXPERTS, dtype=jnp.int32)
    counts = jnp.sum((key >> idx_bits)[None, :] == experts[:, None], axis=1, dtype=jnp.int32)
    off = jnp.cumsum(counts) - counts
    padded = (counts + tm - 1) // tm * tm
    p_end = jnp.cumsum(padded)
    p_off = p_end - padded
    nt = p_rows // tm
    tile_start = jnp.arange(nt, dtype=jnp.int32) * tm
    tile_expert = jnp.minimum(jnp.sum(p_end[None, :] <= tile_start[:, None], axis=1, dtype=jnp.int32),
                              N_EXPERTS - 1)
    tile_count =jnp.clip(counts[tile_expert] - (tile_start - p_off[tile_expert]), 0, tm).astype(jnp.int32)
    n_used = (p_end[-1] // tm).astype(jnp.int32).reshape(1)
    in_tile = jnp.arange(tm, dtype=jnp.int32)[None, :]
    real = in_tile < tile_count[:, None]
    s_idx = jnp.clip((off[tile_expert] + tile_start - p_off[tile_expert])[:, None] + in_tile, 0, n_assign - 1)
    assign = jnp.where(real, order[s_idx], 0)
    src = (assign % n).reshape(p_rows)
    dest = assign.reshape(p_rows)
    w_row = jnp.where(real, w_flat[assign], 0.0).reshape(p_rows)
    return (src.reshape(nt, 1, tm), dest.reshape(nt, 1, tm), w_row.reshape(p_rows, 1), tile_expert,
            tile_count, n_used)


def _moe_kernel(te_ref, tc_ref, nu_ref, src_ref, dst_ref, wrow_ref, x_hbm, wg_ref, wu_ref, wd_ref, y_hbm,
                xbuf, ybuf, wgb, wub, wdb, gsem, ssem):
    tm = MOE_TILE
    i = pl.program_id(0)
    n_used = nu_ref[0]
    last = pl.num_programs(0) - 1

    def tile_rows(r):
        return pl.ds(pl.multiple_of(r * FEAT_ROWS, FEAT_ROWS), FEAT_ROWS)

    def gather_copy(slot, r, tok):
        return pltpu.make_async_copy(x_hbm.at[tile_rows(tok)], xbuf.at[slot, tile_rows(r)], gsem.at[slot])

    def scatter_copy(slot, r, row):
        return pltpu.make_async_copy(ybuf.at[slot, tile_rows(r)], y_hbm.at[tile_rows(row)], ssem.at[slot])

    def start_all(make):
        def body(r2, c):
            make(2 * r2).start(priority=0)
            make(2 * r2 + 1).start(priority=1)
            return c
        lax.fori_loop(0, tm // 2, body, 0, unroll=4)

    def start_rows(count, make):
        @pl.when(count == tm)
        def _():
            start_all(make)

        @pl.when(count < tm)
        def _():
            lax.fori_loop(0, count, lambda r, c: (make(r).start(), c)[1], 0)

    def wait_scatter(t):
        slot = t % 2
        count = tc_ref[t]

        @pl.when(count == tm)
        def _():
            pltpu.make_async_copy(ybuf.at[slot], y_hbm.at[pl.ds(0, tm * FEAT_ROWS)], ssem.at[slot]).wait()

        @pl.when(count < tm)
        def _():
            lax.fori_loop(0, count, lambda r, c: (scatter_copy(slot, r, 0).wait(), c)[1], 0)

    @pl.when(i < n_used)
    def _():
        slot = i % 2
        start_all(lambda r: gather_copy(slot, r, src_ref[0, r]))

    @pl.when((i >= 1) & (i <= n_used))
    def _():
        t = i - 1
        slot = t % 2
        pltpu.make_async_copy(x_hbm.at[pl.ds(0, tm * FEAT_ROWS)], xbuf.at[slot], gsem.at[slot]).wait()

        @pl.when(t >= 2)
        def _():
            wait_scatter(t - 2)

        @pl.when(jnp.logical_or(t == 0, te_ref[t] != te_ref[jnp.maximum(t - 1, 0)]))
        def _():
            wgb[...] = wg_ref[...].astype(BF16)
            wub[...] = wu_ref[...].astype(BF16)
            wdb[...] = wd_ref[...].astype(BF16)

        x = _load_token_tiles(xbuf, tm, lead=(slot,)).astype(BF16)
        g = jnp.dot(x, wgb[...], preferred_element_type=F32)
        u = jnp.dot(x, wub[...], preferred_element_type=F32)
        hid = (g * _sigmoid(g) * u).astype(BF16)
        y = jnp.dot(hid, wdb[...], preferred_element_type=F32)
        _store_token_tiles(ybuf, y * wrow_ref[...], lead=(slot,))

        start_rows(tc_ref[t], lambda r: scatter_copy(slot, r, dst_ref[0, r]))

    @pl.when(i == last)
    def _():
        wait_scatter(n_used - 1)

        @pl.when(n_used >= 2)
        def _():
            wait_scatter(n_used - 2)


def _moe_experts(n, hn, plan, w_gate, w_up, w_down, layer):
    tm = MOE_TILE
    src, dest, w_row, tile_expert, tile_count, n_used = plan
    nt = src.shape[0]
    prev = lambda i: jnp.maximum(i - 1, 0)
    grid_spec = pltpu.PrefetchScalarGridSpec(
        num_scalar_prefetch=3,
        grid=(nt + 1,),
        in_specs=[
            pl.BlockSpec((None, 1, tm), lambda i, te, tc, nu: (jnp.minimum(i, nt - 1), 0, 0), memory_space=pltpu.SMEM),
            pl.BlockSpec((None, 1, tm), lambda i, te, tc, nu: (prev(i), 0, 0), memory_space=pltpu.SMEM),
            pl.BlockSpec((tm, 1), lambda i, te, tc, nu: (prev(i), 0)),
            pl.BlockSpec(memory_space=pl.ANY),
            pl.BlockSpec((None, None, D_MODEL, D_EXPERT), lambda i, te, tc, nu: (layer, te[prev(i)], 0, 0)),
            pl.BlockSpec((None, None, D_MODEL, D_EXPERT), lambda i, te, tc, nu: (layer, te[prev(i)], 0, 0)),
            pl.BlockSpec((None, None, D_EXPERT, D_MODEL), lambda i, te, tc, nu: (layer, te[prev(i)], 0, 0)),
        ],
        out_specs=pl.BlockSpec(memory_space=pl.ANY),
        scratch_shapes=[
            pltpu.VMEM((2, tm * FEAT_ROWS, LANES), F32),
            pltpu.VMEM((2, tm * FEAT_ROWS, LANES), F32),
            pltpu.VMEM((D_MODEL, D_EXPERT), BF16),
            pltpu.VMEM((D_MODEL, D_EXPERT), BF16),
            pltpu.VMEM((D_EXPERT, D_MODEL), BF16),
            pltpu.SemaphoreType.DMA((2,)),
            pltpu.SemaphoreType.DMA((2,)),
        ],
    )
    return pl.pallas_call(
        _moe_kernel,
        grid_spec=grid_spec,
        out_shape=jax.ShapeDtypeStruct((2 * n * FEAT_ROWS, LANES), F32),
        compiler_params=_cparams(("arbitrary",)),
        name="moe_experts",
    )(tile_expert, tile_count, n_used, src, dest, w_row, hn, w_gate, w_up, w_down)


def _gla_kernel(q_ref, k_ref, v_ref, g_ref, gin_ref, wf_ref, bf_ref, wb_ref, bb_ref, gain_ref, o_ref,
                bf_s, bb_s, s_all, stf, stb, *, seq):
    c_len = C_CHUNK
    nc = seq // c_len
    scale = C_DK ** -0.5
    row = lax.broadcasted_iota(jnp.int32, (c_len, C_DK), 0)
    shifts = [1 << s for s in range(int(math.log2(c_len)))]

    def rows(c):
        return pl.ds(pl.multiple_of(c * c_len, c_len), c_len)

    gin = gin_ref[...]
    bf_s[...] = _log_sigmoid(jnp.dot(gin, wf_ref[...], preferred_element_type=F32) + bf_ref[...]) / C_GATE_NORM
    bb_s[...] = _log_sigmoid(jnp.dot(gin, wb_ref[...], preferred_element_type=F32) + bb_ref[...]) / C_GATE_NORM

    def cum_body(c, carry):
        y = bf_s[rows(c), :]
        for s in shifts:
            y = y + jnp.where(row >= s, pltpu.roll(y, s, 0), 0.0)
        bf_s[rows(c), :] = y
        y = bb_s[rows(c), :]
        for s in shifts:
            y = y + jnp.where(row < c_len - s, pltpu.roll(y, c_len - s, 0), 0.0)
        bb_s[rows(c), :] = y
        return carry

    lax.fori_loop(0, nc, cum_body, 0)

    stf[...] = jnp.zeros_like(stf)
    stb[...] = jnp.zeros_like(stb)

    def state_step(c, b_s, last_row, st, lane0):
        k = k_ref[rows(c), :].astype(F32)
        bc = b_s[rows(c), :]
        bl = bc[last_row:last_row + 1, :]
        kd = (k * jnp.exp(bl - bc)).astype(BF16)
        kv_t = lax.dot_general(v_ref[rows(c), :], kd, TN_DIMS, preferred_element_type=F32)
        s_all[c, :, lane0:lane0 + C_DK] = st[...].astype(BF16)
        st[...] = st[...] * jnp.exp(bl) + kv_t

    def state_body(i, carry):
        state_step(i, bf_s, c_len - 1, stf, 0)
        state_step(nc - 1 - i, bb_s, 0, stb, C_DK)
        return carry

    lax.fori_loop(0, nc, state_body, 0, unroll=8)

    ri = lax.broadcasted_iota(jnp.int32, (c_len, c_len), 0)
    ci = lax.broadcasted_iota(jnp.int32, (c_len, c_len), 1)

    def out_body(c, carry):
        q = q_ref[rows(c), :].astype(F32) * scale
        k = k_ref[rows(c), :].astype(F32)
        bcf = bf_s[rows(c), :]
        bcb = bb_s[rows(c), :]
        qf = (q * jnp.exp(bcf)).astype(BF16)
        kf = (k * jnp.exp(-bcf)).astype(BF16)
        qb = (q * jnp.exp(bcb)).astype(BF16)
        kb = (k * jnp.exp(-bcb)).astype(BF16)
        att_f = lax.dot_general(qf, kf, NT_DIMS, preferred_element_type=F32)
        att_b = lax.dot_general(qb, kb, NT_DIMS, preferred_element_type=F32)
        att = jnp.where(ri >= ci, att_f, att_b).astype(BF16)
        q_both = jnp.concatenate([qf, qb], axis=1)
        o = (jnp.dot(att, v_ref[rows(c), :], preferred_element_type=F32)
             + lax.dot_general(q_both, s_all[c], NT_DIMS, preferred_element_type=F32))
        g = g_ref[rows(c), :].astype(F32)
        o_ref[rows(c), :] = (_rms(o, gain_ref[...]) * (g * _sigmoid(g))).astype(BF16)
        return carry

    lax.fori_loop(0, nc, out_body, 0, unroll=16)


def _gla(proj3, gin3, wf, b_f, wb, b_b, out_norm):
    b, seq, _ = proj3.shape
    kw = C_HEADS * C_DK
    k_blk = kw // C_DK
    v_blk = 2 * kw // C_DV
    g_blk = v_blk + C_HEADS
    nc = seq // C_CHUNK
    return pl.pallas_call(
        functools.partial(_gla_kernel, seq=seq),
        grid=(b, C_HEADS),
        in_specs=[
            pl.BlockSpec((None, seq, C_DK), lambda i, h: (i, 0, h)),
            pl.BlockSpec((None, seq, C_DK), lambda i, h: (i, 0, k_blk + h)),
            pl.BlockSpec((None, seq, C_DV), lambda i, h: (i, 0, v_blk + h)),
            pl.BlockSpec((None, seq, C_DV), lambda i, h: (i, 0, g_blk + h)),
            pl.BlockSpec((None, seq, LANES), lambda i, h: (i, 0, 0)),
            pl.BlockSpec((LANES, C_DK), lambda i, h: (0, h)),
            pl.BlockSpec((1, C_DK), lambda i, h: (0, h)),
            pl.BlockSpec((LANES, C_DK), lambda i, h: (0, h)),
            pl.BlockSpec((1, C_DK), lambda i, h: (0, h)),
            pl.BlockSpec((1, C_DV), lambda i, h: (0, 0)),
        ],
        out_specs=pl.BlockSpec((None, seq, C_DV), lambda i, h: (i, 0, h)),
        out_shape=jax.ShapeDtypeStruct((b, seq, C_HEADS * C_DV), BF16),
        scratch_shapes=[
            pltpu.VMEM((seq, C_DK), F32),
            pltpu.VMEM((seq, C_DK), F32),
            pltpu.VMEM((nc, C_DV, 2 * C_DK), BF16),
            pltpu.VMEM((C_DV, C_DK), F32),
            pltpu.VMEM((C_DV, C_DK), F32),
        ],
        compiler_params=_cparams(("parallel", "parallel")),
        name="gla",
    )(proj3, proj3, proj3, proj3, gin3, wf, b_f, wb, b_b, out_norm.reshape(1, C_DV))


def _final_kernel(x_ref, y0_ref, y1_ref, gain_ref, o_ref):
    tm = x_ref.shape[0]
    x = x_ref[...] + _load_token_tiles(y0_ref, tm) + _load_token_tiles(y1_ref, tm)
    o_ref[...] = _rms(x, gain_ref[...])


def _final_norm(n, x, y, gain):
    tm = ROW_TILE
    return pl.pallas_call(
        _final_kernel,
        grid=(n // tm,),
        in_specs=[_row_spec(tm, D_MODEL), _row_spec(tm * FEAT_ROWS, LANES),
                  _row_spec(tm * FEAT_ROWS, LANES, n // tm), _full_spec((1, D_MODEL))],
        out_specs=_row_spec(tm, D_MODEL),
        out_shape=jax.ShapeDtypeStruct((n, D_MODEL), F32),
        compiler_params=_cparams(("parallel",)),
        name="final_norm",
    )(x, y, y, gain.reshape(1, D_MODEL))


def _moe_layer(n, hn, ids_pad, wts_pad, w_gate, w_up, w_down, layer):
    plan = _moe_plan(ids_pad[:2], wts_pad[:2], n)
    return _moe_experts(n, hn, plan, w_gate, w_up, w_down, layer)


def kernel(x, t5_bias, norm_mix, norm_ffn, norm_final, ev_w_in, ev_lambda, ev_subln, ev_rpb, ev_w_out,
           od_w_in, od_w_gk_fwd, od_b_gk_fwd, od_w_gk_bwd, od_b_gk_bwd, od_out_norm, od_w_out,
           moe_w_grp, moe_b_grp, moe_w_exp, moe_b_exp, moe_w_gate, moe_w_up, moe_w_down):
    b, seq, d = x.shape
    n = b * seq
    rows = seq // GRID_W
    assert d == D_MODEL and n % ROW_TILE == 0 and seq % ATT_TQ == 0 and seq % C_CHUNK == 0
    assert rows % NA_QROWS == 0 and rows >= NA_KROWS and MOE_TILE <= ROW_TILE
    x2 = x.reshape(n, d)

    (proj,) = _in_proj_call(n, [(x2, 0)], norm_mix[0], ev_w_in[0].astype(BF16))
    proj3 = proj.reshape(b, seq, proj.shape[1])
    lam_init = 0.8 - 0.6 * math.exp(-0.3 * 0)
    lp = ev_lambda[0].astype(F32)
    lam = (jnp.exp(jnp.sum(lp[0] * lp[1])) - jnp.exp(jnp.sum(lp[2] * lp[3])) + lam_init).reshape(1)
    strip = _t5_strip(t5_bias, seq, ATT_TQ)
    a_diff = _diff_attn(proj3, lam, strip, ev_subln[0], 1.0 - lam_init)

    starts, classes, _ = _na_block_geometry(rows)
    table = _na_table(ev_rpb[0], rows)
    a_na = _na_attn(proj3, table, starts, classes)

    w_out = ev_w_out[0].astype(BF16)
    w_r, b_r = _router_params(moe_w_grp[0], moe_b_grp[0], moe_w_exp[0], moe_b_exp[0])
    x1, hn, ids, wts = _out_proj_router(
        n, x2, [a_diff.reshape(n, A_W), a_na.reshape(n, B_W)], [w_out[:A_W], w_out[A_W:]],
        norm_ffn[0], w_r, b_r)
    y = _moe_layer(n, hn, ids, wts, moe_w_gate, moe_w_up, moe_w_down, 0)

    kw = C_HEADS * C_DK
    main_w = 2 * kw + 2 * C_HEADS * C_DV
    w_in = od_w_in[0]
    w_gates = jnp.pad(w_in[:, main_w:], ((0, 0), (0, LANES - 2 * C_GATE_RANK))).astype(BF16)
    wf = jnp.pad(od_w_gk_fwd[0], ((0, LANES - C_GATE_RANK), (0, 0))).astype(BF16)
    wb = jnp.pad(od_w_gk_bwd[0], ((C_GATE_RANK, LANES - 2 * C_GATE_RANK), (0, 0))).astype(BF16)
    x2b, proj, gin = _in_proj_call(
        n, [(x1, 0), (y, 0), (y, n)], norm_mix[1], w_in[:, :main_w].astype(BF16), gates=w_gates)
    a_gla = _gla(proj.reshape(b, seq, main_w), gin.reshape(b, seq, LANES), wf, od_b_gk_fwd[0].reshape(1, kw),
                 wb, od_b_gk_bwd[0].reshape(1, kw), od_out_norm[0])

    w_r, b_r = _router_params(moe_w_grp[1], moe_b_grp[1], moe_w_exp[1], moe_b_exp[1])
    x3, hn, ids, wts = _out_proj_router(
        n, x2b, [a_gla.reshape(n, C_HEADS * C_DV)], [od_w_out[0].astype(BF16)], norm_ffn[1], w_r, b_r)
    y = _moe_layer(n, hn, ids, wts, moe_w_gate, moe_w_up, moe_w_down, 1)

    return _final_norm(n, x3, y, norm_final).reshape(b, seq, d)


def _na_table(rpb, rows):
    kh = min(NA_MAX_ROWS, rows)
    starts, classes, reps = _na_block_geometry(rows)
    qn, kn = NA_QROWS * GRID_W, NA_KROWS * GRID_W
    q_row, q_col = jnp.arange(qn) // GRID_W, jnp.arange(qn) % GRID_W
    k_row, k_col = jnp.arange(kn) // GRID_W, jnp.arange(kn) % GRID_W
    col_start = jnp.clip(q_col - NA_COLS // 2, 0, GRID_W - NA_COLS)
    col_ok = (k_col[None, :] >= col_start[:, None]) & (k_col[None, :] < col_start[:, None] + NA_COLS)
    side = GRID_W - NA_COLS
    col_bias = _toeplitz(jnp.pad(rpb.astype(F32), ((0, 0), (0, 0), (side, side))), GRID_W, GRID_W)
    per_class = []
    for cls in range(len(reps)):
        j = classes.index(cls)
        q_abs = j * NA_QROWS + q_row
        k_abs = starts[j] + k_row
        row_start = jnp.clip(q_abs - kh // 2, 0, rows - kh)
        row_ok = (k_abs[None, :] >= row_start[:, None]) & (k_abs[None, :] < row_start[:, None] + kh)
        dr = jnp.clip(starts[j] + jnp.arange(NA_KROWS)[None, :] - (j * NA_QROWS + jnp.arange(NA_QROWS))[:, None]
                      + NA_MAX_ROWS - 1, 0, 2 * NA_MAX_ROWS - 2)
        blocks = col_bias[:, dr]
        bias = jnp.transpose(blocks, (0, 1, 3, 2, 4)).reshape(B_HEADS, qn, kn)
        per_class.append(jnp.where((row_ok & col_ok)[None], bias, -jnp.inf))
    t = jnp.stack(per_class, axis=1)
    return t.reshape(B_HEADS // 2, 2, len(reps), qn, kn)
```

```python
import functools
import math

import jax
import jax.numpy as jnp
from jax import lax
from jax.experimental import pallas as pl
from jax.experimental.pallas import tpu as pltpu

D_MODEL = 1024
GRID_W = 64
A_HEADS = 4
A_DH = 64
B_HEADS = 8
B_DH = 64
NA_MAX_ROWS = 8
NA_COLS = 16
T5_BUCKETS = 32
T5_MAX_DIST = 128
C_HEADS = 4
C_DK = 128
C_DV = 256
C_GATE_RANK = 16
C_GATE_NORM = 16.0
C_CHUNK = 64
N_GROUPS = 4
EXP_PER_GROUP = 8
N_EXPERTS = N_GROUPS * EXP_PER_GROUP
D_EXPERT = 512
EPS = 1e-6

A_W = A_HEADS * 2 * A_DH
B_W = B_HEADS * B_DH
LANES = 128
FEAT_ROWS = D_MODEL // LANES
VMEM_LIMIT = 56 * 1024 * 1024

ROW_TILE = 512
ATT_TQ = 256
NA_QROWS = 4
NA_KROWS = 12
MOE_TILE = 256
ROUTE_ROWS = 8

F32 = jnp.float32
BF16 = jnp.bfloat16
NT_DIMS = (((1,), (1,)), ((), ()))
TN_DIMS = (((0,), (0,)), ((), ()))


def _cparams(sem):
    return pltpu.CompilerParams(dimension_semantics=sem, vmem_limit_bytes=VMEM_LIMIT)


def _load_token_tiles(ref, tm, lead=()):
    return jnp.concatenate(
        [ref[lead + (pl.ds(j, tm, stride=FEAT_ROWS), slice(None))] for j in range(FEAT_ROWS)], axis=1)


def _store_token_tiles(ref, val, lead=()):
    tm = val.shape[0]
    for j in range(FEAT_ROWS):
        ref[lead + (pl.ds(j, tm, stride=FEAT_ROWS), slice(None))] = val[:, j * LANES:(j + 1) * LANES]


def _rms(x, gain):
    return x * lax.rsqrt(jnp.mean(x * x, axis=-1, keepdims=True) + EPS) * gain


def _sigmoid(x):
    return 1.0 / (1.0 + jnp.exp(-x))


def _log_sigmoid(z):
    return jnp.minimum(z, 0.0) - jnp.log(1.0 + jnp.exp(-jnp.abs(z)))


def _in_proj_kernel(*refs, n_parts, gates):
    parts = refs[:n_parts]
    gain_ref, w_ref = refs[n_parts], refs[n_parts + 1]
    pos = n_parts + 2
    if gates:
        wg_ref = refs[pos]
        pos += 1
    outs = refs[pos:]
    x = parts[0][...]
    for p in parts[1:]:
        x = x + _load_token_tiles(p, x.shape[0])
    oi = 0
    if n_parts > 1:
        outs[0][...] = x
        oi = 1
    h = _rms(x, gain_ref[...]).astype(BF16)
    proj_ref = outs[oi]
    n_out = proj_ref.shape[1]
    step = 512
    for j in range(n_out // step):
        proj_ref[:, j * step:(j + 1) * step] = jnp.dot(
            h, w_ref[:, j * step:(j + 1) * step], preferred_element_type=F32).astype(BF16)
    if gates:
        outs[oi + 1][...] = jnp.dot(h, wg_ref[...], preferred_element_type=F32).astype(BF16)


def _row_spec(tm, width, block_off=0):
    return pl.BlockSpec((tm, width), lambda i, o=block_off: (i + o, 0))


def _full_spec(shape):
    nd = len(shape)
    return pl.BlockSpec(shape, lambda i, _nd=nd: (0,) * _nd)


def _in_proj_call(n, parts, gain, w, gates=None):
    tm = ROW_TILE
    n_parts = len(parts)
    n_out = w.shape[1]
    in_specs = [_row_spec(tm, D_MODEL)] + [_row_spec(tm * FEAT_ROWS, LANES, off // tm) for _, off in parts[1:]]
    args = [a for a, _ in parts]
    in_specs += [_full_spec((1, D_MODEL)), _full_spec(w.shape)]
    args += [gain.reshape(1, D_MODEL), w]
    out_shape, out_specs = [], []
    if n_parts > 1:
        out_shape.append(jax.ShapeDtypeStruct((n, D_MODEL), F32))
        out_specs.append(_row_spec(tm, D_MODEL))
    out_shape.append(jax.ShapeDtypeStruct((n, n_out), BF16))
    out_specs.append(_row_spec(tm, n_out))
    if gates is not None:
        in_specs.append(_full_spec(gates.shape))
        args.append(gates)
        out_shape.append(jax.ShapeDtypeStruct((n, LANES), BF16))
        out_specs.append(_row_spec(tm, LANES))
    return pl.pallas_call(
        functools.partial(_in_proj_kernel, n_parts=n_parts, gates=gates is not None),
        grid=(n // tm,),
        in_specs=in_specs,
        out_specs=out_specs,
        out_shape=out_shape,
        compiler_params=_cparams(("parallel",)),
        name="in_proj_gla" if gates is not None else "in_proj_attn",
    )(*args)


def _diff_attn_kernel(lam_ref, q_ref, k_ref, v_ref, bias_ref, gain_ref, o_ref, *, seq, out_scale):
    tq = q_ref.shape[0]
    qb = pl.program_id(2)
    q = q_ref[...]
    k = k_ref[...]
    lane = lax.broadcasted_iota(jnp.int32, q.shape, 1)
    scale = A_DH ** -0.5
    zero = jnp.zeros_like(q)
    off = pl.multiple_of((seq - tq) - qb * tq, LANES)
    bias = bias_ref[:, pl.ds(off, seq)]

    def exp_scores(qm):
        s = lax.dot_general(qm * scale, k, NT_DIMS, preferred_element_type=F32) + bias
        e = jnp.exp(s - jnp.max(s, axis=-1, keepdims=True))
        return e, jnp.sum(e, axis=-1, keepdims=True)

    e0, l0 = exp_scores(jnp.where(lane < A_DH, q, zero))
    e1, l1 = exp_scores(jnp.where(lane >= A_DH, q, zero))
    attn = (e0 - (lam_ref[0] * l0 / l1) * e1).astype(BF16)
    o = jnp.dot(attn, v_ref[...], preferred_element_type=F32) / l0
    o_ref[...] = (_rms(o, gain_ref[...]) * out_scale).astype(BF16)


def _t5_bucket(rel):
    half = T5_BUCKETS // 2
    max_exact = half // 2
    sign_off = jnp.where(rel > 0, half, 0)
    n = jnp.abs(rel)
    nf = jnp.maximum(n, 1).astype(F32)
    large = max_exact + (jnp.log(nf / max_exact) / math.log(T5_MAX_DIST / max_exact)
                         * (half - max_exact)).astype(jnp.int32)
    large = jnp.minimum(large, half - 1)
    return sign_off + jnp.where(n < max_exact, n, large)


def _t5_strip(t5_bias, seq, tq):
    n_rel = 2 * seq - 1
    vec = t5_bias[_t5_bucket(jnp.arange(n_rel) - (seq - 1))].astype(F32).T
    return _toeplitz(vec, tq, 2 * seq - tq)


def _toeplitz(vec, n_rows, n_cols):
    n = vec.shape[-1]
    lead = vec.shape[:-1]
    assert n_cols + n_rows - 1 <= n and n_cols <= n - 1
    rolled = jnp.roll(vec, -(n_rows - 1), axis=-1)
    reps = -(-(n_rows * (n - 1)) // n)
    flat = jnp.broadcast_to(rolled[..., None, :], lead + (reps, n)).reshape(lead + (reps * n,))
    return flat[..., :n_rows * (n - 1)].reshape(lead + (n_rows, n - 1))[..., :n_cols]


def _diff_attn(proj3, lam, strip, subln, out_scale):
    b, seq, _ = proj3.shape
    tq = ATT_TQ
    kblk, vblk = A_W // LANES, 2 * A_W // LANES
    return pl.pallas_call(
        functools.partial(_diff_attn_kernel, seq=seq, out_scale=out_scale),
        grid=(A_HEADS, b, seq // tq),
        in_specs=[
            pl.BlockSpec(memory_space=pltpu.SMEM),
            pl.BlockSpec((None, tq, LANES), lambda h, i, j: (i, j, h)),
            pl.BlockSpec((None, seq, LANES), lambda h, i, j: (i, 0, kblk + h)),
            pl.BlockSpec((None, seq, LANES), lambda h, i, j: (i, 0, vblk + h)),
            pl.BlockSpec((None, tq, 2 * seq - tq), lambda h, i, j: (h, 0, 0)),
            pl.BlockSpec((1, LANES), lambda h, i, j: (0, 0)),
        ],
        out_specs=pl.BlockSpec((None, tq, LANES), lambda h, i, j: (i, j, h)),
        out_shape=jax.ShapeDtypeStruct((b, seq, A_W), BF16),
        compiler_params=_cparams(("parallel", "parallel", "parallel")),
        name="diff_attn",
    )(lam, proj3, proj3, proj3, strip, subln.reshape(1, LANES))


def _na_block_geometry(rows):
    nblk = rows // NA_QROWS
    kh = min(NA_MAX_ROWS, rows)
    starts, classes, reps = [], [], []
    for j in range(nblk):
        ks = min(max(j * NA_QROWS - kh // 2, 0), rows - NA_KROWS)
        rel = (j * NA_QROWS - ks,) + tuple(
            min(max(r - kh // 2, 0), rows - kh) - ks for r in range(j * NA_QROWS, (j + 1) * NA_QROWS))
        starts.append(ks)
        if rel not in reps:
            reps.append(rel)
        classes.append(reps.index(rel))
    return starts, classes, reps


def _na_kernel(q_ref, k_ref, v_ref, tab_ref, o_ref, *, starts, classes):
    qn, kn = NA_QROWS * GRID_W, NA_KROWS * GRID_W
    scale = B_DH ** -0.5
    lane = lax.broadcasted_iota(jnp.int32, (qn, LANES), 1)
    for j, (ks, cls) in enumerate(zip(starts, classes)):
        q = q_ref[j * qn:(j + 1) * qn, :]
        kw = k_ref[ks * GRID_W:ks * GRID_W + kn, :]
        vw = v_ref[ks * GRID_W:ks * GRID_W + kn, :]
        zero = jnp.zeros_like(q)
        outs = []
        for hl in range(2):
            in_head = (lane >= hl * B_DH) & (lane < (hl + 1) * B_DH)
            qm = jnp.where(in_head, q, zero) * scale
            s = lax.dot_general(qm, kw, NT_DIMS, preferred_element_type=F32) + tab_ref[hl, cls]
            m = jnp.max(s, axis=-1, keepdims=True)
            p = jnp.exp(s - m)
            denom = jnp.sum(p, axis=-1, keepdims=True)
            outs.append(jnp.dot(p.astype(BF16), vw, preferred_element_type=F32) / denom)
        o_ref[j * qn:(j + 1) * qn, :] = jnp.where(lane < B_DH, outs[0], outs[1]).astype(BF16)


def _na_attn(proj3, table, starts, classes):
    b, seq, _ = proj3.shape
    qblk = 3 * A_W // LANES
    kblk = qblk + B_W // LANES
    vblk = kblk + B_W // LANES
    ncls = table.shape[2]
    qn, kn = NA_QROWS * GRID_W, NA_KROWS * GRID_W
    return pl.pallas_call(
        functools.partial(_na_kernel, starts=tuple(starts), classes=tuple(classes)),
        grid=(B_HEADS // 2, b),
        in_specs=[
            pl.BlockSpec((None, seq, LANES), lambda h, i: (i, 0, qblk + h)),
            pl.BlockSpec((None, seq, LANES), lambda h, i: (i, 0, kblk + h)),
            pl.BlockSpec((None, seq, LANES), lambda h, i: (i, 0, vblk + h)),
            pl.BlockSpec((None, 2, ncls, qn, kn), lambda h, i: (h, 0, 0, 0, 0)),
        ],
        out_specs=pl.BlockSpec((None, seq, LANES), lambda h, i: (i, 0, h)),
        out_shape=jax.ShapeDtypeStruct((b, seq, B_W), BF16),
        compiler_params=_cparams(("parallel", "parallel")),
        name="na_attn",
    )(proj3, proj3, proj3, table)


def _out_proj_router_kernel(*refs, n_a):
    x_ref = refs[0]
    a_refs = refs[1:1 + n_a]
    w_refs = refs[1 + n_a:1 + 2 * n_a]
    gain_ref, wr_ref, br_ref, xo_ref, hn_ref, ids_ref, wts_ref = refs[1 + 2 * n_a:]
    acc = x_ref[...]
    for a, w in zip(a_refs, w_refs):
        acc = acc + jnp.dot(a[...], w[...], preferred_element_type=F32)
    xo_ref[...] = acc
    h = _rms(acc, gain_ref[...])
    _store_token_tiles(hn_ref, h)
    h_hi = h.astype(BF16)
    h_lo = (h - h_hi.astype(F32)).astype(BF16)
    hi = jnp.dot(h_hi, wr_ref[...], preferred_element_type=F32)
    logits = (hi[:, :LANES] + hi[:, LANES:]
              + jnp.dot(h_lo, wr_ref[:, :LANES], preferred_element_type=F32) + br_ref[...])
    lane = lax.broadcasted_iota(jnp.int32, logits.shape, 1)
    neg = jnp.float32(-jnp.inf)
    big = jnp.int32(LANES)

    def masked_softmax(mask):
        z = jnp.where(mask, logits, neg)
        e = jnp.exp(z - jnp.max(z, axis=-1, keepdims=True))
        return e / jnp.sum(e, axis=-1, keepdims=True)

    def top1(p, mask):
        w = jnp.max(jnp.where(mask, p, -1.0), axis=-1, keepdims=True)
        idx = jnp.min(jnp.where(mask & (p == w), lane, big), axis=-1, keepdims=True)
        return w, idx

    is_grp = lane < N_GROUPS
    g_w, g_idx = top1(masked_softmax(is_grp), is_grp)
    e_lane = lane - N_GROUPS
    in_grp = (e_lane >= g_idx * EXP_PER_GROUP) & (e_lane < (g_idx + 1) * EXP_PER_GROUP)
    p_e = masked_softmax(in_grp)
    w1, i1 = top1(p_e, in_grp)
    rest = in_grp & (lane != i1)
    w2, i2 = top1(p_e, rest)
    denom = w1 + w2
    ids = jnp.where(lane == 0, i1 - N_GROUPS, i2 - N_GROUPS)
    wts = jnp.where(lane == 0, g_w * (w1 / denom), g_w * (w2 / denom))
    ids_ref[...] = ids.T[:ids_ref.shape[0]]
    wts_ref[...] = wts.T[:wts_ref.shape[0]]


def _out_proj_router(n, x, acts, ws, gain, w_router, b_router):
    tm = ROW_TILE
    n_a = len(acts)
    in_specs = [_row_spec(tm, D_MODEL)]
    in_specs += [_row_spec(tm, a.shape[1]) for a in acts]
    in_specs += [_full_spec(w.shape) for w in ws]
    in_specs += [_full_spec((1, D_MODEL)), _full_spec(w_router.shape), _full_spec((1, LANES))]
    return pl.pallas_call(
        functools.partial(_out_proj_router_kernel, n_a=n_a),
        grid=(n // tm,),
        in_specs=in_specs,
        out_specs=[_row_spec(tm, D_MODEL), _row_spec(tm * FEAT_ROWS, LANES),
                   pl.BlockSpec((ROUTE_ROWS, tm), lambda i: (0, i)), pl.BlockSpec((ROUTE_ROWS, tm), lambda i: (0, i))],
        out_shape=[jax.ShapeDtypeStruct((n, D_MODEL), F32), jax.ShapeDtypeStruct((n * FEAT_ROWS, LANES), F32),
                   jax.ShapeDtypeStruct((ROUTE_ROWS, n), jnp.int32), jax.ShapeDtypeStruct((ROUTE_ROWS, n), F32)],
        compiler_params=_cparams(("parallel",)),
        name="out_proj_router",
    )(x, *acts, *ws, gain.reshape(1, D_MODEL), w_router, b_router)


def _router_params(w_grp, b_grp, w_exp, b_exp):
    pad = LANES - N_GROUPS - N_EXPERTS
    w = jnp.concatenate([w_grp, w_exp, jnp.zeros((D_MODEL, pad), F32)], axis=1)
    b = jnp.concatenate([b_grp, b_exp, jnp.zeros((pad,), F32)]).reshape(1, LANES)
    w_hi = w.astype(BF16)
    w_lo = (w - w_hi.astype(F32)).astype(BF16)
    return jnp.concatenate([w_hi, w_lo], axis=1), b


def _moe_plan(ids, wts, n):
    tm = MOE_TILE
    n_assign = 2 * n
    p_rows = n_assign + N_EXPERTS * tm
    e_flat = ids.reshape(-1)
    w_flat = wts.reshape(-1)
    idx_bits = max(n_assign - 1, 1).bit_length()
    assert idx_bits + N_EXPERTS.bit_length() <= 31
    key = jnp.sort((e_flat << idx_bits) | jnp.arange(n_assign, dtype=jnp.int32))
    order = key & ((1 << idx_bits) - 1)
    experts = jnp.arange(N_EXPERTS, dtype=jnp.int32)
    counts = jnp.sum((key >> idx_bits)[None, :] == experts[:, None], axis=1, dtype=jnp.int32)
    off = jnp.cumsum(counts) - counts
    padded = (counts + tm - 1) // tm * tm
    p_end = jnp.cumsum(padded)
    p_off = p_end - padded
    nt = p_rows // tm
    tile_start = jnp.arange(nt, dtype=jnp.int32) * tm
    tile_expert = jnp.minimum(jnp.sum(p_end[None, :] <= tile_start[:, None], axis=1, dtype=jnp.int32),
                              N_EXPERTS - 1)
    tile_count = jnp.clip(counts[tile_expert] - (tile_start - p_off[tile_expert]), 0, tm).astype(jnp.int32)
    n_used = (p_end[-1] // tm).astype(jnp.int32).reshape(1)
    in_tile = jnp.arange(tm, dtype=jnp.int32)[None, :]
    real = in_tile < tile_count[:, None]
    s_idx = jnp.clip((off[tile_expert] + tile_start - p_off[tile_expert])[:, None] + in_tile, 0, n_assign - 1)
    assign = jnp.where(real, order[s_idx], 0)
    src = (assign % n).reshape(p_rows)
    dest = assign.reshape(p_rows)
    w_row = jnp.where(real, w_flat[assign], 0.0).reshape(p_rows)
    return (src.reshape(nt, 1, tm), dest.reshape(nt, 1, tm), w_row.reshape(p_rows, 1), tile_expert,
            tile_count, n_used)


def _moe_kernel(te_ref, tc_ref, nu_ref, src_ref, dst_ref, wrow_ref, x_hbm, wg_ref, wu_ref, wd_ref, y_hbm,
                xbuf, ybuf, wgb, wub, wdb, gsem, ssem):
    tm = MOE_TILE
    i = pl.program_id(0)
    n_used = nu_ref[0]
    last = pl.num_programs(0) - 1

    def tile_rows(r):
        return pl.ds(pl.multiple_of(r * FEAT_ROWS, FEAT_ROWS), FEAT_ROWS)

    def gather_copy(slot, r, tok):
        return pltpu.make_async_copy(x_hbm.at[tile_rows(tok)], xbuf.at[slot, tile_rows(r)], gsem.at[slot])

    def scatter_copy(slot, r, row):
        return pltpu.make_async_copy(ybuf.at[slot, tile_rows(r)], y_hbm.at[tile_rows(row)], ssem.at[slot])

    def start_all(make):
        def body(r2, c):
            make(2 * r2).start(priority=0)
            make(2 * r2 + 1).start(priority=1)
            return c
        lax.fori_loop(0, tm // 2, body, 0, unroll=4)

    def start_rows(count, make):
        @pl.when(count == tm)
        def _():
            start_all(make)

        @pl.when(count < tm)
        def _():
            lax.fori_loop(0, count, lambda r, c: (make(r).start(), c)[1], 0)

    def wait_scatter(t):
        slot = t % 2
        count = tc_ref[t]

        @pl.when(count == tm)
        def _():
            pltpu.make_async_copy(ybuf.at[slot], y_hbm.at[pl.ds(0, tm * FEAT_ROWS)], ssem.at[slot]).wait()

        @pl.when(count < tm)
        def _():
            lax.fori_loop(0, count, lambda r, c: (scatter_copy(slot, r, 0).wait(), c)[1], 0)

    @pl.when(i < n_used)
    def _():
        slot = i % 2
        start_all(lambda r: gather_copy(slot, r, src_ref[0, r]))

    @pl.when((i >= 1) & (i <= n_used))
    def _():
        t = i - 1
        slot = t % 2
        pltpu.make_async_copy(x_hbm.at[pl.ds(0, tm * FEAT_ROWS)], xbuf.at[slot], gsem.at[slot]).wait()

        @pl.when(t >= 2)
        def _():
            wait_scatter(t - 2)

        @pl.when(jnp.logical_or(t == 0, te_ref[t] != te_ref[jnp.maximum(t - 1, 0)]))
        def _():
            wgb[...] = wg_ref[...].astype(BF16)
            wub[...] = wu_ref[...].astype(BF16)
            wdb[...] = wd_ref[...].astype(BF16)

        x = _load_token_tiles(xbuf, tm, lead=(slot,)).astype(BF16)
        g = jnp.dot(x, wgb[...], preferred_element_type=F32)
        u = jnp.dot(x, wub[...], preferred_element_type=F32)
        hid = (g * _sigmoid(g) * u).astype(BF16)
        y = jnp.dot(hid, wdb[...], preferred_element_type=F32)
        _store_token_tiles(ybuf, y * wrow_ref[...], lead=(slot,))

        start_rows(tc_ref[t], lambda r: scatter_copy(slot, r, dst_ref[0, r]))

    @pl.when(i == last)
    def _():
        wait_scatter(n_used - 1)

        @pl.when(n_used >= 2)
        def _():
            wait_scatter(n_used - 2)


def _moe_experts(n, hn, plan, w_gate, w_up, w_down, layer):
    tm = MOE_TILE
    src, dest, w_row, tile_expert, tile_count, n_used = plan
    nt = src.shape[0]
    prev = lambda i: jnp.maximum(i - 1, 0)
    grid_spec = pltpu.PrefetchScalarGridSpec(
        num_scalar_prefetch=3,
        grid=(nt + 1,),
        in_specs=[
            pl.BlockSpec((None, 1, tm), lambda i, te, tc, nu: (jnp.minimum(i, nt - 1), 0, 0), memory_space=pltpu.SMEM),
            pl.BlockSpec((None, 1, tm), lambda i, te, tc, nu: (prev(i), 0, 0), memory_space=pltpu.SMEM),
            pl.BlockSpec((tm, 1), lambda i, te, tc, nu: (prev(i), 0)),
            pl.BlockSpec(memory_space=pl.ANY),
            pl.BlockSpec((None, None, D_MODEL, D_EXPERT), lambda i, te, tc, nu: (layer, te[prev(i)], 0, 0)),
            pl.BlockSpec((None, None, D_MODEL, D_EXPERT), lambda i, te, tc, nu: (layer, te[prev(i)], 0, 0)),
            pl.BlockSpec((None, None, D_EXPERT, D_MODEL), lambda i, te, tc, nu: (layer, te[prev(i)], 0, 0)),
        ],
        out_specs=pl.BlockSpec(memory_space=pl.ANY),
        scratch_shapes=[
            pltpu.VMEM((2, tm * FEAT_ROWS, LANES), F32),
            pltpu.VMEM((2, tm * FEAT_ROWS, LANES), F32),
            pltpu.VMEM((D_MODEL, D_EXPERT), BF16),
            pltpu.VMEM((D_MODEL, D_EXPERT), BF16),
            pltpu.VMEM((D_EXPERT, D_MODEL), BF16),
            pltpu.SemaphoreType.DMA((2,)),
            pltpu.SemaphoreType.DMA((2,)),
        ],
    )
    return pl.pallas_call(
        _moe_kernel,
        grid_spec=grid_spec,
        out_shape=jax.ShapeDtypeStruct((2 * n * FEAT_ROWS, LANES), F32),
        compiler_params=_cparams(("arbitrary",)),
        name="moe_experts",
    )(tile_expert, tile_count, n_used, src, dest, w_row, hn, w_gate, w_up, w_down)


def _gla_kernel(q_ref, k_ref, v_ref, g_ref, gin_ref, wf_ref, bf_ref, wb_ref, bb_ref, gain_ref, o_ref,
                bf_s, bb_s, s_all, stf, stb, *, seq):
    c_len = C_CHUNK
    nc = seq // c_len
    scale = C_DK ** -0.5
    row = lax.broadcasted_iota(jnp.int32, (c_len, C_DK), 0)
    shifts = [1 << s for s in range(int(math.log2(c_len)))]

    def rows(c):
        return pl.ds(pl.multiple_of(c * c_len, c_len), c_len)

    gin = gin_ref[...]
    bf_s[...] = _log_sigmoid(jnp.dot(gin, wf_ref[...], preferred_element_type=F32) + bf_ref[...]) / C_GATE_NORM
    bb_s[...] = _log_sigmoid(jnp.dot(gin, wb_ref[...], preferred_element_type=F32) + bb_ref[...]) / C_GATE_NORM

    def cum_body(c, carry):
        y = bf_s[rows(c), :]
        for s in shifts:
            y = y + jnp.where(row >= s, pltpu.roll(y, s, 0), 0.0)
        bf_s[rows(c), :] = y
        y = bb_s[rows(c), :]
        for s in shifts:
            y = y + jnp.where(row < c_len - s, pltpu.roll(y, c_len - s, 0), 0.0)
        bb_s[rows(c), :] = y
        return carry

    lax.fori_loop(0, nc, cum_body, 0)

    stf[...] = jnp.zeros_like(stf)
    stb[...] = jnp.zeros_like(stb)

    def state_step(c, b_s, last_row, st, lane0):
        k = k_ref[rows(c), :].astype(F32)
        bc = b_s[rows(c), :]
        bl = bc[last_row:last_row + 1, :]
        kd = (k * jnp.exp(bl - bc)).astype(BF16)
        kv_t = lax.dot_general(v_ref[rows(c), :], kd, TN_DIMS, preferred_element_type=F32)
        s_all[c, :, lane0:lane0 + C_DK] = st[...].astype(BF16)
        st[...] = st[...] * jnp.exp(bl) + kv_t

    def state_body(i, carry):
        state_step(i, bf_s, c_len - 1, stf, 0)
        state_step(nc - 1 - i, bb_s, 0, stb, C_DK)
        return carry

    lax.fori_loop(0, nc, state_body, 0, unroll=8)

    ri = lax.broadcasted_iota(jnp.int32, (c_len, c_len), 0)
    ci = lax.broadcasted_iota(jnp.int32, (c_len, c_len), 1)

    def out_body(c, carry):
        q = q_ref[rows(c), :].astype(F32) * scale
        k = k_ref[rows(c), :].astype(F32)
        bcf = bf_s[rows(c), :]
        bcb = bb_s[rows(c), :]
        qf = (q * jnp.exp(bcf)).astype(BF16)
        kf = (k * jnp.exp(-bcf)).astype(BF16)
        qb = (q * jnp.exp(bcb)).astype(BF16)
        kb = (k * jnp.exp(-bcb)).astype(BF16)
        att_f = lax.dot_general(qf, kf, NT_DIMS, preferred_element_type=F32)
        att_b = lax.dot_general(qb, kb, NT_DIMS, preferred_element_type=F32)
        att = jnp.where(ri >= ci, att_f, att_b).astype(BF16)
        q_both = jnp.concatenate([qf, qb], axis=1)
        o = (jnp.dot(att, v_ref[rows(c), :], preferred_element_type=F32)
             + lax.dot_general(q_both, s_all[c], NT_DIMS, preferred_element_type=F32))
        g = g_ref[rows(c), :].astype(F32)
        o_ref[rows(c), :] = (_rms(o, gain_ref[...]) * (g * _sigmoid(g))).astype(BF16)
        return carry

    lax.fori_loop(0, nc, out_body, 0, unroll=16)


def _gla(proj3, gin3, wf, b_f, wb, b_b, out_norm):
    b, seq, _ = proj3.shape
    kw = C_HEADS * C_DK
    k_blk = kw // C_DK
    v_blk = 2 * kw // C_DV
    g_blk = v_blk + C_HEADS
    nc = seq // C_CHUNK
    return pl.pallas_call(
        functools.partial(_gla_kernel, seq=seq),
        grid=(b, C_HEADS),
        in_specs=[
            pl.BlockSpec((None, seq, C_DK), lambda i, h: (i, 0, h)),
            pl.BlockSpec((None, seq, C_DK), lambda i, h: (i, 0, k_blk + h)),
            pl.BlockSpec((None, seq, C_DV), lambda i, h: (i, 0, v_blk + h)),
            pl.BlockSpec((None, seq, C_DV), lambda i, h: (i, 0, g_blk + h)),
            pl.BlockSpec((None, seq, LANES), lambda i, h: (i, 0, 0)),
            pl.BlockSpec((LANES, C_DK), lambda i, h: (0, h)),
            pl.BlockSpec((1, C_DK), lambda i, h: (0, h)),
            pl.BlockSpec((LANES, C_DK), lambda i, h: (0, h)),
            pl.BlockSpec((1, C_DK), lambda i, h: (0, h)),
            pl.BlockSpec((1, C_DV), lambda i, h: (0, 0)),
        ],
        out_specs=pl.BlockSpec((None, seq, C_DV), lambda i, h: (i, 0, h)),
        out_shape=jax.ShapeDtypeStruct((b, seq, C_HEADS * C_DV), BF16),
        scratch_shapes=[
            pltpu.VMEM((seq, C_DK), F32),
            pltpu.VMEM((seq, C_DK), F32),
            pltpu.VMEM((nc, C_DV, 2 * C_DK), BF16),
            pltpu.VMEM((C_DV, C_DK), F32),
            pltpu.VMEM((C_DV, C_DK), F32),
        ],
        compiler_params=_cparams(("parallel", "parallel")),
        name="gla",
    )(proj3, proj3, proj3, proj3, gin3, wf, b_f, wb, b_b, out_norm.reshape(1, C_DV))


def _final_kernel(x_ref, y0_ref, y1_ref, gain_ref, o_ref):
    tm = x_ref.shape[0]
    x = x_ref[...] + _load_token_tiles(y0_ref, tm) + _load_token_tiles(y1_ref, tm)
    o_ref[...] = _rms(x, gain_ref[...])


def _final_norm(n, x, y, gain):
    tm = ROW_TILE
    return pl.pallas_call(
        _final_kernel,
        grid=(n // tm,),
        in_specs=[_row_spec(tm, D_MODEL), _row_spec(tm * FEAT_ROWS, LANES),
                  _row_spec(tm * FEAT_ROWS, LANES, n // tm), _full_spec((1, D_MODEL))],
        out_specs=_row_spec(tm, D_MODEL),
        out_shape=jax.ShapeDtypeStruct((n, D_MODEL), F32),
        compiler_params=_cparams(("parallel",)),
        name="final_norm",
    )(x, y, y, gain.reshape(1, D_MODEL))


def _moe_layer(n, hn, ids_pad, wts_pad, w_gate, w_up, w_down, layer):
    plan = _moe_plan(ids_pad[:2], wts_pad[:2], n)
    return _moe_experts(n, hn, plan, w_gate, w_up, w_down, layer)


def kernel(x, t5_bias, norm_mix, norm_ffn, norm_final, ev_w_in, ev_lambda, ev_subln, ev_rpb, ev_w_out,
           od_w_in, od_w_gk_fwd, od_b_gk_fwd, od_w_gk_bwd, od_b_gk_bwd, od_out_norm, od_w_out,
           moe_w_grp, moe_b_grp, moe_w_exp, moe_b_exp, moe_w_gate, moe_w_up, moe_w_down):
    b, seq, d = x.shape
    n = b * seq
    rows = seq // GRID_W
    assert d == D_MODEL and n % ROW_TILE == 0 and seq % ATT_TQ == 0 and seq % C_CHUNK == 0
    assert rows % NA_QROWS == 0 and rows >= NA_KROWS and MOE_TILE <= ROW_TILE
    x2 = x.reshape(n, d)

    (proj,) = _in_proj_call(n, [(x2, 0)], norm_mix[0], ev_w_in[0].astype(BF16))
    proj3 = proj.reshape(b, seq, proj.shape[1])
    lam_init = 0.8 - 0.6 * math.exp(-0.3 * 0)
    lp = ev_lambda[0].astype(F32)
    lam = (jnp.exp(jnp.sum(lp[0] * lp[1])) - jnp.exp(jnp.sum(lp[2] * lp[3])) + lam_init).reshape(1)
    strip = _t5_strip(t5_bias, seq, ATT_TQ)
    a_diff = _diff_attn(proj3, lam, strip, ev_subln[0], 1.0 - lam_init)

    starts, classes, _ = _na_block_geometry(rows)
    table = _na_table(ev_rpb[0], rows)
    a_na = _na_attn(proj3, table, starts, classes)

    w_out = ev_w_out[0].astype(BF16)
    w_r, b_r = _router_params(moe_w_grp[0], moe_b_grp[0], moe_w_exp[0], moe_b_exp[0])
    x1, hn, ids, wts = _out_proj_router(
        n, x2, [a_diff.reshape(n, A_W), a_na.reshape(n, B_W)], [w_out[:A_W], w_out[A_W:]],
        norm_ffn[0], w_r, b_r)
    y = _moe_layer(n, hn, ids, wts, moe_w_gate, moe_w_up, moe_w_down, 0)

    kw = C_HEADS * C_DK
    main_w = 2 * kw + 2 * C_HEADS * C_DV
    w_in = od_w_in[0]
    w_gates = jnp.pad(w_in[:, main_w:], ((0, 0), (0, LANES - 2 * C_GATE_RANK))).astype(BF16)
    wf = jnp.pad(od_w_gk_fwd[0], ((0, LANES - C_GATE_RANK), (0, 0))).astype(BF16)
    wb = jnp.pad(od_w_gk_bwd[0], ((C_GATE_RANK, LANES - 2 * C_GATE_RANK), (0, 0))).astype(BF16)
    x2b, proj, gin = _in_proj_call(
        n, [(x1, 0), (y, 0), (y, n)], norm_mix[1], w_in[:, :main_w].astype(BF16), gates=w_gates)
    a_gla = _gla(proj.reshape(b, seq, main_w), gin.reshape(b, seq, LANES), wf, od_b_gk_fwd[0].reshape(1, kw),
                 wb, od_b_gk_bwd[0].reshape(1, kw), od_out_norm[0])

    w_r, b_r = _router_params(moe_w_grp[1], moe_b_grp[1], moe_w_exp[1], moe_b_exp[1])
    x3, hn, ids, wts = _out_proj_router(
        n, x2b, [a_gla.reshape(n, C_HEADS * C_DV)], [od_w_out[0].astype(BF16)], norm_ffn[1], w_r, b_r)
    y = _moe_layer(n, hn, ids, wts, moe_w_gate, moe_w_up, moe_w_down, 1)

    return _final_norm(n, x3, y, norm_final).reshape(b, seq, d)


def _na_table(rpb, rows):
    kh = min(NA_MAX_ROWS, rows)
    starts, classes, reps = _na_block_geometry(rows)
    qn, kn = NA_QROWS * GRID_W, NA_KROWS * GRID_W
    q_row, q_col = jnp.arange(qn) // GRID_W, jnp.arange(qn) % GRID_W
    k_row, k_col = jnp.arange(kn) // GRID_W, jnp.arange(kn) % GRID_W
    col_start = jnp.clip(q_col - NA_COLS // 2, 0, GRID_W - NA_COLS)
    col_ok = (k_col[None, :] >= col_start[:, None]) & (k_col[None, :] < col_start[:, None] + NA_COLS)
    side = GRID_W - NA_COLS
    col_bias = _toeplitz(jnp.pad(rpb.astype(F32), ((0, 0), (0, 0), (side, side))), GRID_W, GRID_W)
    per_class = []
    for cls in range(len(reps)):
        j = classes.index(cls)
        q_abs = j * NA_QROWS + q_row
        k_abs = starts[j] + k_row
        row_start = jnp.clip(q_abs - kh // 2, 0, rows - kh)
        row_ok = (k_abs[None, :] >= row_start[:, None]) & (k_abs[None, :] < row_start[:, None] + kh)
        dr = jnp.clip(starts[j] + jnp.arange(NA_KROWS)[None, :] - (j * NA_QROWS + jnp.arange(NA_QROWS))[:, None]
                      + NA_MAX_ROWS - 1, 0, 2 * NA_MAX_ROWS - 2)
        blocks = col_bias[:, dr]
        bias = jnp.transpose(blocks, (0, 1, 3, 2, 4)).reshape(B_HEADS, qn, kn)
        per_class.append(jnp.where((row_ok & col_ok)[None], bias, -jnp.inf))
    t = jnp.stack(per_class, axis=1)
    return t.reshape(B_HEADS // 2, 2, len(reps), qn, kn)
```

```python
import functools
import math

import jax
import jax.numpy as jnp
from jax import lax
from jax.experimental import pallas as pl
from jax.experimental.pallas import tpu as pltpu

D_MODEL = 1024
GRID_W = 64
A_HEADS = 4
A_DH = 64
B_HEADS = 8
B_DH = 64
NA_MAX_ROWS = 8
NA_COLS = 16
T5_BUCKETS = 32
T5_MAX_DIST = 128
C_HEADS = 4
C_DK = 128
C_DV = 256
C_GATE_RANK = 16
C_GATE_NORM = 16.0
C_CHUNK = 64
N_GROUPS = 4
EXP_PER_GROUP = 8
N_EXPERTS = N_GROUPS * EXP_PER_GROUP
D_EXPERT = 512
EPS = 1e-6

A_W = A_HEADS * 2 * A_DH
B_W = B_HEADS * B_DH
LANES = 128
FEAT_ROWS = D_MODEL // LANES
VMEM_LIMIT = 56 * 1024 * 1024

ROW_TILE = 512
ATT_TQ = 256
NA_QROWS = 2
NA_KROWS = 10
MOE_TILE = 256
ROUTE_ROWS = 8

F32 = jnp.float32
BF16 = jnp.bfloat16
NT_DIMS = (((1,), (1,)), ((), ()))
TN_DIMS = (((0,), (0,)), ((), ()))


def _cparams(sem):
    return pltpu.CompilerParams(dimension_semantics=sem, vmem_limit_bytes=VMEM_LIMIT)


def _load_token_tiles(ref, tm, lead=()):
    return jnp.concatenate(
        [ref[lead + (pl.ds(j, tm, stride=FEAT_ROWS), slice(None))] for j in range(FEAT_ROWS)], axis=1)


def _store_token_tiles(ref, val, lead=()):
    tm = val.shape[0]
    for j in range(FEAT_ROWS):
        ref[lead + (pl.ds(j, tm, stride=FEAT_ROWS), slice(None))] = val[:, j * LANES:(j + 1) * LANES]


def _rms(x, gain):
    return x * lax.rsqrt(jnp.mean(x * x, axis=-1, keepdims=True) + EPS) * gain


def _sigmoid(x):
    return 1.0 / (1.0 + jnp.exp(-x))


def _log_sigmoid(z):
    return jnp.minimum(z, 0.0) - jnp.log(1.0 + jnp.exp(-jnp.abs(z)))


def _in_proj_kernel(*refs, n_parts, gates):
    parts = refs[:n_parts]
    gain_ref, w_ref = refs[n_parts], refs[n_parts + 1]
    pos = n_parts + 2
    if gates:
        wg_ref = refs[pos]
        pos += 1
    outs = refs[pos:]
    x = parts[0][...]
    for p in parts[1:]:
        x = x + _load_token_tiles(p, x.shape[0])
    oi = 0
    if n_parts > 1:
        outs[0][...] = x
        oi = 1
    h = _rms(x, gain_ref[...]).astype(BF16)
    proj_ref = outs[oi]
    n_out = proj_ref.shape[1]
    step = 512
    for j in range(n_out // step):
        proj_ref[:, j * step:(j + 1) * step] = jnp.dot(
            h, w_ref[:, j * step:(j + 1) * step], preferred_element_type=F32).astype(BF16)
    if gates:
        outs[oi + 1][...] = jnp.dot(h, wg_ref[...], preferred_element_type=F32).astype(BF16)


def _row_spec(tm, width, block_off=0):
    return pl.BlockSpec((tm, width), lambda i, o=block_off: (i + o, 0))


def _full_spec(shape):
    nd = len(shape)
    return pl.BlockSpec(shape, lambda i, _nd=nd: (0,) * _nd)


def _in_proj_call(n, parts, gain, w, gates=None):
    tm = ROW_TILE
    n_parts = len(parts)
    n_out = w.shape[1]
    in_specs = [_row_spec(tm, D_MODEL)] + [_row_spec(tm * FEAT_ROWS, LANES, off // tm) for _, off in parts[1:]]
    args = [a for a, _ in parts]
    in_specs += [_full_spec((1, D_MODEL)), _full_spec(w.shape)]
    args += [gain.reshape(1, D_MODEL), w]
    out_shape, out_specs = [], []
    if n_parts > 1:
        out_shape.append(jax.ShapeDtypeStruct((n, D_MODEL), F32))
        out_specs.append(_row_spec(tm, D_MODEL))
    out_shape.append(jax.ShapeDtypeStruct((n, n_out), BF16))
    out_specs.append(_row_spec(tm, n_out))
    if gates is not None:
        in_specs.append(_full_spec(gates.shape))
        args.append(gates)
        out_shape.append(jax.ShapeDtypeStruct((n, LANES), BF16))
        out_specs.append(_row_spec(tm, LANES))
    return pl.pallas_call(
        functools.partial(_in_proj_kernel, n_parts=n_parts, gates=gates is not None),
        grid=(n // tm,),
        in_specs=in_specs,
        out_specs=out_specs,
        out_shape=out_shape,
        compiler_params=_cparams(("parallel",)),
        name="in_proj_gla" if gates is not None else "in_proj_attn",
    )(*args)


def _diff_attn_kernel(lam_ref, q_ref, k_ref, v_ref, bias_ref, gain_ref, o_ref, *, seq, out_scale):
    tq = q_ref.shape[0]
    qb = pl.program_id(2)
    q = q_ref[...]
    k = k_ref[...]
    lane = lax.broadcasted_iota(jnp.int32, q.shape, 1)
    scale = A_DH ** -0.5
    zero = jnp.zeros_like(q)
    off = pl.multiple_of((seq - tq) - qb * tq, LANES)
    bias = bias_ref[:, pl.ds(off, seq)]

    def exp_scores(qm):
        s = lax.dot_general(qm * scale, k, NT_DIMS, preferred_element_type=F32) + bias
        e = jnp.exp(s - jnp.max(s, axis=-1, keepdims=True))
        return e, jnp.sum(e, axis=-1, keepdims=True)

    e0, l0 = exp_scores(jnp.where(lane < A_DH, q, zero))
    e1, l1 = exp_scores(jnp.where(lane >= A_DH, q, zero))
    attn = (e0 - (lam_ref[0] * l0 / l1) * e1).astype(BF16)
    o = jnp.dot(attn, v_ref[...], preferred_element_type=F32) / l0
    o_ref[...] = (_rms(o, gain_ref[...]) * out_scale).astype(BF16)


def _t5_bucket(rel):
    half = T5_BUCKETS // 2
    max_exact = half // 2
    sign_off = jnp.where(rel > 0, half, 0)
    n = jnp.abs(rel)
    nf = jnp.maximum(n, 1).astype(F32)
    large = max_exact + (jnp.log(nf / max_exact) / math.log(T5_MAX_DIST / max_exact)
                         * (half - max_exact)).astype(jnp.int32)
    large = jnp.minimum(large, half - 1)
    return sign_off + jnp.where(n < max_exact, n, large)


def _t5_strip(t5_bias, seq, tq):
    n_rel = 2 * seq - 1
    vec = t5_bias[_t5_bucket(jnp.arange(n_rel) - (seq - 1))].astype(F32).T
    return _toeplitz(vec, tq, 2 * seq - tq)


def _toeplitz(vec, n_rows, n_cols):
    n = vec.shape[-1]
    lead = vec.shape[:-1]
    assert n_cols + n_rows - 1 <= n and n_cols <= n - 1
    rolled = jnp.roll(vec, -(n_rows - 1), axis=-1)
    reps = -(-(n_rows * (n - 1)) // n)
    flat = jnp.broadcast_to(rolled[..., None, :], lead + (reps, n)).reshape(lead + (reps * n,))
    return flat[..., :n_rows * (n - 1)].reshape(lead + (n_rows, n - 1))[..., :n_cols]


def _diff_attn(proj3, lam, strip, subln, out_scale):
    b, seq, _ = proj3.shape
    tq = ATT_TQ
    kblk, vblk = A_W // LANES, 2 * A_W // LANES
    return pl.pallas_call(
        functools.partial(_diff_attn_kernel, seq=seq, out_scale=out_scale),
        grid=(A_HEADS, b, seq // tq),
        in_specs=[
            pl.BlockSpec(memory_space=pltpu.SMEM),
            pl.BlockSpec((None, tq, LANES), lambda h, i, j: (i, j, h)),
            pl.BlockSpec((None, seq, LANES), lambda h, i, j: (i, 0, kblk + h)),
            pl.BlockSpec((None, seq, LANES), lambda h, i, j: (i, 0, vblk + h)),
            pl.BlockSpec((None, tq, 2 * seq - tq), lambda h, i, j: (h, 0, 0)),
            pl.BlockSpec((1, LANES), lambda h, i, j: (0, 0)),
        ],
        out_specs=pl.BlockSpec((None, tq, LANES), lambda h, i, j: (i, j, h)),
        out_shape=jax.ShapeDtypeStruct((b, seq, A_W), BF16),
        compiler_params=_cparams(("parallel", "parallel", "parallel")),
        name="diff_attn",
    )(lam, proj3, proj3, proj3, strip, subln.reshape(1, LANES))


def _na_block_geometry(rows):
    nblk = rows // NA_QROWS
    kh = min(NA_MAX_ROWS, rows)
    starts, classes, reps = [], [], []
    for j in range(nblk):
        ks = min(max(j * NA_QROWS - kh // 2, 0), rows - NA_KROWS)
        rel = (j * NA_QROWS - ks,) + tuple(
            min(max(r - kh // 2, 0), rows - kh) - ks for r in range(j * NA_QROWS, (j + 1) * NA_QROWS))
        starts.append(ks)
        if rel not in reps:
            reps.append(rel)
        classes.append(reps.index(rel))
    return starts, classes, reps


def _na_kernel(q_ref, k_ref, v_ref, tab_ref, o_ref, *, starts, classes):
    qn, kn = NA_QROWS * GRID_W, NA_KROWS * GRID_W
    scale = B_DH ** -0.5
    lane = lax.broadcasted_iota(jnp.int32, (qn, LANES), 1)
    for j, (ks, cls) in enumerate(zip(starts, classes)):
        q = q_ref[j * qn:(j + 1) * qn, :]
        kw = k_ref[ks * GRID_W:ks * GRID_W + kn, :]
        vw = v_ref[ks * GRID_W:ks * GRID_W + kn, :]
        zero = jnp.zeros_like(q)
        outs = []
        for hl in range(2):
            in_head = (lane >= hl * B_DH) & (lane < (hl + 1) * B_DH)
            qm = jnp.where(in_head, q, zero) * scale
            s = lax.dot_general(qm, kw, NT_DIMS, preferred_element_type=F32) + tab_ref[hl, cls]
            m = jnp.max(s, axis=-1, keepdims=True)
            p = jnp.exp(s - m)
            denom = jnp.sum(p, axis=-1, keepdims=True)
            outs.append(jnp.dot(p.astype(BF16), vw, preferred_element_type=F32) / denom)
        o_ref[j * qn:(j + 1) * qn, :] = jnp.where(lane < B_DH, outs[0], outs[1]).astype(BF16)


def _na_attn(proj3, table, starts, classes):
    b, seq, _ = proj3.shape
    qblk = 3 * A_W // LANES
    kblk = qblk + B_W // LANES
    vblk = kblk + B_W // LANES
    ncls = table.shape[2]
    qn, kn = NA_QROWS * GRID_W, NA_KROWS * GRID_W
    return pl.pallas_call(
        functools.partial(_na_kernel, starts=tuple(starts), classes=tuple(classes)),
        grid=(B_HEADS // 2, b),
        in_specs=[
            pl.BlockSpec((None, seq, LANES), lambda h, i: (i, 0, qblk + h)),
            pl.BlockSpec((None, seq, LANES), lambda h, i: (i, 0, kblk + h)),
            pl.BlockSpec((None, seq, LANES), lambda h, i: (i, 0, vblk + h)),
            pl.BlockSpec((None, 2, ncls, qn, kn), lambda h, i: (h, 0, 0, 0, 0)),
        ],
        out_specs=pl.BlockSpec((None, seq, LANES), lambda h, i: (i, 0, h)),
        out_shape=jax.ShapeDtypeStruct((b, seq, B_W), BF16),
        compiler_params=_cparams(("parallel", "parallel")),
        name="na_attn",
    )(proj3, proj3, proj3, table)


def _out_proj_router_kernel(*refs, n_a):
    x_ref = refs[0]
    a_refs = refs[1:1 + n_a]
    w_refs = refs[1 + n_a:1 + 2 * n_a]
    gain_ref, wr_ref, br_ref, xo_ref, hn_ref, ids_ref, wts_ref = refs[1 + 2 * n_a:]
    acc = x_ref[...]
    for a, w in zip(a_refs, w_refs):
        acc = acc + jnp.dot(a[...], w[...], preferred_element_type=F32)
    xo_ref[...] = acc
    h = _rms(acc, gain_ref[...])
    _store_token_tiles(hn_ref, h)
    h_hi = h.astype(BF16)
    h_lo = (h - h_hi.astype(F32)).astype(BF16)
    hi = jnp.dot(h_hi, wr_ref[...], preferred_element_type=F32)
    logits = (hi[:, :LANES] + hi[:, LANES:]
              + jnp.dot(h_lo, wr_ref[:, :LANES], preferred_element_type=F32) + br_ref[...])
    lane = lax.broadcasted_iota(jnp.int32, logits.shape, 1)
    neg = jnp.float32(-jnp.inf)
    big = jnp.int32(LANES)

    def masked_softmax(mask):
        z = jnp.where(mask, logits, neg)
        e = jnp.exp(z - jnp.max(z, axis=-1, keepdims=True))
        return e / jnp.sum(e, axis=-1, keepdims=True)

    def top1(p, mask):
        w = jnp.max(jnp.where(mask, p, -1.0), axis=-1, keepdims=True)
        idx = jnp.min(jnp.where(mask & (p == w), lane, big), axis=-1, keepdims=True)
        return w, idx

    is_grp = lane < N_GROUPS
    g_w, g_idx = top1(masked_softmax(is_grp), is_grp)
    e_lane = lane - N_GROUPS
    in_grp = (e_lane >= g_idx * EXP_PER_GROUP) & (e_lane < (g_idx + 1) * EXP_PER_GROUP)
    p_e = masked_softmax(in_grp)
    w1, i1 = top1(p_e, in_grp)
    rest = in_grp & (lane != i1)
    w2, i2 = top1(p_e, rest)
    denom = w1 + w2
    ids = jnp.where(lane == 0, i1 - N_GROUPS, i2 - N_GROUPS)
    wts = jnp.where(lane == 0, g_w * (w1 / denom), g_w * (w2 / denom))
    ids_ref[...] = ids.T[:ids_ref.shape[0]]
    wts_ref[...] = wts.T[:wts_ref.shape[0]]


def _out_proj_router(n, x, acts, ws, gain, w_router, b_router):
    tm = ROW_TILE
    n_a = len(acts)
    in_specs = [_row_spec(tm, D_MODEL)]
    in_specs += [_row_spec(tm, a.shape[1]) for a in acts]
    in_specs += [_full_spec(w.shape) for w in ws]
    in_specs += [_full_spec((1, D_MODEL)), _full_spec(w_router.shape), _full_spec((1, LANES))]
    return pl.pallas_call(
        functools.partial(_out_proj_router_kernel, n_a=n_a),
        grid=(n // tm,),
        in_specs=in_specs,
        out_specs=[_row_spec(tm, D_MODEL), _row_spec(tm * FEAT_ROWS, LANES),
                   pl.BlockSpec((ROUTE_ROWS, tm), lambda i: (0, i)), pl.BlockSpec((ROUTE_ROWS, tm), lambda i: (0, i))],
        out_shape=[jax.ShapeDtypeStruct((n, D_MODEL), F32), jax.ShapeDtypeStruct((n * FEAT_ROWS, LANES), F32),
                   jax.ShapeDtypeStruct((ROUTE_ROWS, n), jnp.int32), jax.ShapeDtypeStruct((ROUTE_ROWS, n), F32)],
        compiler_params=_cparams(("parallel",)),
        name="out_proj_router",
    )(x, *acts, *ws, gain.reshape(1, D_MODEL), w_router, b_router)


def _router_params(w_grp, b_grp, w_exp, b_exp):
    pad = LANES - N_GROUPS - N_EXPERTS
    w = jnp.concatenate([w_grp, w_exp, jnp.zeros((D_MODEL, pad), F32)], axis=1)
    b = jnp.concatenate([b_grp, b_exp, jnp.zeros((pad,), F32)]).reshape(1, LANES)
    w_hi = w.astype(BF16)
    w_lo = (w - w_hi.astype(F32)).astype(BF16)
    return jnp.concatenate([w_hi, w_lo], axis=1), b


def _moe_plan(ids, wts, n):
    tm = MOE_TILE
    n_assign = 2 * n
    p_rows = n_assign + N_EXPERTS * tm
    e_flat = ids.reshape(-1)
    w_flat = wts.reshape(-1)
    idx_bits = max(n_assign - 1, 1).bit_length()
    assert idx_bits + N_EXPERTS.bit_length() <= 31
    key = jnp.sort((e_flat << idx_bits) | jnp.arange(n_assign, dtype=jnp.int32))
    order = key & ((1 << idx_bits) - 1)
    experts = jnp.arange(N_EXPERTS, dtype=jnp.int32)
    counts = jnp.sum((key >> idx_bits)[None, :] == experts[:, None], axis=1, dtype=jnp.int32)
    off = jnp.cumsum(counts) - counts
    padded = (counts + tm - 1) // tm * tm
    p_end = jnp.cumsum(padded)
    p_off = p_end - padded
    nt = p_rows // tm
    tile_start = jnp.arange(nt, dtype=jnp.int32) * tm
    tile_expert = jnp.minimum(jnp.sum(p_end[None, :] <= tile_start[:, None], axis=1, dtype=jnp.int32),
                              N_EXPERTS - 1)
    tile_count = jnp.clip(counts[tile_expert] - (tile_start - p_off[tile_expert]), 0, tm).astype(jnp.int32)
    n_used = (p_end[-1] // tm).astype(jnp.int32).reshape(1)
    in_tile = jnp.arange(tm, dtype=jnp.int32)[None, :]
    real = in_tile < tile_count[:, None]
    s_idx = jnp.clip((off[tile_expert] + tile_start - p_off[tile_expert])[:, None] + in_tile, 0, n_assign - 1)
    assign = jnp.where(real, order[s_idx], 0)
    src = (assign % n).reshape(p_rows)
    dest = assign.reshape(p_rows)
    w_row = jnp.where(real, w_flat[assign], 0.0).reshape(p_rows)
    return (src.reshape(nt, 1, tm), dest.reshape(nt, 1, tm), w_row.reshape(p_rows, 1), tile_expert,
            tile_count, n_used)


def _moe_kernel(te_ref, tc_ref, nu_ref, src_ref, dst_ref, wrow_ref, x_hbm, wg_ref, wu_ref, wd_ref, y_hbm,
                xbuf, ybuf, wgb, wub, wdb, gsem, ssem):
    tm = MOE_TILE
    i = pl.program_id(0)
    n_used = nu_ref[0]
    last = pl.num_programs(0) - 1

    def tile_rows(r):
        return pl.ds(pl.multiple_of(r * FEAT_ROWS, FEAT_ROWS), FEAT_ROWS)

    def gather_copy(slot, r, tok):
        return pltpu.make_async_copy(x_hbm.at[tile_rows(tok)], xbuf.at[slot, tile_rows(r)], gsem.at[slot])

    def scatter_copy(slot, r, row):
        return pltpu.make_async_copy(ybuf.at[slot, tile_rows(r)], y_hbm.at[tile_rows(row)], ssem.at[slot])

    def start_all(make):
        def body(r2, c):
            make(2 * r2).start(priority=0)
            make(2 * r2 + 1).start(priority=1)
            return c
        lax.fori_loop(0, tm // 2, body, 0, unroll=4)

    def start_rows(count, make):
        @pl.when(count == tm)
        def _():
            start_all(make)

        @pl.when(count < tm)
        def _():
            lax.fori_loop(0, count, lambda r, c: (make(r).start(), c)[1], 0)

    def wait_scatter(t):
        slot = t % 2
        count = tc_ref[t]

        @pl.when(count == tm)
        def _():
            pltpu.make_async_copy(ybuf.at[slot], y_hbm.at[pl.ds(0, tm * FEAT_ROWS)], ssem.at[slot]).wait()

        @pl.when(count < tm)
        def _():
            lax.fori_loop(0, count, lambda r, c: (scatter_copy(slot, r, 0).wait(), c)[1], 0)

    @pl.when(i < n_used)
    def _():
        slot = i % 2
        start_all(lambda r: gather_copy(slot, r, src_ref[0, r]))

    @pl.when((i >= 1) & (i <= n_used))
    def _():
        t = i - 1
        slot = t % 2
        pltpu.make_async_copy(x_hbm.at[pl.ds(0, tm * FEAT_ROWS)], xbuf.at[slot], gsem.at[slot]).wait()

        @pl.when(t >= 2)
        def _():
            wait_scatter(t - 2)

        @pl.when(jnp.logical_or(t == 0, te_ref[t] != te_ref[jnp.maximum(t - 1, 0)]))
        def _():
            wgb[...] = wg_ref[...].astype(BF16)
            wub[...] = wu_ref[...].astype(BF16)
            wdb[...] = wd_ref[...].astype(BF16)

        x = _load_token_tiles(xbuf, tm, lead=(slot,)).astype(BF16)
        g = jnp.dot(x, wgb[...], preferred_element_type=F32)
        u = jnp.dot(x, wub[...], preferred_element_type=F32)
        hid = (g * _sigmoid(g) * u).astype(BF16)
        y = jnp.dot(hid, wdb[...], preferred_element_type=F32)
        _store_token_tiles(ybuf, y * wrow_ref[...], lead=(slot,))

        start_rows(tc_ref[t], lambda r: scatter_copy(slot, r, dst_ref[0, r]))

    @pl.when(i == last)
    def _():
        wait_scatter(n_used - 1)

        @pl.when(n_used >= 2)
        def _():
            wait_scatter(n_used - 2)


def _moe_experts(n, hn, plan, w_gate, w_up, w_down, layer):
    tm = MOE_TILE
    src, dest, w_row, tile_expert, tile_count, n_used = plan
    nt = src.shape[0]
    prev = lambda i: jnp.maximum(i - 1, 0)
    grid_spec = pltpu.PrefetchScalarGridSpec(
        num_scalar_prefetch=3,
        grid=(nt + 1,),
        in_specs=[
            pl.BlockSpec((None, 1, tm), lambda i, te, tc, nu: (jnp.minimum(i, nt - 1), 0, 0), memory_space=pltpu.SMEM),
            pl.BlockSpec((None, 1, tm), lambda i, te, tc, nu: (prev(i), 0, 0), memory_space=pltpu.SMEM),
            pl.BlockSpec((tm, 1), lambda i, te, tc, nu: (prev(i), 0)),
            pl.BlockSpec(memory_space=pl.ANY),
            pl.BlockSpec((None, None, D_MODEL, D_EXPERT), lambda i, te, tc, nu: (layer, te[prev(i)], 0, 0)),
            pl.BlockSpec((None, None, D_MODEL, D_EXPERT), lambda i, te, tc, nu: (layer, te[prev(i)], 0, 0)),
            pl.BlockSpec((None, None, D_EXPERT, D_MODEL), lambda i, te, tc, nu: (layer, te[prev(i)], 0, 0)),
        ],
        out_specs=pl.BlockSpec(memory_space=pl.ANY),
        scratch_shapes=[
            pltpu.VMEM((2, tm * FEAT_ROWS, LANES), F32),
            pltpu.VMEM((2, tm * FEAT_ROWS, LANES), F32),
            pltpu.VMEM((D_MODEL, D_EXPERT), BF16),
            pltpu.VMEM((D_MODEL, D_EXPERT), BF16),
            pltpu.VMEM((D_EXPERT, D_MODEL), BF16),
            pltpu.SemaphoreType.DMA((2,)),
            pltpu.SemaphoreType.DMA((2,)),
        ],
    )
    return pl.pallas_call(
        _moe_kernel,
        grid_spec=grid_spec,
        out_shape=jax.ShapeDtypeStruct((2 * n * FEAT_ROWS, LANES), F32),
        compiler_params=_cparams(("arbitrary",)),
        name="moe_experts",
    )(tile_expert, tile_count, n_used, src, dest, w_row, hn, w_gate, w_up, w_down)


def _gla_kernel(q_ref, k_ref, v_ref, g_ref, gin_ref, wf_ref, bf_ref, wb_ref, bb_ref, gain_ref, o_ref,
                bf_s, bb_s, s_all, stf, stb, *, seq):
    c_len = C_CHUNK
    nc = seq // c_len
    scale = C_DK ** -0.5
    row = lax.broadcasted_iota(jnp.int32, (c_len, C_DK), 0)
    shifts = [1 << s for s in range(int(math.log2(c_len)))]

    def rows(c):
        return pl.ds(pl.multiple_of(c * c_len, c_len), c_len)

    gin = gin_ref[...]
    bf_s[...] = _log_sigmoid(jnp.dot(gin, wf_ref[...], preferred_element_type=F32) + bf_ref[...]) / C_GATE_NORM
    bb_s[...] = _log_sigmoid(jnp.dot(gin, wb_ref[...], preferred_element_type=F32) + bb_ref[...]) / C_GATE_NORM

    def cum_body(c, carry):
        y = bf_s[rows(c), :]
        for s in shifts:
            y = y + jnp.where(row >= s, pltpu.roll(y, s, 0), 0.0)
        bf_s[rows(c), :] = y
        y = bb_s[rows(c), :]
        for s in shifts:
            y = y + jnp.where(row < c_len - s, pltpu.roll(y, c_len - s, 0), 0.0)
        bb_s[rows(c), :] = y
        return carry

    lax.fori_loop(0, nc, cum_body, 0)

    stf[...] = jnp.zeros_like(stf)
    stb[...] = jnp.zeros_like(stb)

    def state_step(c, b_s, last_row, st, lane0):
        k = k_ref[rows(c), :].astype(F32)
        bc = b_s[rows(c), :]
        bl = bc[last_row:last_row + 1, :]
        kd = (k * jnp.exp(bl - bc)).astype(BF16)
        kv_t = lax.dot_general(v_ref[rows(c), :], kd, TN_DIMS, preferred_element_type=F32)
        s_all[c, :, lane0:lane0 + C_DK] = st[...].astype(BF16)
        st[...] = st[...] * jnp.exp(bl) + kv_t

    def state_body(i, carry):
        state_step(i, bf_s, c_len - 1, stf, 0)
        state_step(nc - 1 - i, bb_s, 0, stb, C_DK)
        return carry

    lax.fori_loop(0, nc, state_body, 0, unroll=8)

    ri = lax.broadcasted_iota(jnp.int32, (c_len, c_len), 0)
    ci = lax.broadcasted_iota(jnp.int32, (c_len, c_len), 1)

    def out_body(c, carry):
        q = q_ref[rows(c), :].astype(F32) * scale
        k = k_ref[rows(c), :].astype(F32)
        bcf = bf_s[rows(c), :]
        bcb = bb_s[rows(c), :]
        qf = (q * jnp.exp(bcf)).astype(BF16)
        kf = (k * jnp.exp(-bcf)).astype(BF16)
        qb = (q * jnp.exp(bcb)).astype(BF16)
        kb = (k * jnp.exp(-bcb)).astype(BF16)
        att_f = lax.dot_general(qf, kf, NT_DIMS, preferred_element_type=F32)
        att_b = lax.dot_general(qb, kb, NT_DIMS, preferred_element_type=F32)
        att = jnp.where(ri >= ci, att_f, att_b).astype(BF16)
        q_both = jnp.concatenate([qf, qb], axis=1)
        o = (jnp.dot(att, v_ref[rows(c), :], preferred_element_type=F32)
             + lax.dot_general(q_both, s_all[c], NT_DIMS, preferred_element_type=F32))
        g = g_ref[rows(c), :].astype(F32)
        o_ref[rows(c), :] = (_rms(o, gain_ref[...]) * (g * _sigmoid(g))).astype(BF16)
        return carry

    lax.fori_loop(0, nc, out_body, 0, unroll=16)


def _gla(proj3, gin3, wf, b_f, wb, b_b, out_norm):
    b, seq, _ = proj3.shape
    kw = C_HEADS * C_DK
    k_blk = kw // C_DK
    v_blk = 2 * kw // C_DV
    g_blk = v_blk + C_HEADS
    nc = seq // C_CHUNK
    return pl.pallas_call(
        functools.partial(_gla_kernel, seq=seq),
        grid=(b, C_HEADS),
        in_specs=[
            pl.BlockSpec((None, seq, C_DK), lambda i, h: (i, 0, h)),
            pl.BlockSpec((None, seq, C_DK), lambda i, h: (i, 0, k_blk + h)),
            pl.BlockSpec((None, seq, C_DV), lambda i, h: (i, 0, v_blk + h)),
            pl.BlockSpec((None, seq, C_DV), lambda i, h: (i, 0, g_blk + h)),
            pl.BlockSpec((None, seq, LANES), lambda i, h: (i, 0, 0)),
            pl.BlockSpec((LANES, C_DK), lambda i, h: (0, h)),
            pl.BlockSpec((1, C_DK), lambda i, h: (0, h)),
            pl.BlockSpec((LANES, C_DK), lambda i, h: (0, h)),
            pl.BlockSpec((1, C_DK), lambda i, h: (0, h)),
            pl.BlockSpec((1, C_DV), lambda i, h: (0, 0)),
        ],
        out_specs=pl.BlockSpec((None, seq, C_DV), lambda i, h: (i, 0, h)),
        out_shape=jax.ShapeDtypeStruct((b, seq, C_HEADS * C_DV), BF16),
        scratch_shapes=[
            pltpu.VMEM((seq, C_DK), F32),
            pltpu.VMEM((seq, C_DK), F32),
            pltpu.VMEM((nc, C_DV, 2 * C_DK), BF16),
            pltpu.VMEM((C_DV, C_DK), F32),
            pltpu.VMEM((C_DV, C_DK), F32),
        ],
        compiler_params=_cparams(("parallel", "parallel")),
        name="gla",
    )(proj3, proj3, proj3, proj3, gin3, wf, b_f, wb, b_b, out_norm.reshape(1, C_DV))


def _final_kernel(x_ref, y0_ref, y1_ref, gain_ref, o_ref):
    tm = x_ref.shape[0]
    x = x_ref[...] + _load_token_tiles(y0_ref, tm) + _load_token_tiles(y1_ref, tm)
    o_ref[...] = _rms(x, gain_ref[...])


def _final_norm(n, x, y, gain):
    tm = ROW_TILE
    return pl.pallas_call(
        _final_kernel,
        grid=(n // tm,),
        in_specs=[_row_spec(tm, D_MODEL), _row_spec(tm * FEAT_ROWS, LANES),
                  _row_spec(tm * FEAT_ROWS, LANES, n // tm), _full_spec((1, D_MODEL))],
        out_specs=_row_spec(tm, D_MODEL),
        out_shape=jax.ShapeDtypeStruct((n, D_MODEL), F32),
        compiler_params=_cparams(("parallel",)),
        name="final_norm",
    )(x, y, y, gain.reshape(1, D_MODEL))


def _moe_layer(n, hn, ids_pad, wts_pad, w_gate, w_up, w_down, layer):
    plan = _moe_plan(ids_pad[:2], wts_pad[:2], n)
    return _moe_experts(n, hn, plan, w_gate, w_up, w_down, layer)


def kernel(x, t5_bias, norm_mix, norm_ffn, norm_final, ev_w_in, ev_lambda, ev_subln, ev_rpb, ev_w_out,
           od_w_in, od_w_gk_fwd, od_b_gk_fwd, od_w_gk_bwd, od_b_gk_bwd, od_out_norm, od_w_out,
           moe_w_grp, moe_b_grp, moe_w_exp, moe_b_exp, moe_w_gate, moe_w_up, moe_w_down):
    b, seq, d = x.shape
    n = b * seq
    rows = seq // GRID_W
    assert d == D_MODEL and n % ROW_TILE == 0 and seq % ATT_TQ == 0 and seq % C_CHUNK == 0
    assert rows % NA_QROWS == 0 and rows >= NA_KROWS and MOE_TILE <= ROW_TILE
    x2 = x.reshape(n, d)

    (proj,) = _in_proj_call(n, [(x2, 0)], norm_mix[0], ev_w_in[0].astype(BF16))
    proj3 = proj.reshape(b, seq, proj.shape[1])
    lam_init = 0.8 - 0.6 * math.exp(-0.3 * 0)
    lp = ev_lambda[0].astype(F32)
    lam = (jnp.exp(jnp.sum(lp[0] * lp[1])) - jnp.exp(jnp.sum(lp[2] * lp[3])) + lam_init).reshape(1)
    strip = _t5_strip(t5_bias, seq, ATT_TQ)
    a_diff = _diff_attn(proj3, lam, strip, ev_subln[0], 1.0 - lam_init)

    starts, classes, _ = _na_block_geometry(rows)
    table = _na_table(ev_rpb[0], rows)
    a_na = _na_attn(proj3, table, starts, classes)

    w_out = ev_w_out[0].astype(BF16)
    w_r, b_r = _router_params(moe_w_grp[0], moe_b_grp[0], moe_w_exp[0], moe_b_exp[0])
    x1, hn, ids, wts = _out_proj_router(
        n, x2, [a_diff.reshape(n, A_W), a_na.reshape(n, B_W)], [w_out[:A_W], w_out[A_W:]],
        norm_ffn[0], w_r, b_r)
    y = _moe_layer(n, hn, ids, wts, moe_w_gate, moe_w_up, moe_w_down, 0)

    kw = C_HEADS * C_DK
    main_w = 2 * kw + 2 * C_HEADS * C_DV
    w_in = od_w_in[0]
    w_gates = jnp.pad(w_in[:, main_w:], ((0, 0), (0, LANES - 2 * C_GATE_RANK))).astype(BF16)
    wf = jnp.pad(od_w_gk_fwd[0], ((0, LANES - C_GATE_RANK), (0, 0))).astype(BF16)
    wb = jnp.pad(od_w_gk_bwd[0], ((C_GATE_RANK, LANES - 2 * C_GATE_RANK), (0, 0))).astype(BF16)
    x2b, proj, gin = _in_proj_call(
        n, [(x1, 0), (y, 0), (y, n)], norm_mix[1], w_in[:, :main_w].astype(BF16), gates=w_gates)
    a_gla = _gla(proj.reshape(b, seq, main_w), gin.reshape(b, seq, LANES), wf, od_b_gk_fwd[0].reshape(1, kw),
                 wb, od_b_gk_bwd[0].reshape(1, kw), od_out_norm[0])

    w_r, b_r = _router_params(moe_w_grp[1], moe_b_grp[1], moe_w_exp[1], moe_b_exp[1])
    x3, hn, ids, wts = _out_proj_router(
        n, x2b, [a_gla.reshape(n, C_HEADS * C_DV)], [od_w_out[0].astype(BF16)], norm_ffn[1], w_r, b_r)
    y = _moe_layer(n, hn, ids, wts, moe_w_gate, moe_w_up, moe_w_down, 1)

    return _final_norm(n, x3, y, norm_final).reshape(b, seq, d)


def _na_table(rpb, rows):
    kh = min(NA_MAX_ROWS, rows)
    starts, classes, reps = _na_block_geometry(rows)
    qn, kn = NA_QROWS * GRID_W, NA_KROWS * GRID_W
    q_row, q_col = jnp.arange(qn) // GRID_W, jnp.arange(qn) % GRID_W
    k_row, k_col = jnp.arange(kn) // GRID_W, jnp.arange(kn) % GRID_W
    col_start = jnp.clip(q_col - NA_COLS // 2, 0, GRID_W - NA_COLS)
    col_ok = (k_col[None, :] >= col_start[:, None]) & (k_col[None, :] < col_start[:, None] + NA_COLS)
    side = GRID_W - NA_COLS
    col_bias = _toeplitz(jnp.pad(rpb.astype(F32), ((0, 0), (0, 0), (side, side))), GRID_W, GRID_W)
    per_class = []
    for cls in range(len(reps)):
        j = classes.index(cls)
        q_abs = j * NA_QROWS + q_row
        k_abs = starts[j] + k_row
        row_start = jnp.clip(q_abs - kh // 2, 0, rows - kh)
        row_ok = (k_abs[None, :] >= row_start[:, None]) & (k_abs[None, :] < row_start[:, None] + kh)
        dr = jnp.clip(starts[j] + jnp.arange(NA_KROWS)[None, :] - (j * NA_QROWS + jnp.arange(NA_QROWS))[:, None]
                      + NA_MAX_ROWS - 1, 0, 2 * NA_MAX_ROWS - 2)
        blocks = col_bias[:, dr]
        bias = jnp.transpose(blocks, (0, 1, 3, 2, 4)).reshape(B_HEADS, qn, kn)
        per_class.append(jnp.where((row_ok & col_ok)[None], bias, -jnp.inf))
    t = jnp.stack(per_class, axis=1)
    return t.reshape(B_HEADS // 2, 2, len(reps), qn, kn)
```

```python
import functools
import math

import jax
import jax.numpy as jnp
from jax import lax
from jax.experimental import pallas as pl
from jax.experimental.pallas import tpu as pltpu

D_MODEL = 1024
GRID_W = 64
A_HEADS = 4
A_DH = 64
B_HEADS = 8
B_DH = 64
NA_MAX_ROWS = 8
NA_COLS = 16
T5_BUCKETS = 32
T5_MAX_DIST = 128
C_HEADS = 4
C_DK = 128
C_DV = 256
C_GATE_RANK = 16
C_GATE_NORM = 16.0
C_CHUNK = 64
N_GROUPS = 4
EXP_PER_GROUP = 8
N_EXPERTS = N_GROUPS * EXP_PER_GROUP
D_EXPERT = 512
EPS = 1e-6

A_W = A_HEADS * 2 * A_DH
B_W = B_HEADS * B_DH
LANES = 128
FEAT_ROWS = D_MODEL // LANES
VMEM_LIMIT = 56 * 1024 * 1024

ROW_TILE = 512
ATT_TQ = 512
NA_QROWS = 4
NA_KROWS = 12
MOE_TILE = 256
ROUTE_ROWS = 8

F32 = jnp.float32
BF16 = jnp.bfloat16
NT_DIMS = (((1,), (1,)), ((), ()))
TN_DIMS = (((0,), (0,)), ((), ()))


def _cparams(sem):
    return pltpu.CompilerParams(dimension_semantics=sem, vmem_limit_bytes=VMEM_LIMIT)


def _load_token_tiles(ref, tm, lead=()):
    return jnp.concatenate(
        [ref[lead + (pl.ds(j, tm, stride=FEAT_ROWS), slice(None))] for j in range(FEAT_ROWS)], axis=1)


def _store_token_tiles(ref, val, lead=()):
    tm = val.shape[0]
    for j in range(FEAT_ROWS):
        ref[lead + (pl.ds(j, tm, stride=FEAT_ROWS), slice(None))] = val[:, j * LANES:(j + 1) * LANES]


def _rms(x, gain):
    return x * lax.rsqrt(jnp.mean(x * x, axis=-1, keepdims=True) + EPS) * gain


def _sigmoid(x):
    return 1.0 / (1.0 + jnp.exp(-x))


def _log_sigmoid(z):
    return jnp.minimum(z, 0.0) - jnp.log(1.0 + jnp.exp(-jnp.abs(z)))


def _in_proj_kernel(*refs, n_parts, gates):
    parts = refs[:n_parts]
    gain_ref, w_ref = refs[n_parts], refs[n_parts + 1]
    pos = n_parts + 2
    if gates:
        wg_ref = refs[pos]
        pos += 1
    outs = refs[pos:]
    x = parts[0][...]
    for p in parts[1:]:
        x = x + _load_token_tiles(p, x.shape[0])
    oi = 0
    if n_parts > 1:
        outs[0][...] = x
        oi = 1
    h = _rms(x, gain_ref[...]).astype(BF16)
    proj_ref = outs[oi]
    n_out = proj_ref.shape[1]
    step = 512
    for j in range(n_out // step):
        proj_ref[:, j * step:(j + 1) * step] = jnp.dot(
            h, w_ref[:, j * step:(j + 1) * step], preferred_element_type=F32).astype(BF16)
    if gates:
        outs[oi + 1][...] = jnp.dot(h, wg_ref[...], preferred_element_type=F32).astype(BF16)


def _row_spec(tm, width, block_off=0):
    return pl.BlockSpec((tm, width), lambda i, o=block_off: (i + o, 0))


def _full_spec(shape):
    nd = len(shape)
    return pl.BlockSpec(shape, lambda i, _nd=nd: (0,) * _nd)


def _in_proj_call(n, parts, gain, w, gates=None):
    tm = ROW_TILE
    n_parts = len(parts)
    n_out = w.shape[1]
    in_specs = [_row_spec(tm, D_MODEL)] + [_row_spec(tm * FEAT_ROWS, LANES, off // tm) for _, off in parts[1:]]
    args = [a for a, _ in parts]
    in_specs += [_full_spec((1, D_MODEL)), _full_spec(w.shape)]
    args += [gain.reshape(1, D_MODEL), w]
    out_shape, out_specs = [], []
    if n_parts > 1:
        out_shape.append(jax.ShapeDtypeStruct((n, D_MODEL), F32))
        out_specs.append(_row_spec(tm, D_MODEL))
    out_shape.append(jax.ShapeDtypeStruct((n, n_out), BF16))
    out_specs.append(_row_spec(tm, n_out))
    if gates is not None:
        in_specs.append(_full_spec(gates.shape))
        args.append(gates)
        out_shape.append(jax.ShapeDtypeStruct((n, LANES), BF16))
        out_specs.append(_row_spec(tm, LANES))
    return pl.pallas_call(
        functools.partial(_in_proj_kernel, n_parts=n_parts, gates=gates is not None),
        grid=(n // tm,),
        in_specs=in_specs,
        out_specs=out_specs,
        out_shape=out_shape,
        compiler_params=_cparams(("parallel",)),
        name="in_proj_gla" if gates is not None else "in_proj_attn",
    )(*args)


def _diff_attn_kernel(lam_ref, q_ref, k_ref, v_ref, bias_ref, gain_ref, o_ref, *, seq, out_scale):
    tq = q_ref.shape[0]
    qb = pl.program_id(2)
    q = q_ref[...]
    k = k_ref[...]
    lane = lax.broadcasted_iota(jnp.int32, q.shape, 1)
    scale = A_DH ** -0.5
    zero = jnp.zeros_like(q)
    off = pl.multiple_of((seq - tq) - qb * tq, LANES)
    bias = bias_ref[:, pl.ds(off, seq)]

    def exp_scores(qm):
        s = lax.dot_general(qm * scale, k, NT_DIMS, preferred_element_type=F32) + bias
        e = jnp.exp(s - jnp.max(s, axis=-1, keepdims=True))
        return e, jnp.sum(e, axis=-1, keepdims=True)

    e0, l0 = exp_scores(jnp.where(lane < A_DH, q, zero))
    e1, l1 = exp_scores(jnp.where(lane >= A_DH, q, zero))
    attn = (e0 - (lam_ref[0] * l0 / l1) * e1).astype(BF16)
    o = jnp.dot(attn, v_ref[...], preferred_element_type=F32) / l0
    o_ref[...] = (_rms(o, gain_ref[...]) * out_scale).astype(BF16)


def _t5_bucket(rel):
    half = T5_BUCKETS // 2
    max_exact = half // 2
    sign_off = jnp.where(rel > 0, half, 0)
    n = jnp.abs(rel)
    nf = jnp.maximum(n, 1).astype(F32)
    large = max_exact + (jnp.log(nf / max_exact) / math.log(T5_MAX_DIST / max_exact)
                         * (half - max_exact)).astype(jnp.int32)
    large = jnp.minimum(large, half - 1)
    return sign_off + jnp.where(n < max_exact, n, large)


def _t5_strip(t5_bias, seq, tq):
    n_rel = 2 * seq - 1
    vec = t5_bias[_t5_bucket(jnp.arange(n_rel) - (seq - 1))].astype(F32).T
    return _toeplitz(vec, tq, 2 * seq - tq)


def _toeplitz(vec, n_rows, n_cols):
    n = vec.shape[-1]
    lead = vec.shape[:-1]
    assert n_cols + n_rows - 1 <= n and n_cols <= n - 1
    rolled = jnp.roll(vec, -(n_rows - 1), axis=-1)
    reps = -(-(n_rows * (n - 1)) // n)
    flat = jnp.broadcast_to(rolled[..., None, :], lead + (reps, n)).reshape(lead + (reps * n,))
    return flat[..., :n_rows * (n - 1)].reshape(lead + (n_rows, n - 1))[..., :n_cols]


def _diff_attn(proj3, lam, strip, subln, out_scale):
    b, seq, _ = proj3.shape
    tq = ATT_TQ
    kblk, vblk = A_W // LANES, 2 * A_W // LANES
    return pl.pallas_call(
        functools.partial(_diff_attn_kernel, seq=seq, out_scale=out_scale),
        grid=(A_HEADS, b, seq // tq),
        in_specs=[
            pl.BlockSpec(memory_space=pltpu.SMEM),
            pl.BlockSpec((None, tq, LANES), lambda h, i, j: (i, j, h)),
            pl.BlockSpec((None, seq, LANES), lambda h, i, j: (i, 0, kblk + h)),
            pl.BlockSpec((None, seq, LANES), lambda h, i, j: (i, 0, vblk + h)),
            pl.BlockSpec((None, tq, 2 * seq - tq), lambda h, i, j: (h, 0, 0)),
            pl.BlockSpec((1, LANES), lambda h, i, j: (0, 0)),
        ],
        out_specs=pl.BlockSpec((None, tq, LANES), lambda h, i, j: (i, j, h)),
        out_shape=jax.ShapeDtypeStruct((b, seq, A_W), BF16),
        compiler_params=_cparams(("parallel", "parallel", "parallel")),
        name="diff_attn",
    )(lam, proj3, proj3, proj3, strip, subln.reshape(1, LANES))


def _na_block_geometry(rows):
    nblk = rows // NA_QROWS
    kh = min(NA_MAX_ROWS, rows)
    starts, classes, reps = [], [], []
    for j in range(nblk):
        ks = min(max(j * NA_QROWS - kh // 2, 0), rows - NA_KROWS)
        rel = (j * NA_QROWS - ks,) + tuple(
            min(max(r - kh // 2, 0), rows - kh) - ks for r in range(j * NA_QROWS, (j + 1) * NA_QROWS))
        starts.append(ks)
        if rel not in reps:
            reps.append(rel)
        classes.append(reps.index(rel))
    return starts, classes, reps


def _na_kernel(q_ref, k_ref, v_ref, tab_ref, o_ref, *, starts, classes):
    qn, kn = NA_QROWS * GRID_W, NA_KROWS * GRID_W
    scale = B_DH ** -0.5
    lane = lax.broadcasted_iota(jnp.int32, (qn, LANES), 1)
    for j, (ks, cls) in enumerate(zip(starts, classes)):
        q = q_ref[j * qn:(j + 1) * qn, :]
        kw = k_ref[ks * GRID_W:ks * GRID_W + kn, :]
        vw = v_ref[ks * GRID_W:ks * GRID_W + kn, :]
        zero = jnp.zeros_like(q)
        outs = []
        for hl in range(2):
            in_head = (lane >= hl * B_DH) & (lane < (hl + 1) * B_DH)
            qm = jnp.where(in_head, q, zero) * scale
            s = lax.dot_general(qm, kw, NT_DIMS, preferred_element_type=F32) + tab_ref[hl, cls]
            m = jnp.max(s, axis=-1, keepdims=True)
            p = jnp.exp(s - m)
            denom = jnp.sum(p, axis=-1, keepdims=True)
            outs.append(jnp.dot(p.astype(BF16), vw, preferred_element_type=F32) / denom)
        o_ref[j * qn:(j + 1) * qn, :] = jnp.where(lane < B_DH, outs[0], outs[1]).astype(BF16)


def _na_attn(proj3, table, starts, classes):
    b, seq, _ = proj3.shape
    qblk = 3 * A_W // LANES
    kblk = qblk + B_W // LANES
    vblk = kblk + B_W // LANES
    ncls = table.shape[2]
    qn, kn = NA_QROWS * GRID_W, NA_KROWS * GRID_W
    return pl.pallas_call(
        functools.partial(_na_kernel, starts=tuple(starts), classes=tuple(classes)),
        grid=(B_HEADS // 2, b),
        in_specs=[
            pl.BlockSpec((None, seq, LANES), lambda h, i: (i, 0, qblk + h)),
            pl.BlockSpec((None, seq, LANES), lambda h, i: (i, 0, kblk + h)),
            pl.BlockSpec((None, seq, LANES), lambda h, i: (i, 0, vblk + h)),
            pl.BlockSpec((None, 2, ncls, qn, kn), lambda h, i: (h, 0, 0, 0, 0)),
        ],
        out_specs=pl.BlockSpec((None, seq, LANES), lambda h, i: (i, 0, h)),
        out_shape=jax.ShapeDtypeStruct((b, seq, B_W), BF16),
        compiler_params=_cparams(("parallel", "parallel")),
        name="na_attn",
    )(proj3, proj3, proj3, table)


def _out_proj_router_kernel(*refs, n_a):
    x_ref = refs[0]
    a_refs = refs[1:1 + n_a]
    w_refs = refs[1 + n_a:1 + 2 * n_a]
    gain_ref, wr_ref, br_ref, xo_ref, hn_ref, ids_ref, wts_ref = refs[1 + 2 * n_a:]
    acc = x_ref[...]
    for a, w in zip(a_refs, w_refs):
        acc = acc + jnp.dot(a[...], w[...], preferred_element_type=F32)
    xo_ref[...] = acc
    h = _rms(acc, gain_ref[...])
    _store_token_tiles(hn_ref, h)
    h_hi = h.astype(BF16)
    h_lo = (h - h_hi.astype(F32)).astype(BF16)
    hi = jnp.dot(h_hi, wr_ref[...], preferred_element_type=F32)
    logits = (hi[:, :LANES] + hi[:, LANES:]
              + jnp.dot(h_lo, wr_ref[:, :LANES], preferred_element_type=F32) + br_ref[...])
    lane = lax.broadcasted_iota(jnp.int32, logits.shape, 1)
    neg = jnp.float32(-jnp.inf)
    big = jnp.int32(LANES)

    def masked_softmax(mask):
        z = jnp.where(mask, logits, neg)
        e = jnp.exp(z - jnp.max(z, axis=-1, keepdims=True))
        return e / jnp.sum(e, axis=-1, keepdims=True)

    def top1(p, mask):
        w = jnp.max(jnp.where(mask, p, -1.0), axis=-1, keepdims=True)
        idx = jnp.min(jnp.where(mask & (p == w), lane, big), axis=-1, keepdims=True)
        return w, idx

    is_grp = lane < N_GROUPS
    g_w, g_idx = top1(masked_softmax(is_grp), is_grp)
    e_lane = lane - N_GROUPS
    in_grp = (e_lane >= g_idx * EXP_PER_GROUP) & (e_lane < (g_idx + 1) * EXP_PER_GROUP)
    p_e = masked_softmax(in_grp)
    w1, i1 = top1(p_e, in_grp)
    rest = in_grp & (lane != i1)
    w2, i2 = top1(p_e, rest)
    denom = w1 + w2
    ids = jnp.where(lane == 0, i1 - N_GROUPS, i2 - N_GROUPS)
    wts = jnp.where(lane == 0, g_w * (w1 / denom), g_w * (w2 / denom))
    ids_ref[...] = ids.T[:ids_ref.shape[0]]
    wts_ref[...] = wts.T[:wts_ref.shape[0]]


def _out_proj_router(n, x, acts, ws, gain, w_router, b_router):
    tm = ROW_TILE
    n_a = len(acts)
    in_specs = [_row_spec(tm, D_MODEL)]
    in_specs += [_row_spec(tm, a.shape[1]) for a in acts]
    in_specs += [_full_spec(w.shape) for w in ws]
    in_specs += [_full_spec((1, D_MODEL)), _full_spec(w_router.shape), _full_spec((1, LANES))]
    return pl.pallas_call(
        functools.partial(_out_proj_router_kernel, n_a=n_a),
        grid=(n // tm,),
        in_specs=in_specs,
        out_specs=[_row_spec(tm, D_MODEL), _row_spec(tm * FEAT_ROWS, LANES),
                   pl.BlockSpec((ROUTE_ROWS, tm), lambda i: (0, i)), pl.BlockSpec((ROUTE_ROWS, tm), lambda i: (0, i))],
        out_shape=[jax.ShapeDtypeStruct((n, D_MODEL), F32), jax.ShapeDtypeStruct((n * FEAT_ROWS, LANES), F32),
                   jax.ShapeDtypeStruct((ROUTE_ROWS, n), jnp.int32), jax.ShapeDtypeStruct((ROUTE_ROWS, n), F32)],
        compiler_params=_cparams(("parallel",)),
        name="out_proj_router",
    )(x, *acts, *ws, gain.reshape(1, D_MODEL), w_router, b_router)


def _router_params(w_grp, b_grp, w_exp, b_exp):
    pad = LANES - N_GROUPS - N_EXPERTS
    w = jnp.concatenate([w_grp, w_exp, jnp.zeros((D_MODEL, pad), F32)], axis=1)
    b = jnp.concatenate([b_grp, b_exp, jnp.zeros((pad,), F32)]).reshape(1, LANES)
    w_hi = w.astype(BF16)
    w_lo = (w - w_hi.astype(F32)).astype(BF16)
    return jnp.concatenate([w_hi, w_lo], axis=1), b


def _moe_plan(ids, wts, n):
    tm = MOE_TILE
    n_assign = 2 * n
    p_rows = n_assign + N_EXPERTS * tm
    e_flat = ids.reshape(-1)
    w_flat = wts.reshape(-1)
    idx_bits = max(n_assign - 1, 1).bit_length()
    assert idx_bits + N_EXPERTS.bit_length() <= 31
    key = jnp.sort((e_flat << idx_bits) | jnp.arange(n_assign, dtype=jnp.int32))
    order = key & ((1 << idx_bits) - 1)
    experts = jnp.arange(N_EXPERTS, dtype=jnp.int32)
    counts = jnp.sum((key >> idx_bits)[None, :] == experts[:, None], axis=1, dtype=jnp.int32)
    off = jnp.cumsum(counts) - counts
    padded = (counts + tm - 1) // tm * tm
    p_end = jnp.cumsum(padded)
    p_off = p_end - padded
    nt = p_rows // tm
    tile_start = jnp.arange(nt, dtype=jnp.int32) * tm
    tile_expert = jnp.minimum(jnp.sum(p_end[None, :] <= tile_start[:, None], axis=1, dtype=jnp.int32),
                              N_EXPERTS - 1)
    tile_count = jnp.clip(counts[tile_expert] - (tile_start - p_off[tile_expert]), 0, tm).astype(jnp.int32)
    n_used = (p_end[-1] // tm).astype(jnp.int32).reshape(1)
    in_tile = jnp.arange(tm, dtype=jnp.int32)[None, :]
    real = in_tile < tile_count[:, None]
    s_idx = jnp.clip((off[tile_expert] + tile_start - p_off[tile_expert])[:, None] + in_tile, 0, n_assign - 1)
    assign = jnp.where(real, order[s_idx], 0)
    src = (assign % n).reshape(p_rows)
    dest = assign.reshape(p_rows)
    w_row = jnp.where(real, w_flat[assign], 0.0).reshape(p_rows)
    return (src.reshape(nt, 1, tm), dest.reshape(nt, 1, tm), w_row.reshape(p_rows, 1), tile_expert,
            tile_count, n_used)


def _moe_kernel(te_ref, tc_ref, nu_ref, src_ref, dst_ref, wrow_ref, x_hbm, wg_ref, wu_ref, wd_ref, y_hbm,
                xbuf, ybuf, wgb, wub, wdb, gsem, ssem):
    tm = MOE_TILE
    i = pl.program_id(0)
    n_used = nu_ref[0]
    last = pl.num_programs(0) - 1

    def tile_rows(r):
        return pl.ds(pl.multiple_of(r * FEAT_ROWS, FEAT_ROWS), FEAT_ROWS)

    def gather_copy(slot, r, tok):
        return pltpu.make_async_copy(x_hbm.at[tile_rows(tok)], xbuf.at[slot, tile_rows(r)], gsem.at[slot])

    def scatter_copy(slot, r, row):
        return pltpu.make_async_copy(ybuf.at[slot, tile_rows(r)], y_hbm.at[tile_rows(row)], ssem.at[slot])

    def start_all(make):
        def body(r2, c):
            make(2 * r2).start(priority=0)
            make(2 * r2 + 1).start(priority=1)
            return c
        lax.fori_loop(0, tm // 2, body, 0, unroll=4)

    def start_rows(count, make):
        @pl.when(count == tm)
        def _():
            start_all(make)

        @pl.when(count < tm)
        def _():
            lax.fori_loop(0, count, lambda r, c: (make(r).start(), c)[1], 0)

    def wait_scatter(t):
        slot = t % 2
        count = tc_ref[t]

        @pl.when(count == tm)
        def _():
            pltpu.make_async_copy(ybuf.at[slot], y_hbm.at[pl.ds(0, tm * FEAT_ROWS)], ssem.at[slot]).wait()

        @pl.when(count < tm)
        def _():
            lax.fori_loop(0, count, lambda r, c: (scatter_copy(slot, r, 0).wait(), c)[1], 0)

    @pl.when(i < n_used)
    def _():
        slot = i % 2
        start_all(lambda r: gather_copy(slot, r, src_ref[0, r]))

    @pl.when((i >= 1) & (i <= n_used))
    def _():
        t = i - 1
        slot = t % 2
        pltpu.make_async_copy(x_hbm.at[pl.ds(0, tm * FEAT_ROWS)], xbuf.at[slot], gsem.at[slot]).wait()

        @pl.when(t >= 2)
        def _():
            wait_scatter(t - 2)

        @pl.when(jnp.logical_or(t == 0, te_ref[t] != te_ref[jnp.maximum(t - 1, 0)]))
        def _():
            wgb[...] = wg_ref[...].astype(BF16)
            wub[...] = wu_ref[...].astype(BF16)
            wdb[...] = wd_ref[...].astype(BF16)

        x = _load_token_tiles(xbuf, tm, lead=(slot,)).astype(BF16)
        g = jnp.dot(x, wgb[...], preferred_element_type=F32)
        u = jnp.dot(x, wub[...], preferred_element_type=F32)
        hid = (g * _sigmoid(g) * u).astype(BF16)
        y = jnp.dot(hid, wdb[...], preferred_element_type=F32)
        _store_token_tiles(ybuf, y * wrow_ref[...], lead=(slot,))

        start_rows(tc_ref[t], lambda r: scatter_copy(slot, r, dst_ref[0, r]))

    @pl.when(i == last)
    def _():
        wait_scatter(n_used - 1)

        @pl.when(n_used >= 2)
        def _():
            wait_scatter(n_used - 2)


def _moe_experts(n, hn, plan, w_gate, w_up, w_down, layer):
    tm = MOE_TILE
    src, dest, w_row, tile_expert, tile_count, n_used = plan
    nt = src.shape[0]
    prev = lambda i: jnp.maximum(i - 1, 0)
    grid_spec = pltpu.PrefetchScalarGridSpec(
        num_scalar_prefetch=3,
        grid=(nt + 1,),
        in_specs=[
            pl.BlockSpec((None, 1, tm), lambda i, te, tc, nu: (jnp.minimum(i, nt - 1), 0, 0), memory_space=pltpu.SMEM),
            pl.BlockSpec((None, 1, tm), lambda i, te, tc, nu: (prev(i), 0, 0), memory_space=pltpu.SMEM),
            pl.BlockSpec((tm, 1), lambda i, te, tc, nu: (prev(i), 0)),
            pl.BlockSpec(memory_space=pl.ANY),
            pl.BlockSpec((None, None, D_MODEL, D_EXPERT), lambda i, te, tc, nu: (layer, te[prev(i)], 0, 0)),
            pl.BlockSpec((None, None, D_MODEL, D_EXPERT), lambda i, te, tc, nu: (layer, te[prev(i)], 0, 0)),
            pl.BlockSpec((None, None, D_EXPERT, D_MODEL), lambda i, te, tc, nu: (layer, te[prev(i)], 0, 0)),
        ],
        out_specs=pl.BlockSpec(memory_space=pl.ANY),
        scratch_shapes=[
            pltpu.VMEM((2, tm * FEAT_ROWS, LANES), F32),
            pltpu.VMEM((2, tm * FEAT_ROWS, LANES), F32),
            pltpu.VMEM((D_MODEL, D_EXPERT), BF16),
            pltpu.VMEM((D_MODEL, D_EXPERT), BF16),
            pltpu.VMEM((D_EXPERT, D_MODEL), BF16),
            pltpu.SemaphoreType.DMA((2,)),
            pltpu.SemaphoreType.DMA((2,)),
        ],
    )
    return pl.pallas_call(
        _moe_kernel,
        grid_spec=grid_spec,
        out_shape=jax.ShapeDtypeStruct((2 * n * FEAT_ROWS, LANES), F32),
        compiler_params=_cparams(("arbitrary",)),
        name="moe_experts",
    )(tile_expert, tile_count, n_used, src, dest, w_row, hn, w_gate, w_up, w_down)


def _gla_kernel(q_ref, k_ref, v_ref, g_ref, gin_ref, wf_ref, bf_ref, wb_ref, bb_ref, gain_ref, o_ref,
                bf_s, bb_s, s_all, stf, stb, *, seq):
    c_len = C_CHUNK
    nc = seq // c_len
    scale = C_DK ** -0.5
    row = lax.broadcasted_iota(jnp.int32, (c_len, C_DK), 0)
    shifts = [1 << s for s in range(int(math.log2(c_len)))]

    def rows(c):
        return pl.ds(pl.multiple_of(c * c_len, c_len), c_len)

    gin = gin_ref[...]
    bf_s[...] = _log_sigmoid(jnp.dot(gin, wf_ref[...], preferred_element_type=F32) + bf_ref[...]) / C_GATE_NORM
    bb_s[...] = _log_sigmoid(jnp.dot(gin, wb_ref[...], preferred_element_type=F32) + bb_ref[...]) / C_GATE_NORM

    def cum_body(c, carry):
        y = bf_s[rows(c), :]
        for s in shifts:
            y = y + jnp.where(row >= s, pltpu.roll(y, s, 0), 0.0)
        bf_s[rows(c), :] = y
        y = bb_s[rows(c), :]
        for s in shifts:
            y = y + jnp.where(row < c_len - s, pltpu.roll(y, c_len - s, 0), 0.0)
        bb_s[rows(c), :] = y
        return carry

    lax.fori_loop(0, nc, cum_body, 0)

    stf[...] = jnp.zeros_like(stf)
    stb[...] = jnp.zeros_like(stb)

    def state_step(c, b_s, last_row, st, lane0):
        k = k_ref[rows(c), :].astype(F32)
        bc = b_s[rows(c), :]
        bl = bc[last_row:last_row + 1, :]
        kd = (k * jnp.exp(bl - bc)).astype(BF16)
        kv_t = lax.dot_general(v_ref[rows(c), :], kd, TN_DIMS, preferred_element_type=F32)
        s_all[c, :, lane0:lane0 + C_DK] = st[...].astype(BF16)
        st[...] = st[...] * jnp.exp(bl) + kv_t

    def state_body(i, carry):
        state_step(i, bf_s, c_len - 1, stf, 0)
        state_step(nc - 1 - i, bb_s, 0, stb, C_DK)
        return carry

    lax.fori_loop(0, nc, state_body, 0, unroll=16)

    ri = lax.broadcasted_iota(jnp.int32, (c_len, c_len), 0)
    ci = lax.broadcasted_iota(jnp.int32, (c_len, c_len), 1)

    def out_body(c, carry):
        q = q_ref[rows(c), :].astype(F32) * scale
        k = k_ref[rows(c), :].astype(F32)
        bcf = bf_s[rows(c), :]
        bcb = bb_s[rows(c), :]
        qf = (q * jnp.exp(bcf)).astype(BF16)
        kf = (k * jnp.exp(-bcf)).astype(BF16)
        qb = (q * jnp.exp(bcb)).astype(BF16)
        kb = (k * jnp.exp(-bcb)).astype(BF16)
        att_f = lax.dot_general(qf, kf, NT_DIMS, preferred_element_type=F32)
        att_b = lax.dot_general(qb, kb, NT_DIMS, preferred_element_type=F32)
        att = jnp.where(ri >= ci, att_f, att_b).astype(BF16)
        q_both = jnp.concatenate([qf, qb], axis=1)
        o = (jnp.dot(att, v_ref[rows(c), :], preferred_element_type=F32)
             + lax.dot_general(q_both, s_all[c], NT_DIMS, preferred_element_type=F32))
        g = g_ref[rows(c), :].astype(F32)
        o_ref[rows(c), :] = (_rms(o, gain_ref[...]) * (g * _sigmoid(g))).astype(BF16)
        return carry

    lax.fori_loop(0, nc, out_body, 0, unroll=16)


def _gla(proj3, gin3, wf, b_f, wb, b_b, out_norm):
    b, seq, _ = proj3.shape
    kw = C_HEADS * C_DK
    k_blk = kw // C_DK
    v_blk = 2 * kw // C_DV
    g_blk = v_blk + C_HEADS
    nc = seq // C_CHUNK
    return pl.pallas_call(
        functools.partial(_gla_kernel, seq=seq),
        grid=(b, C_HEADS),
        in_specs=[
            pl.BlockSpec((None, seq, C_DK), lambda i, h: (i, 0, h)),
            pl.BlockSpec((None, seq, C_DK), lambda i, h: (i, 0, k_blk + h)),
            pl.BlockSpec((None, seq, C_DV), lambda i, h: (i, 0, v_blk + h)),
            pl.BlockSpec((None, seq, C_DV), lambda i, h: (i, 0, g_blk + h)),
            pl.BlockSpec((None, seq, LANES), lambda i, h: (i, 0, 0)),
            pl.BlockSpec((LANES, C_DK), lambda i, h: (0, h)),
            pl.BlockSpec((1, C_DK), lambda i, h: (0, h)),
            pl.BlockSpec((LANES, C_DK), lambda i, h: (0, h)),
            pl.BlockSpec((1, C_DK), lambda i, h: (0, h)),
            pl.BlockSpec((1, C_DV), lambda i, h: (0, 0)),
        ],
        out_specs=pl.BlockSpec((None, seq, C_DV), lambda i, h: (i, 0, h)),
        out_shape=jax.ShapeDtypeStruct((b, seq, C_HEADS * C_DV), BF16),
        scratch_shapes=[
            pltpu.VMEM((seq, C_DK), F32),
            pltpu.VMEM((seq, C_DK), F32),
            pltpu.VMEM((nc, C_DV, 2 * C_DK), BF16),
            pltpu.VMEM((C_DV, C_DK), F32),
            pltpu.VMEM((C_DV, C_DK), F32),
        ],
        compiler_params=_cparams(("parallel", "parallel")),
        name="gla",
    )(proj3, proj3, proj3, proj3, gin3, wf, b_f, wb, b_b, out_norm.reshape(1, C_DV))


def _final_kernel(x_ref, y0_ref, y1_ref, gain_ref, o_ref):
    tm = x_ref.shape[0]
    x = x_ref[...] + _load_token_tiles(y0_ref, tm) + _load_token_tiles(y1_ref, tm)
    o_ref[...] = _rms(x, gain_ref[...])


def _final_norm(n, x, y, gain):
    tm = ROW_TILE
    return pl.pallas_call(
        _final_kernel,
        grid=(n // tm,),
        in_specs=[_row_spec(tm, D_MODEL), _row_spec(tm * FEAT_ROWS, LANES),
                  _row_spec(tm * FEAT_ROWS, LANES, n // tm), _full_spec((1, D_MODEL))],
        out_specs=_row_spec(tm, D_MODEL),
        out_shape=jax.ShapeDtypeStruct((n, D_MODEL), F32),
        compiler_params=_cparams(("parallel",)),
        name="final_norm",
    )(x, y, y, gain.reshape(1, D_MODEL))


def _moe_layer(n, hn, ids_pad, wts_pad, w_gate, w_up, w_down, layer):
    plan = _moe_plan(ids_pad[:2], wts_pad[:2], n)
    return _moe_experts(n, hn, plan, w_gate, w_up, w_down, layer)


def kernel(x, t5_bias, norm_mix, norm_ffn, norm_final, ev_w_in, ev_lambda, ev_subln, ev_rpb, ev_w_out,
           od_w_in, od_w_gk_fwd, od_b_gk_fwd, od_w_gk_bwd, od_b_gk_bwd, od_out_norm, od_w_out,
           moe_w_grp, moe_b_grp, moe_w_exp, moe_b_exp, moe_w_gate, moe_w_up, moe_w_down):
    b, seq, d = x.shape
    n = b * seq
    rows = seq // GRID_W
    assert d == D_MODEL and n % ROW_TILE == 0 and seq % ATT_TQ == 0 and seq % C_CHUNK == 0
    assert rows % NA_QROWS == 0 and rows >= NA_KROWS and MOE_TILE <= ROW_TILE
    x2 = x.reshape(n, d)

    (proj,) = _in_proj_call(n, [(x2, 0)], norm_mix[0], ev_w_in[0].astype(BF16))
    proj3 = proj.reshape(b, seq, proj.shape[1])
    lam_init = 0.8 - 0.6 * math.exp(-0.3 * 0)
    lp = ev_lambda[0].astype(F32)
    lam = (jnp.exp(jnp.sum(lp[0] * lp[1])) - jnp.exp(jnp.sum(lp[2] * lp[3])) + lam_init).reshape(1)
    strip = _t5_strip(t5_bias, seq, ATT_TQ)
    a_diff = _diff_attn(proj3, lam, strip, ev_subln[0], 1.0 - lam_init)

    starts, classes, _ = _na_block_geometry(rows)
    table = _na_table(ev_rpb[0], rows)
    a_na = _na_attn(proj3, table, starts, classes)

    w_out = ev_w_out[0].astype(BF16)
    w_r, b_r = _router_params(moe_w_grp[0], moe_b_grp[0], moe_w_exp[0], moe_b_exp[0])
    x1, hn, ids, wts = _out_proj_router(
        n, x2, [a_diff.reshape(n, A_W), a_na.reshape(n, B_W)], [w_out[:A_W], w_out[A_W:]],
        norm_ffn[0], w_r, b_r)
    y = _moe_layer(n, hn, ids, wts, moe_w_gate, moe_w_up, moe_w_down, 0)

    kw = C_HEADS * C_DK
    main_w = 2 * kw + 2 * C_HEADS * C_DV
    w_in = od_w_in[0]
    w_gates = jnp.pad(w_in[:, main_w:], ((0, 0), (0, LANES - 2 * C_GATE_RANK))).astype(BF16)
    wf = jnp.pad(od_w_gk_fwd[0], ((0, LANES - C_GATE_RANK), (0, 0))).astype(BF16)
    wb = jnp.pad(od_w_gk_bwd[0], ((C_GATE_RANK, LANES - 2 * C_GATE_RANK), (0, 0))).astype(BF16)
    x2b, proj, gin = _in_proj_call(
        n, [(x1, 0), (y, 0), (y, n)], norm_mix[1], w_in[:, :main_w].astype(BF16), gates=w_gates)
    a_gla = _gla(proj.reshape(b, seq, main_w), gin.reshape(b, seq, LANES), wf, od_b_gk_fwd[0].reshape(1, kw),
                 wb, od_b_gk_bwd[0].reshape(1, kw), od_out_norm[0])

    w_r, b_r = _router_params(moe_w_grp[1], moe_b_grp[1], moe_w_exp[1], moe_b_exp[1])
    x3, hn, ids, wts = _out_proj_router(
        n, x2b, [a_gla.reshape(n, C_HEADS * C_DV)], [od_w_out[0].astype(BF16)], norm_ffn[1], w_r, b_r)
    y = _moe_layer(n, hn, ids, wts, moe_w_gate, moe_w_up, moe_w_down, 1)

    return _final_norm(n, x3, y, norm_final).reshape(b, seq, d)


def _na_table(rpb, rows):
    kh = min(NA_MAX_ROWS, rows)
    starts, classes, reps = _na_block_geometry(rows)
    qn, kn = NA_QROWS * GRID_W, NA_KROWS * GRID_W
    q_row, q_col = jnp.arange(qn) // GRID_W, jnp.arange(qn) % GRID_W
    k_row, k_col = jnp.arange(kn) // GRID_W, jnp.arange(kn) % GRID_W
    col_start = jnp.clip(q_col - NA_COLS // 2, 0, GRID_W - NA_COLS)
    col_ok = (k_col[None, :] >= col_start[:, None]) & (k_col[None, :] < col_start[:, None] + NA_COLS)
    side = GRID_W - NA_COLS
    col_bias = _toeplitz(jnp.pad(rpb.astype(F32), ((0, 0), (0, 0), (side, side))), GRID_W, GRID_W)
    per_class = []
    for cls in range(len(reps)):
        j = classes.index(cls)
        q_abs = j * NA_QROWS + q_row
        k_abs = starts[j] + k_row
        row_start = jnp.clip(q_abs - kh // 2, 0, rows - kh)
        row_ok = (k_abs[None, :] >= row_start[:, None]) & (k_abs[None, :] < row_start[:, None] + kh)
        dr = jnp.clip(starts[j] + jnp.arange(NA_KROWS)[None, :] - (j * NA_QROWS + jnp.arange(NA_QROWS))[:, None]
                      + NA_MAX_ROWS - 1, 0, 2 * NA_MAX_ROWS - 2)
        blocks = col_bias[:, dr]
        bias = jnp.transpose(blocks, (0, 1, 3, 2, 4)).reshape(B_HEADS, qn, kn)
        per_class.append(jnp.where((row_ok & col_ok)[None], bias, -jnp.inf))
    t = jnp.stack(per_class, axis=1)
    return t.reshape(B_HEADS // 2, 2, len(reps), qn, kn)
```

```python
import functools
import math

import jax
import jax.numpy as jnp
from jax import lax
from jax.experimental import pallas as pl
from jax.experimental.pallas import tpu as pltpu

D_MODEL = 1024
GRID_W = 64
A_HEADS = 4
A_DH = 64
B_HEADS = 8
B_DH = 64
NA_MAX_ROWS = 8
NA_COLS = 16
T5_BUCKETS = 32
T5_MAX_DIST = 128
C_HEADS = 4
C_DK = 128
C_DV = 256
C_GATE_RANK = 16
C_GATE_NORM = 16.0
C_CHUNK = 64
N_GROUPS = 4
EXP_PER_GROUP = 8
N_EXPERTS = N_GROUPS * EXP_PER_GROUP
D_EXPERT = 512
EPS = 1e-6

A_W = A_HEADS * 2 * A_DH
B_W = B_HEADS * B_DH
LANES = 128
FEAT_ROWS = D_MODEL // LANES
VMEM_LIMIT = 56 * 1024 * 1024

ROW_TILE = 512
ATT_TQ = 256
NA_QROWS = 4
NA_KROWS = 12
MOE_TILE = 256
ROUTE_ROWS = 8

F32 = jnp.float32
BF16 = jnp.bfloat16
NT_DIMS = (((1,), (1,)), ((), ()))
TN_DIMS = (((0,), (0,)), ((), ()))


def _cparams(sem):
    return pltpu.CompilerParams(dimension_semantics=sem, vmem_limit_bytes=VMEM_LIMIT)


def _load_token_tiles(ref, tm, lead=()):
    return jnp.concatenate(
        [ref[lead + (pl.ds(j, tm, stride=FEAT_ROWS), slice(None))] for j in range(FEAT_ROWS)], axis=1)


def _store_token_tiles(ref, val, lead=()):
    tm = val.shape[0]
    for j in range(FEAT_ROWS):
        ref[lead + (pl.ds(j, tm, stride=FEAT_ROWS), slice(None))] = val[:, j * LANES:(j + 1) * LANES]


def _rms(x, gain):
    return x * lax.rsqrt(jnp.mean(x * x, axis=-1, keepdims=True) + EPS) * gain


def _sigmoid(x):
    return 1.0 / (1.0 + jnp.exp(-x))


def _log_sigmoid(z):
    return jnp.minimum(z, 0.0) - jnp.log(1.0 + jnp.exp(-jnp.abs(z)))


def _in_proj_kernel(*refs, n_parts, gates):
    parts = refs[:n_parts]
    gain_ref, w_ref = refs[n_parts], refs[n_parts + 1]
    pos = n_parts + 2
    if gates:
        wg_ref = refs[pos]
        pos += 1
    outs = refs[pos:]
    x = parts[0][...]
    for p in parts[1:]:
        x = x + _load_token_tiles(p, x.shape[0])
    oi = 0
    if n_parts > 1:
        outs[0][...] = x
        oi = 1
    h = _rms(x, gain_ref[...]).astype(BF16)
    proj_ref = outs[oi]
    n_out = proj_ref.shape[1]
    step = 512
    for j in range(n_out // step):
        proj_ref[:, j * step:(j + 1) * step] = jnp.dot(
            h, w_ref[:, j * step:(j + 1) * step], preferred_element_type=F32).astype(BF16)
    if gates:
        outs[oi + 1][...] = jnp.dot(h, wg_ref[...], preferred_element_type=F32).astype(BF16)


def _row_spec(tm, width, block_off=0):
    return pl.BlockSpec((tm, width), lambda i, o=block_off: (i + o, 0))


def _full_spec(shape):
    nd = len(shape)
    return pl.BlockSpec(shape, lambda i, _nd=nd: (0,) * _nd)


def _in_proj_call(n, parts, gain, w, gates=None):
    tm = ROW_TILE
    n_parts = len(parts)
    n_out = w.shape[1]
    in_specs = [_row_spec(tm, D_MODEL)] + [_row_spec(tm * FEAT_ROWS, LANES, off // tm) for _, off in parts[1:]]
    args = [a for a, _ in parts]
    in_specs += [_full_spec((1, D_MODEL)), _full_spec(w.shape)]
    args += [gain.reshape(1, D_MODEL), w]
    out_shape, out_specs = [], []
    if n_parts > 1:
        out_shape.append(jax.ShapeDtypeStruct((n, D_MODEL), F32))
        out_specs.append(_row_spec(tm, D_MODEL))
    out_shape.append(jax.ShapeDtypeStruct((n, n_out), BF16))
    out_specs.append(_row_spec(tm, n_out))
    if gates is not None:
        in_specs.append(_full_spec(gates.shape))
        args.append(gates)
        out_shape.append(jax.ShapeDtypeStruct((n, LANES), BF16))
        out_specs.append(_row_spec(tm, LANES))
    return pl.pallas_call(
        functools.partial(_in_proj_kernel, n_parts=n_parts, gates=gates is not None),
        grid=(n // tm,),
        in_specs=in_specs,
        out_specs=out_specs,
        out_shape=out_shape,
        compiler_params=_cparams(("parallel",)),
        name="in_proj_gla" if gates is not None else "in_proj_attn",
    )(*args)


def _diff_attn_kernel(lam_ref, q_ref, k_ref, v_ref, bias_ref, gain_ref, o_ref, *, seq, out_scale):
    tq = q_ref.shape[0]
    qb = pl.program_id(2)
    q = q_ref[...]
    k = k_ref[...]
    lane = lax.broadcasted_iota(jnp.int32, q.shape, 1)
    scale = A_DH ** -0.5
    zero = jnp.zeros_like(q)
    off = pl.multiple_of((seq - tq) - qb * tq, LANES)
    bias = bias_ref[:, pl.ds(off, seq)]

    def exp_scores(qm):
        s = lax.dot_general(qm * scale, k, NT_DIMS, preferred_element_type=F32) + bias
        e = jnp.exp(s - jnp.max(s, axis=-1, keepdims=True))
        return e, jnp.sum(e, axis=-1, keepdims=True)

    e0, l0 = exp_scores(jnp.where(lane < A_DH, q, zero))
    e1, l1 = exp_scores(jnp.where(lane >= A_DH, q, zero))
    attn = (e0 - (lam_ref[0] * l0 / l1) * e1).astype(BF16)
    o = jnp.dot(attn, v_ref[...], preferred_element_type=F32) / l0
    o_ref[...] = (_rms(o, gain_ref[...]) * out_scale).astype(BF16)


def _t5_bucket(rel):
    half = T5_BUCKETS // 2
    max_exact = half // 2
    sign_off = jnp.where(rel > 0, half, 0)
    n = jnp.abs(rel)
    nf = jnp.maximum(n, 1).astype(F32)
    large = max_exact + (jnp.log(nf / max_exact) / math.log(T5_MAX_DIST / max_exact)
                         * (half - max_exact)).astype(jnp.int32)
    large = jnp.minimum(large, half - 1)
    return sign_off + jnp.where(n < max_exact, n, large)


def _t5_strip(t5_bias, seq, tq):
    n_rel = 2 * seq - 1
    vec = t5_bias[_t5_bucket(jnp.arange(n_rel) - (seq - 1))].astype(F32).T
    return _toeplitz(vec, tq, 2 * seq - tq)


def _toeplitz(vec, n_rows, n_cols):
    n = vec.shape[-1]
    lead = vec.shape[:-1]
    assert n_cols + n_rows - 1 <= n and n_cols <= n - 1
    rolled = jnp.roll(vec, -(n_rows - 1), axis=-1)
    reps = -(-(n_rows * (n - 1)) // n)
    flat = jnp.broadcast_to(rolled[..., None, :], lead + (reps, n)).reshape(lead + (reps * n,))
    return flat[..., :n_rows * (n - 1)].reshape(lead + (n_rows, n - 1))[..., :n_cols]


def _diff_attn(proj3, lam, strip, subln, out_scale):
    b, seq, _ = proj3.shape
    tq = ATT_TQ
    kblk, vblk = A_W // LANES, 2 * A_W // LANES
    return pl.pallas_call(
        functools.partial(_diff_attn_kernel, seq=seq, out_scale=out_scale),
        grid=(A_HEADS, b, seq // tq),
        in_specs=[
            pl.BlockSpec(memory_space=pltpu.SMEM),
            pl.BlockSpec((None, tq, LANES), lambda h, i, j: (i, j, h)),
            pl.BlockSpec((None, seq, LANES), lambda h, i, j: (i, 0, kblk + h)),
            pl.BlockSpec((None, seq, LANES), lambda h, i, j: (i, 0, vblk + h)),
            pl.BlockSpec((None, tq, 2 * seq - tq), lambda h, i, j: (h, 0, 0)),
            pl.BlockSpec((1, LANES), lambda h, i, j: (0, 0)),
        ],
        out_specs=pl.BlockSpec((None, tq, LANES), lambda h, i, j: (i, j, h)),
        out_shape=jax.ShapeDtypeStruct((b, seq, A_W), BF16),
        compiler_params=_cparams(("parallel", "parallel", "parallel")),
        name="diff_attn",
    )(lam, proj3, proj3, proj3, strip, subln.reshape(1, LANES))


def _na_block_geometry(rows):
    nblk = rows // NA_QROWS
    kh = min(NA_MAX_ROWS, rows)
    starts, classes, reps = [], [], []
    for j in range(nblk):
        ks = min(max(j * NA_QROWS - kh // 2, 0), rows - NA_KROWS)
        rel = (j * NA_QROWS - ks,) + tuple(
            min(max(r - kh // 2, 0), rows - kh) - ks for r in range(j * NA_QROWS, (j + 1) * NA_QROWS))
        starts.append(ks)
        if rel not in reps:
            reps.append(rel)
        classes.append(reps.index(rel))
    return starts, classes, reps


def _na_kernel(q_ref, k_ref, v_ref, tab_ref, o_ref, *, starts, classes):
    qn, kn = NA_QROWS * GRID_W, NA_KROWS * GRID_W
    scale = B_DH ** -0.5
    lane = lax.broadcasted_iota(jnp.int32, (qn, LANES), 1)
    for j, (ks, cls) in enumerate(zip(starts, classes)):
        q = q_ref[j * qn:(j + 1) * qn, :]
        kw = k_ref[ks * GRID_W:ks * GRID_W + kn, :]
        vw = v_ref[ks * GRID_W:ks * GRID_W + kn, :]
        zero = jnp.zeros_like(q)
        outs = []
        for hl in range(2):
            in_head = (lane >= hl * B_DH) & (lane < (hl + 1) * B_DH)
            qm = jnp.where(in_head, q, zero) * scale
            s = lax.dot_general(qm, kw, NT_DIMS, preferred_element_type=F32) + tab_ref[hl, cls]
            m = jnp.max(s, axis=-1, keepdims=True)
            p = jnp.exp(s - m)
            denom = jnp.sum(p, axis=-1, keepdims=True)
            outs.append(jnp.dot(p.astype(BF16), vw, preferred_element_type=F32) / denom)
        o_ref[j * qn:(j + 1) * qn, :] = jnp.where(lane < B_DH, outs[0], outs[1]).astype(BF16)


def _na_attn(proj3, table, starts, classes):
    b, seq, _ = proj3.shape
    qblk = 3 * A_W // LANES
    kblk = qblk + B_W // LANES
    vblk = kblk + B_W // LANES
    ncls = table.shape[2]
    qn, kn = NA_QROWS * GRID_W, NA_KROWS * GRID_W
    return pl.pallas_call(
        functools.partial(_na_kernel, starts=tuple(starts), classes=tuple(classes)),
        grid=(B_HEADS // 2, b),
        in_specs=[
            pl.BlockSpec((None, seq, LANES), lambda h, i: (i, 0, qblk + h)),
            pl.BlockSpec((None, seq, LANES), lambda h, i: (i, 0, kblk + h)),
            pl.BlockSpec((None, seq, LANES), lambda h, i: (i, 0, vblk + h)),
            pl.BlockSpec((None, 2, ncls, qn, kn), lambda h, i: (h, 0, 0, 0, 0)),
        ],
        out_specs=pl.BlockSpec((None, seq, LANES), lambda h, i: (i, 0, h)),
        out_shape=jax.ShapeDtypeStruct((b, seq, B_W), BF16),
        compiler_params=_cparams(("parallel", "parallel")),
        name="na_attn",
    )(proj3, proj3, proj3, table)


def _out_proj_router_kernel(*refs, n_a):
    x_ref = refs[0]
    a_refs = refs[1:1 + n_a]
    w_refs = refs[1 + n_a:1 + 2 * n_a]
    gain_ref, wr_ref, br_ref, xo_ref, hn_ref, ids_ref, wts_ref = refs[1 + 2 * n_a:]
    acc = x_ref[...]
    for a, w in zip(a_refs, w_refs):
        acc = acc + jnp.dot(a[...], w[...], preferred_element_type=F32)
    xo_ref[...] = acc
    h = _rms(acc, gain_ref[...])
    _store_token_tiles(hn_ref, h)
    h_hi = h.astype(BF16)
    h_lo = (h - h_hi.astype(F32)).astype(BF16)
    hi = jnp.dot(h_hi, wr_ref[...], preferred_element_type=F32)
    logits = (hi[:, :LANES] + hi[:, LANES:]
              + jnp.dot(h_lo, wr_ref[:, :LANES], preferred_element_type=F32) + br_ref[...])
    lane = lax.broadcasted_iota(jnp.int32, logits.shape, 1)
    neg = jnp.float32(-jnp.inf)
    big = jnp.int32(LANES)

    def masked_softmax(mask):
        z = jnp.where(mask, logits, neg)
        e = jnp.exp(z - jnp.max(z, axis=-1, keepdims=True))
        return e / jnp.sum(e, axis=-1, keepdims=True)

    def top1(p, mask):
        w = jnp.max(jnp.where(mask, p, -1.0), axis=-1, keepdims=True)
        idx = jnp.min(jnp.where(mask & (p == w), lane, big), axis=-1, keepdims=True)
        return w, idx

    is_grp = lane < N_GROUPS
    g_w, g_idx = top1(masked_softmax(is_grp), is_grp)
    e_lane = lane - N_GROUPS
    in_grp = (e_lane >= g_idx * EXP_PER_GROUP) & (e_lane < (g_idx + 1) * EXP_PER_GROUP)
    p_e = masked_softmax(in_grp)
    w1, i1 = top1(p_e, in_grp)
    rest = in_grp & (lane != i1)
    w2, i2 = top1(p_e, rest)
    denom = w1 + w2
    ids = jnp.where(lane == 0, i1 - N_GROUPS, i2 - N_GROUPS)
    wts = jnp.where(lane == 0, g_w * (w1 / denom), g_w * (w2 / denom))
    ids_ref[...] = ids.T[:ids_ref.shape[0]]
    wts_ref[...] = wts.T[:wts_ref.shape[0]]


def _out_proj_router(n, x, acts, ws, gain, w_router, b_router):
    tm = ROW_TILE
    n_a = len(acts)
    in_specs = [_row_spec(tm, D_MODEL)]
    in_specs += [_row_spec(tm, a.shape[1]) for a in acts]
    in_specs += [_full_spec(w.shape) for w in ws]
    in_specs += [_full_spec((1, D_MODEL)), _full_spec(w_router.shape), _full_spec((1, LANES))]
    return pl.pallas_call(
        functools.partial(_out_proj_router_kernel, n_a=n_a),
        grid=(n // tm,),
        in_specs=in_specs,
        out_specs=[_row_spec(tm, D_MODEL), _row_spec(tm * FEAT_ROWS, LANES),
                   pl.BlockSpec((ROUTE_ROWS, tm), lambda i: (0, i)), pl.BlockSpec((ROUTE_ROWS, tm), lambda i: (0, i))],
        out_shape=[jax.ShapeDtypeStruct((n, D_MODEL), F32), jax.ShapeDtypeStruct((n * FEAT_ROWS, LANES), F32),
                   jax.ShapeDtypeStruct((ROUTE_ROWS, n), jnp.int32), jax.ShapeDtypeStruct((ROUTE_ROWS, n), F32)],
        compiler_params=_cparams(("parallel",)),
        name="out_proj_router",
    )(x, *acts, *ws, gain.reshape(1, D_MODEL), w_router, b_router)


def _router_params(w_grp, b_grp, w_exp, b_exp):
    pad = LANES - N_GROUPS - N_EXPERTS
    w = jnp.concatenate([w_grp, w_exp, jnp.zeros((D_MODEL, pad), F32)], axis=1)
    b = jnp.concatenate([b_grp, b_exp, jnp.zeros((pad,), F32)]).reshape(1, LANES)
    w_hi = w.astype(BF16)
    w_lo = (w - w_hi.astype(F32)).astype(BF16)
    return jnp.concatenate([w_hi, w_lo], axis=1), b


def _moe_plan(ids, wts, n):
    tm = MOE_TILE
    n_assign = 2 * n
    p_rows = n_assign + N_EXPERTS * tm
    e_flat = ids.reshape(-1)
    w_flat = wts.reshape(-1)
    idx_bits = max(n_assign - 1, 1).bit_length()
    assert idx_bits + N_EXPERTS.bit_length() <= 31
    key = jnp.sort((e_flat << idx_bits) | jnp.arange(n_assign, dtype=jnp.int32))
    order = key & ((1 << idx_bits) - 1)
    experts = jnp.arange(N_EXPERTS, dtype=jnp.int32)
    counts = jnp.sum((key >> idx_bits)[None, :] == experts[:, None], axis=1, dtype=jnp.int32)
    off = jnp.cumsum(counts) - counts
    padded = (counts + tm - 1) // tm * tm
    p_end = jnp.cumsum(padded)
    p_off = p_end - padded
    nt = p_rows // tm
    tile_start = jnp.arange(nt, dtype=jnp.int32) * tm
    tile_expert = jnp.minimum(jnp.sum(p_end[None, :] <= tile_start[:, None], axis=1, dtype=jnp.int32),
                              N_EXPERTS - 1)
    tile_count = jnp.clip(counts[tile_expert] - (tile_start - p_off[tile_expert]), 0, tm).astype(jnp.int32)
    n_used = (p_end[-1] // tm).astype(jnp.int32).reshape(1)
    in_tile = jnp.arange(tm, dtype=jnp.int32)[None, :]
    real = in_tile < tile_count[:, None]
    s_idx = jnp.clip((off[tile_expert] + tile_start - p_off[tile_expert])[:, None] + in_tile, 0, n_assign - 1)
    assign = jnp.where(real, order[s_idx], 0)
    src = (assign % n).reshape(p_rows)
    dest = assign.reshape(p_rows)
    w_row = jnp.where(real, w_flat[assign], 0.0).reshape(p_rows)
    return (src.reshape(nt, 1, tm), dest.reshape(nt, 1, tm), w_row.reshape(p_rows, 1), tile_expert,
            tile_count, n_used)


def _moe_kernel(te_ref, tc_ref, nu_ref, src_ref, dst_ref, wrow_ref, x_hbm, wg_ref, wu_ref, wd_ref, y_hbm,
                xbuf, ybuf, wgb, wub, wdb, gsem, ssem):
    tm = MOE_TILE
    i = pl.program_id(0)
    n_used = nu_ref[0]
    last = pl.num_programs(0) - 1

    def tile_rows(r):
        return pl.ds(pl.multiple_of(r * FEAT_ROWS, FEAT_ROWS), FEAT_ROWS)

    def gather_copy(slot, r, tok):
        return pltpu.make_async_copy(x_hbm.at[tile_rows(tok)], xbuf.at[slot, tile_rows(r)], gsem.at[slot])

    def scatter_copy(slot, r, row):
        return pltpu.make_async_copy(ybuf.at[slot, tile_rows(r)], y_hbm.at[tile_rows(row)], ssem.at[slot])

    def start_all(make):
        def body(r2, c):
            make(2 * r2).start(priority=0)
            make(2 * r2 + 1).start(priority=1)
            return c
        lax.fori_loop(0, tm // 2, body, 0, unroll=4)

    def start_rows(count, make):
        @pl.when(count == tm)
        def _():
            start_all(make)

        @pl.when(count < tm)
        def _():
            lax.fori_loop(0, count, lambda r, c: (make(r).start(), c)[1], 0)

    def wait_scatter(t):
        slot = t % 2
        count = tc_ref[t]

        @pl.when(count == tm)
        def _():
            pltpu.make_async_copy(ybuf.at[slot], y_hbm.at[pl.ds(0, tm * FEAT_ROWS)], ssem.at[slot]).wait()

        @pl.when(count < tm)
        def _():
            lax.fori_loop(0, count, lambda r, c: (scatter_copy(slot, r, 0).wait(), c)[1], 0)

    @pl.when(i < n_used)
    def _():
        slot = i % 2
        start_all(lambda r: gather_copy(slot, r, src_ref[0, r]))

    @pl.when((i >= 1) & (i <= n_used))
    def _():
        t = i - 1
        slot = t % 2
        pltpu.make_async_copy(x_hbm.at[pl.ds(0, tm * FEAT_ROWS)], xbuf.at[slot], gsem.at[slot]).wait()

        @pl.when(t >= 2)
        def _():
            wait_scatter(t - 2)

        @pl.when(jnp.logical_or(t == 0, te_ref[t] != te_ref[jnp.maximum(t - 1, 0)]))
        def _():
            wgb[...] = wg_ref[...].astype(BF16)
            wub[...] = wu_ref[...].astype(BF16)
            wdb[...] = wd_ref[...].astype(BF16)

        x = _load_token_tiles(xbuf, tm, lead=(slot,)).astype(BF16)
        g = jnp.dot(x, wgb[...], preferred_element_type=F32)
        u = jnp.dot(x, wub[...], preferred_element_type=F32)
        hid = (g * _sigmoid(g) * u).astype(BF16)
        y = jnp.dot(hid, wdb[...], preferred_element_type=F32)
        _store_token_tiles(ybuf, y * wrow_ref[...], lead=(slot,))

        start_rows(tc_ref[t], lambda r: scatter_copy(slot, r, dst_ref[0, r]))

    @pl.when(i == last)
    def _():
        wait_scatter(n_used - 1)

        @pl.when(n_used >= 2)
        def _():
            wait_scatter(n_used - 2)


def _moe_experts(n, hn, plan, w_gate, w_up, w_down, layer):
    tm = MOE_TILE
    src, dest, w_row, tile_expert, tile_count, n_used = plan
    nt = src.shape[0]
    prev = lambda i: jnp.maximum(i - 1, 0)
    grid_spec = pltpu.PrefetchScalarGridSpec(
        num_scalar_prefetch=3,
        grid=(nt + 1,),
        in_specs=[
            pl.BlockSpec((None, 1, tm), lambda i, te, tc, nu: (jnp.minimum(i, nt - 1), 0, 0), memory_space=pltpu.SMEM),
            pl.BlockSpec((None, 1, tm), lambda i, te, tc, nu: (prev(i), 0, 0), memory_space=pltpu.SMEM),
            pl.BlockSpec((tm, 1), lambda i, te, tc, nu: (prev(i), 0)),
            pl.BlockSpec(memory_space=pl.ANY),
            pl.BlockSpec((None, None, D_MODEL, D_EXPERT), lambda i, te, tc, nu: (layer, te[prev(i)], 0, 0)),
            pl.BlockSpec((None, None, D_MODEL, D_EXPERT), lambda i, te, tc, nu: (layer, te[prev(i)], 0, 0)),
            pl.BlockSpec((None, None, D_EXPERT, D_MODEL), lambda i, te, tc, nu: (layer, te[prev(i)], 0, 0)),
        ],
        out_specs=pl.BlockSpec(memory_space=pl.ANY),
        scratch_shapes=[
            pltpu.VMEM((2, tm * FEAT_ROWS, LANES), F32),
            pltpu.VMEM((2, tm * FEAT_ROWS, LANES), F32),
            pltpu.VMEM((D_MODEL, D_EXPERT), BF16),
            pltpu.VMEM((D_MODEL, D_EXPERT), BF16),
            pltpu.VMEM((D_EXPERT, D_MODEL), BF16),
            pltpu.SemaphoreType.DMA((2,)),
            pltpu.SemaphoreType.DMA((2,)),
        ],
    )
    return pl.pallas_call(
        _moe_kernel,
        grid_spec=grid_spec,
        out_shape=jax.ShapeDtypeStruct((2 * n * FEAT_ROWS, LANES), F32),
        compiler_params=_cparams(("arbitrary",)),
        name="moe_experts",
    )(tile_expert, tile_count, n_used, src, dest, w_row, hn, w_gate, w_up, w_down)


def _gla_kernel(q_ref, k_ref, v_ref, g_ref, gin_ref, wf_ref, bf_ref, wb_ref, bb_ref, gain_ref, o_ref,
                bf_s, bb_s, s_all, stf, stb, *, seq):
    c_len = C_CHUNK
    nc = seq // c_len
    scale = C_DK ** -0.5
    row = lax.broadcasted_iota(jnp.int32, (c_len, C_DK), 0)
    shifts = [1 << s for s in range(int(math.log2(c_len)))]

    def rows(c):
        return pl.ds(pl.multiple_of(c * c_len, c_len), c_len)

    gin = gin_ref[...]
    bf_s[...] = _log_sigmoid(jnp.dot(gin, wf_ref[...], preferred_element_type=F32) + bf_ref[...]) / C_GATE_NORM
    bb_s[...] = _log_sigmoid(jnp.dot(gin, wb_ref[...], preferred_element_type=F32) + bb_ref[...]) / C_GATE_NORM

    def cum_body(c, carry):
        y = bf_s[rows(c), :]
        for s in shifts:
            y = y + jnp.where(row >= s, pltpu.roll(y, s, 0), 0.0)
        bf_s[rows(c), :] = y
        y = bb_s[rows(c), :]
        for s in shifts:
            y = y + jnp.where(row < c_len - s, pltpu.roll(y, c_len - s, 0), 0.0)
        bb_s[rows(c), :] = y
        return carry

    lax.fori_loop(0, nc, cum_body, 0)

    stf[...] = jnp.zeros_like(stf)
    stb[...] = jnp.zeros_like(stb)

    def state_step(c, b_s, last_row, st, lane0):
        k = k_ref[rows(c), :].astype(F32)
        bc = b_s[rows(c), :]
        bl = bc[last_row:last_row + 1, :]
        kd = (k * jnp.exp(bl - bc)).astype(BF16)
        kv_t = lax.dot_general(v_ref[rows(c), :], kd, TN_DIMS, preferred_element_type=F32)
        s_all[c, :, lane0:lane0 + C_DK] = st[...].astype(BF16)
        st[...] = st[...] * jnp.exp(bl) + kv_t

    def state_body(i, carry):
        state_step(i, bf_s, c_len - 1, stf, 0)
        state_step(nc - 1 - i, bb_s, 0, stb, C_DK)
        return carry

    lax.fori_loop(0, nc, state_body, 0, unroll=16)

    ri = lax.broadcasted_iota(jnp.int32, (c_len, c_len), 0)
    ci = lax.broadcasted_iota(jnp.int32, (c_len, c_len), 1)

    def out_body(c, carry):
        q = q_ref[rows(c), :].astype(F32) * scale
        k = k_ref[rows(c), :].astype(F32)
        bcf = bf_s[rows(c), :]
        bcb = bb_s[rows(c), :]
        qf = (q * jnp.exp(bcf)).astype(BF16)
        kf = (k * jnp.exp(-bcf)).astype(BF16)
        qb = (q * jnp.exp(bcb)).astype(BF16)
        kb = (k * jnp.exp(-bcb)).astype(BF16)
        att_f = lax.dot_general(qf, kf, NT_DIMS, preferred_element_type=F32)
        att_b = lax.dot_general(qb, kb, NT_DIMS, preferred_element_type=F32)
        att = jnp.where(ri >= ci, att_f, att_b).astype(BF16)
        q_both = jnp.concatenate([qf, qb], axis=1)
        o = (jnp.dot(att, v_ref[rows(c), :], preferred_element_type=F32)
             + lax.dot_general(q_both, s_all[c], NT_DIMS, preferred_element_type=F32))
        g = g_ref[rows(c), :].astype(F32)
        o_ref[rows(c), :] = (_rms(o, gain_ref[...]) * (g * _sigmoid(g))).astype(BF16)
        return carry

    lax.fori_loop(0, nc, out_body, 0, unroll=32)


def _gla(proj3, gin3, wf, b_f, wb, b_b, out_norm):
    b, seq, _ = proj3.shape
    kw = C_HEADS * C_DK
    k_blk = kw // C_DK
    v_blk = 2 * kw // C_DV
    g_blk = v_blk + C_HEADS
    nc = seq // C_CHUNK
    return pl.pallas_call(
        functools.partial(_gla_kernel, seq=seq),
        grid=(b, C_HEADS),
        in_specs=[
            pl.BlockSpec((None, seq, C_DK), lambda i, h: (i, 0, h)),
            pl.BlockSpec((None, seq, C_DK), lambda i, h: (i, 0, k_blk + h)),
            pl.BlockSpec((None, seq, C_DV), lambda i, h: (i, 0, v_blk + h)),
            pl.BlockSpec((None, seq, C_DV), lambda i, h: (i, 0, g_blk + h)),
            pl.BlockSpec((None, seq, LANES), lambda i, h: (i, 0, 0)),
            pl.BlockSpec((LANES, C_DK), lambda i, h: (0, h)),
            pl.BlockSpec((1, C_DK), lambda i, h: (0, h)),
            pl.BlockSpec((LANES, C_DK), lambda i, h: (0, h)),
            pl.BlockSpec((1, C_DK), lambda i, h: (0, h)),
            pl.BlockSpec((1, C_DV), lambda i, h: (0, 0)),
        ],
        out_specs=pl.BlockSpec((None, seq, C_DV), lambda i, h: (i, 0, h)),
        out_shape=jax.ShapeDtypeStruct((b, seq, C_HEADS * C_DV), BF16),
        scratch_shapes=[
            pltpu.VMEM((seq, C_DK), F32),
            pltpu.VMEM((seq, C_DK), F32),
            pltpu.VMEM((nc, C_DV, 2 * C_DK), BF16),
            pltpu.VMEM((C_DV, C_DK), F32),
            pltpu.VMEM((C_DV, C_DK), F32),
        ],
        compiler_params=_cparams(("parallel", "parallel")),
        name="gla",
    )(proj3, proj3, proj3, proj3, gin3, wf, b_f, wb, b_b, out_norm.reshape(1, C_DV))


def _final_kernel(x_ref, y0_ref, y1_ref, gain_ref, o_ref):
    tm = x_ref.shape[0]
    x = x_ref[...] + _load_token_tiles(y0_ref, tm) + _load_token_tiles(y1_ref, tm)
    o_ref[...] = _rms(x, gain_ref[...])


def _final_norm(n, x, y, gain):
    tm = ROW_TILE
    return pl.pallas_call(
        _final_kernel,
        grid=(n // tm,),
        in_specs=[_row_spec(tm, D_MODEL), _row_spec(tm * FEAT_ROWS, LANES),
                  _row_spec(tm * FEAT_ROWS, LANES, n // tm), _full_spec((1, D_MODEL))],
        out_specs=_row_spec(tm, D_MODEL),
        out_shape=jax.ShapeDtypeStruct((n, D_MODEL), F32),
        compiler_params=_cparams(("parallel",)),
        name="final_norm",
    )(x, y, y, gain.reshape(1, D_MODEL))


def _moe_layer(n, hn, ids_pad, wts_pad, w_gate, w_up, w_down, layer):
    plan = _moe_plan(ids_pad[:2], wts_pad[:2], n)
    return _moe_experts(n, hn, plan, w_gate, w_up, w_down, layer)


def kernel(x, t5_bias, norm_mix, norm_ffn, norm_final, ev_w_in, ev_lambda, ev_subln, ev_rpb, ev_w_out,
           od_w_in, od_w_gk_fwd, od_b_gk_fwd, od_w_gk_bwd, od_b_gk_bwd, od_out_norm, od_w_out,
           moe_w_grp, moe_b_grp, moe_w_exp, moe_b_exp, moe_w_gate, moe_w_up, moe_w_down):
    b, seq, d = x.shape
    n = b * seq
    rows = seq // GRID_W
    assert d == D_MODEL and n % ROW_TILE == 0 and seq % ATT_TQ == 0 and seq % C_CHUNK == 0
    assert rows % NA_QROWS == 0 and rows >= NA_KROWS and MOE_TILE <= ROW_TILE
    x2 = x.reshape(n, d)

    (proj,) = _in_proj_call(n, [(x2, 0)], norm_mix[0], ev_w_in[0].astype(BF16))
    proj3 = proj.reshape(b, seq, proj.shape[1])
    lam_init = 0.8 - 0.6 * math.exp(-0.3 * 0)
    lp = ev_lambda[0].astype(F32)
    lam = (jnp.exp(jnp.sum(lp[0] * lp[1])) - jnp.exp(jnp.sum(lp[2] * lp[3])) + lam_init).reshape(1)
    strip = _t5_strip(t5_bias, seq, ATT_TQ)
    a_diff = _diff_attn(proj3, lam, strip, ev_subln[0], 1.0 - lam_init)

    starts, classes, _ = _na_block_geometry(rows)
    table = _na_table(ev_rpb[0], rows)
    a_na = _na_attn(proj3, table, starts, classes)

    w_out = ev_w_out[0].astype(BF16)
    w_r, b_r = _router_params(moe_w_grp[0], moe_b_grp[0], moe_w_exp[0], moe_b_exp[0])
    x1, hn, ids, wts = _out_proj_router(
        n, x2, [a_diff.reshape(n, A_W), a_na.reshape(n, B_W)], [w_out[:A_W], w_out[A_W:]],
        norm_ffn[0], w_r, b_r)
    y = _moe_layer(n, hn, ids, wts, moe_w_gate, moe_w_up, moe_w_down, 0)

    kw = C_HEADS * C_DK
    main_w = 2 * kw + 2 * C_HEADS * C_DV
    w_in = od_w_in[0]
    w_gates = jnp.pad(w_in[:, main_w:], ((0, 0), (0, LANES - 2 * C_GATE_RANK))).astype(BF16)
    wf = jnp.pad(od_w_gk_fwd[0], ((0, LANES - C_GATE_RANK), (0, 0))).astype(BF16)
    wb = jnp.pad(od_w_gk_bwd[0], ((C_GATE_RANK, LANES - 2 * C_GATE_RANK), (0, 0))).astype(BF16)
    x2b, proj, gin = _in_proj_call(
        n, [(x1, 0), (y, 0), (y, n)], norm_mix[1], w_in[:, :main_w].astype(BF16), gates=w_gates)
    a_gla = _gla(proj.reshape(b, seq, main_w), gin.reshape(b, seq, LANES), wf, od_b_gk_fwd[0].reshape(1, kw),
                 wb, od_b_gk_bwd[0].reshape(1, kw), od_out_norm[0])

    w_r, b_r = _router_params(moe_w_grp[1], moe_b_grp[1], moe_w_exp[1], moe_b_exp[1])
    x3, hn, ids, wts = _out_proj_router(
        n, x2b, [a_gla.reshape(n, C_HEADS * C_DV)], [od_w_out[0].astype(BF16)], norm_ffn[1], w_r, b_r)
    y = _moe_layer(n, hn, ids, wts, moe_w_gate, moe_w_up, moe_w_down, 1)

    return _final_norm(n, x3, y, norm_final).reshape(b, seq, d)


def _na_table(rpb, rows):
    kh = min(NA_MAX_ROWS, rows)
    starts, classes, reps = _na_block_geometry(rows)
    qn, kn = NA_QROWS * GRID_W, NA_KROWS * GRID_W
    q_row, q_col = jnp.arange(qn) // GRID_W, jnp.arange(qn) % GRID_W
    k_row, k_col = jnp.arange(kn) // GRID_W, jnp.arange(kn) % GRID_W
    col_start = jnp.clip(q_col - NA_COLS // 2, 0, GRID_W - NA_COLS)
    col_ok = (k_col[None, :] >= col_start[:, None]) & (k_col[None, :] < col_start[:, None] + NA_COLS)
    side = GRID_W - NA_COLS
    col_bias = _toeplitz(jnp.pad(rpb.astype(F32), ((0, 0), (0, 0), (side, side))), GRID_W, GRID_W)
    per_class = []
    for cls in range(len(reps)):
        j = classes.index(cls)
        q_abs = j * NA_QROWS + q_row
        k_abs = starts[j] + k_row
        row_start = jnp.clip(q_abs - kh // 2, 0, rows - kh)
        row_ok = (k_abs[None, :] >= row_start[:, None]) & (k_abs[None, :] < row_start[:, None] + kh)
        dr = jnp.clip(starts[j] + jnp.arange(NA_KROWS)[None, :] - (j * NA_QROWS + jnp.arange(NA_QROWS))[:, None]
                      + NA_MAX_ROWS - 1, 0, 2 * NA_MAX_ROWS - 2)
        blocks = col_bias[:, dr]
        bias = jnp.transpose(blocks, (0, 1, 3, 2, 4)).reshape(B_HEADS, qn, kn)
        per_class.append(jnp.where((row_ok & col_ok)[None], bias, -jnp.inf))
    t = jnp.stack(per_class, axis=1)
    return t.reshape(B_HEADS // 2, 2, len(reps), qn, kn)
```
